```python
import math
import jax, jax.numpy as jnp
from jax import lax
import numpy as np

D_MODEL = 1024
BATCH = 4
SEQ = 8192
DEPTH = 2

N_EVEN = (DEPTH + 1) // 2
N_ODD = DEPTH // 2
ROPE_THETA = 10000.0
NORM_EPS = 1e-6
Q_BLOCK = 128

HG_W = D_MODEL // 2
HGRN_HEAD_DIM = 128
HGRN_HEADS = HG_W // HGRN_HEAD_DIM
HGRN_CHUNK = 64

MLA_NOPE_DIM = 128
MLA_ROPE_DIM = 64
MLA_V_DIM = 128
MLA_HEADS = (D_MODEL - HG_W) // MLA_V_DIM
MLA_Q_LORA = 3 * D_MODEL // 8
MLA_KV_LORA = D_MODEL // 4
MLA_QK_DIM = MLA_NOPE_DIM + MLA_ROPE_DIM
MLA_SCALE = MLA_QK_DIM ** -0.5

AB_SIZES = (HG_W,) * 5 + (MLA_Q_LORA, MLA_KV_LORA, MLA_ROPE_DIM)
AB_IN = sum(AB_SIZES)
AB_SPLITS = tuple(int(s) for s in np.cumsum(AB_SIZES)[:-1])
AB_MIX = HG_W + MLA_HEADS * MLA_V_DIM

DIFF_HEAD_DIM = 64
DIFF_HEADS = D_MODEL // (2 * DIFF_HEAD_DIM)
DIFF_SCALE = DIFF_HEAD_DIM ** -0.5
C_IN = 3 * D_MODEL

FFN_HIDDEN = ((8 * D_MODEL + 3 * 256 - 1) // (3 * 256)) * 256

kernel_name = 'hgrn2_mla_diffattn_hybrid_encoder'

F32 = jnp.float32


def rmsnorm(x, gain):
    xf = x.astype(F32)
    y = xf * lax.rsqrt(jnp.mean(xf * xf, axis=-1, keepdims=True) + NORM_EPS)
    return (y * gain.astype(F32)).astype(x.dtype)


def rope_tables(seq_len, dim):
    inv = 1.0 / (ROPE_THETA ** (jnp.arange(0, dim, 2, dtype=F32) / dim))
    ang = jnp.arange(seq_len, dtype=F32)[:, None] * inv[None, :]
    return jnp.cos(ang), jnp.sin(ang)


def apply_rope(x, cos, sin):
    shape = (1, x.shape[1]) + (1,) * (x.ndim - 3) + (cos.shape[-1],)
    c = cos.reshape(shape).astype(x.dtype)
    s = sin.reshape(shape).astype(x.dtype)
    x1, x2 = jnp.split(x, 2, axis=-1)
    return jnp.concatenate([x1 * c - x2 * s, x2 * c + x1 * s], axis=-1)


def sweep_query_blocks(fn, *qs):
    b, s = qs[0].shape[:2]
    nb = s // Q_BLOCK
    blocks = tuple(jnp.moveaxis(t.reshape((b, nb, Q_BLOCK) + t.shape[2:]), 1, 0) for t in qs)
    out = lax.map(lambda a: fn(*a), blocks)
    return jnp.moveaxis(out, 0, 1).reshape((b, s) + out.shape[3:])


def hgrn2_scan(q, k, v, log_f):
    b, s, h, dk = q.shape
    dv = v.shape[-1]
    L = HGRN_CHUNK
    nc = s // L

    def to_chunks(t):
        return t.reshape(b, nc, L, h, t.shape[-1]).transpose(1, 0, 3, 2, 4).astype(F32)

    incl = jnp.tril(jnp.ones((L, L), dtype=bool))[:, :, None]

    def step(state, inp):
        qc, kc, vc, gc = inp
        cum = jnp.cumsum(gc, axis=2)
        rel = jnp.exp(jnp.where(incl, cum[:, :, :, None, :] - cum[:, :, None, :, :], -jnp.inf))
        scores = jnp.einsum('bhtk,bhtsk,bhsk->bhts', qc, rel, kc)
        o = (jnp.einsum('bhts,bhsv->bhtv', scores, vc)
             + jnp.einsum('bhtk,bhkv->bhtv', qc * jnp.exp(cum), state))
        last = cum[:, :, -1:, :]
        state = (jnp.exp(last[:, :, 0, :, None]) * state
                 + jnp.einsum('bhsk,bhsv->bhkv', kc * jnp.exp(last - cum), vc))
        return state, o

    state0 = jnp.zeros((b, h, dk, dv), F32)
    _, o = lax.scan(step, state0, (to_chunks(q), to_chunks(k), to_chunks(v), to_chunks(log_f)))
    return o.transpose(1, 0, 3, 2, 4).reshape(b, s, h, dv).astype(v.dtype)


def hgrn2_mla_mixer(h, w_in, lb_fwd, lb_bwd, hgrn_norm, q_norm, w_uq, kv_norm, w_ukv, w_out, cos, sin):
    bsz, s, _ = h.shape
    z = h @ w_in
    q_h, f_fw, f_bw, i_h, g_h, cq, ckv, kr = jnp.split(z, AB_SPLITS, axis=-1)

    def heads(t):
        return t.reshape(bsz, s, HGRN_HEADS, HGRN_HEAD_DIM)

    q = heads(jax.nn.silu(q_h)) * (HGRN_HEAD_DIM ** -0.5)
    v = heads(i_h)

    def forget(f_pre, lb):
        fp = f_pre.astype(F32)
        log_f = jnp.log(lb + (1.0 - lb) * jax.nn.sigmoid(fp))
        k = (1.0 - lb) * jax.nn.sigmoid(-fp)
        return heads(k), heads(log_f)

    k_fw, lf_fw = forget(f_fw, lb_fwd)
    k_bw, lf_bw = forget(f_bw, lb_bwd)
    flip = lambda t: jnp.flip(t, axis=1)
    o_a = (hgrn2_scan(q, k_fw, v, lf_fw)
           + flip(hgrn2_scan(flip(q), flip(k_bw), flip(v), flip(lf_bw))))
    o_a = rmsnorm(o_a, hgrn_norm.reshape(HGRN_HEADS, HGRN_HEAD_DIM))
    o_a = o_a.reshape(bsz, s, HG_W) * jax.nn.silu(g_h)

    qf = (rmsnorm(cq, q_norm) @ w_uq).reshape(bsz, s, MLA_HEADS, MLA_QK_DIM)
    q_nope = qf[..., :MLA_NOPE_DIM]
    q_rope = apply_rope(qf[..., MLA_NOPE_DIM:], cos, sin)
    kvf = (rmsnorm(ckv, kv_norm) @ w_ukv).reshape(bsz, s, MLA_HEADS, MLA_NOPE_DIM + MLA_V_DIM)
    k_nope = kvf[..., :MLA_NOPE_DIM]
    v_b = kvf[..., MLA_NOPE_DIM:]
    k_rope = apply_rope(kr, cos, sin)

    def mla_block(qn, qr):
        sc = (jnp.einsum('bqhd,bkhd->bhqk', qn, k_nope)
              + jnp.einsum('bqhd,bkd->bhqk', qr, k_rope))
        p = jax.nn.softmax(sc.astype(F32) * MLA_SCALE, axis=-1)
        return jnp.einsum('bhqk,bkhd->bqhd', p.astype(v_b.dtype), v_b)

    o_b = sweep_query_blocks(mla_block, q_nope, q_rope).reshape(bsz, s, MLA_HEADS * MLA_V_DIM)

    return jnp.concatenate([o_a, o_b], axis=-1) @ w_out


def diff_attention_mixer(h, w_in, lq1, lk1, lq2, lk2, out_norm, w_out, lambda_init, cos, sin):
    bsz, s, _ = h.shape
    q, k, v = jnp.split(h @ w_in, 3, axis=-1)
    q = apply_rope(q.reshape(bsz, s, DIFF_HEADS, 2, DIFF_HEAD_DIM), cos, sin)
    k = apply_rope(k.reshape(bsz, s, DIFF_HEADS, 2, DIFF_HEAD_DIM), cos, sin)
    v = v.reshape(bsz, s, DIFF_HEADS, 2 * DIFF_HEAD_DIM)
    lam = (jnp.exp(jnp.sum(lq1.astype(F32) * lk1.astype(F32)))
           - jnp.exp(jnp.sum(lq2.astype(F32) * lk2.astype(F32))) + lambda_init)

    def diff_block(qb):
        sc = jnp.einsum('bqhcd,bkhcd->bhcqk', qb, k).astype(F32) * DIFF_SCALE
        p = jax.nn.softmax(sc, axis=-1)
        a = p[:, :, 0] - lam * p[:, :, 1]
        return jnp.einsum('bhqk,bkhd->bqhd', a.astype(v.dtype), v)

    o = sweep_query_blocks(diff_block, q)
    o = rmsnorm(o, out_norm) * (1.0 - lambda_init)
    return o.reshape(bsz, s, DIFF_HEADS * 2 * DIFF_HEAD_DIM) @ w_out


def swiglu(h, w_gate, w_up, w_down):
    return (jax.nn.silu(h @ w_gate) * (h @ w_up)) @ w_down


def setup_inputs(seed: int = 0) -> dict:
    key = jax.random.key(seed)
    ks = jax.random.split(key, 22)

    def w(i, shape, fan_in):
        return jax.random.normal(ks[i], shape, F32) * (fan_in ** -0.5)

    def gain(i, shape):
        return 1.0 + 0.02 * jax.random.normal(ks[i], shape, F32)

    return {
        'x': jax.random.normal(ks[0], (BATCH, SEQ, D_MODEL), F32),
        'norm_attn': gain(1, (DEPTH, D_MODEL)),
        'norm_ffn': gain(2, (DEPTH, D_MODEL)),
        'ffn_w_gate': w(3, (DEPTH, D_MODEL, FFN_HIDDEN), D_MODEL),
        'ffn_w_up': w(4, (DEPTH, D_MODEL, FFN_HIDDEN), D_MODEL),
        'ffn_w_down': w(5, (DEPTH, FFN_HIDDEN, D_MODEL), FFN_HIDDEN),
        'ab_w_in': w(6, (N_EVEN, D_MODEL, AB_IN), D_MODEL),
        'hgrn_lower_bound': 0.5 * jax.random.normal(ks[7], (2, N_EVEN + 1, HG_W), F32),
        'hgrn_out_norm': gain(8, (N_EVEN, HG_W)),
        'mla_q_norm': gain(9, (N_EVEN, MLA_Q_LORA)),
        'mla_w_uq': w(10, (N_EVEN, MLA_Q_LORA, MLA_HEADS * MLA_QK_DIM), MLA_Q_LORA),
        'mla_kv_norm': gain(11, (N_EVEN, MLA_KV_LORA)),
        'mla_w_ukv': w(12, (N_EVEN, MLA_KV_LORA, MLA_HEADS * (MLA_NOPE_DIM + MLA_V_DIM)), MLA_KV_LORA),
        'ab_w_out': w(13, (N_EVEN, AB_MIX, D_MODEL), AB_MIX),
        'c_w_in': w(14, (N_ODD, D_MODEL, C_IN), D_MODEL),
        'diff_lambda_q1': 0.1 * jax.random.normal(ks[15], (N_ODD, DIFF_HEAD_DIM), F32),
        'diff_lambda_k1': 0.1 * jax.random.normal(ks[16], (N_ODD, DIFF_HEAD_DIM), F32),
        'diff_lambda_q2': 0.1 * jax.random.normal(ks[17], (N_ODD, DIFF_HEAD_DIM), F32),
        'diff_lambda_k2': 0.1 * jax.random.normal(ks[18], (N_ODD, DIFF_HEAD_DIM), F32),
        'diff_out_norm': gain(19, (N_ODD, 2 * DIFF_HEAD_DIM)),
        'c_w_out': w(20, (N_ODD, D_MODEL, D_MODEL), D_MODEL),
        'final_norm': gain(21, (D_MODEL,)),
    }


def reference(x, norm_attn, norm_ffn, ffn_w_gate, ffn_w_up, ffn_w_down, ab_w_in, hgrn_lower_bound,
              hgrn_out_norm, mla_q_norm, mla_w_uq, mla_kv_norm, mla_w_ukv, ab_w_out, c_w_in,
              diff_lambda_q1, diff_lambda_k1, diff_lambda_q2, diff_lambda_k2, diff_out_norm,
              c_w_out, final_norm):
    seq_len = x.shape[1]
    cos_mla, sin_mla = rope_tables(seq_len, MLA_ROPE_DIM)
    cos_diff, sin_diff = rope_tables(seq_len, DIFF_HEAD_DIM)
    lower_bounds = jnp.cumsum(jax.nn.softmax(hgrn_lower_bound.astype(F32), axis=1), axis=1)

    for layer in range(DEPTH):
        j = layer // 2
        h = rmsnorm(x, norm_attn[layer])
        if layer % 2 == 0:
            mix = hgrn2_mla_mixer(h, ab_w_in[j], lower_bounds[0, j], lower_bounds[1, j],
                                  hgrn_out_norm[j], mla_q_norm[j], mla_w_uq[j], mla_kv_norm[j],
                                  mla_w_ukv[j], ab_w_out[j], cos_mla, sin_mla)
        else:
            lambda_init = 0.8 - 0.6 * math.exp(-0.3 * layer)
            mix = diff_attention_mixer(h, c_w_in[j], diff_lambda_q1[j], diff_lambda_k1[j],
                                       diff_lambda_q2[j], diff_lambda_k2[j], diff_out_norm[j],
                                       c_w_out[j], lambda_init, cos_diff, sin_diff)
        x = x + mix
        x = x + swiglu(rmsnorm(x, norm_ffn[layer]), ffn_w_gate[layer], ffn_w_up[layer], ffn_w_down[layer])
    return rmsnorm(x, final_norm)
```

```python
import functools
import math

import jax
import jax.numpy as jnp
from jax import lax
from jax.experimental import pallas as pl
from jax.experimental.pallas import tpu as pltpu

F32 = jnp.float32
BF16 = jnp.bfloat16

NORM_EPS = 1e-6
ROPE_THETA = 10000.0
LANES = 128
VMEM_LIMIT = 56 * 1024 * 1024

HGRN_HEADS = 4
HGRN_DIM = 128
HGRN_W = HGRN_HEADS * HGRN_DIM
HGRN_CHUNK = 64
HGRN_BLOCK = 256

MLA_HEADS = 4
MLA_NOPE = 128
MLA_ROPE = 64
MLA_V = 128
MLA_Q_LORA = 384
MLA_KV_LORA = 256
MLA_QK_PAD = 256
MLA_SCALE = (MLA_NOPE + MLA_ROPE) ** -0.5

DIFF_HEADS = 8
DIFF_DIM = 64
DIFF_SCALE = DIFF_DIM ** -0.5


def _rms(x, gain):
    ms = jnp.mean(x * x, axis=-1, keepdims=True)
    return x * lax.rsqrt(ms + NORM_EPS) * gain


def _params(*sem):
    return pltpu.CompilerParams(dimension_semantics=sem, vmem_limit_bytes=VMEM_LIMIT)


def _full(shape):
    n = len(shape)
    return pl.BlockSpec(shape, lambda *_: (0,) * n)


def _proj0_kernel(x_ref, g_ref, wh_ref, wm_ref, qg_ref, wqn_ref, wqr_ref, wqs_ref, kvg_ref, wkv_ref,
                  cos_ref, sin_ref, zh_ref, q_ref, k_ref, vt_ref):
    xn = _rms(x_ref[0], g_ref[...]).astype(BF16)
    zh_ref[0] = jnp.dot(xn, wh_ref[...], preferred_element_type=F32)
    zm = jnp.dot(xn, wm_ref[...], preferred_element_type=F32)
    cos = cos_ref[...]
    sin = sin_ref[...]
    c0 = MLA_Q_LORA
    c1 = c0 + MLA_KV_LORA
    cqn = _rms(zm[:, :c0], qg_ref[...]).astype(BF16)
    ckvn = _rms(zm[:, c0:c1], kvg_ref[...]).astype(BF16)
    k_rope = (zm[:, c1:c1 + LANES] * cos + zm[:, c1 + LANES:c1 + 2 * LANES] * sin).astype(BF16)
    qn = jnp.dot(cqn, wqn_ref[...], preferred_element_type=F32)
    qr = jnp.dot(cqn, wqr_ref[...], preferred_element_type=F32)
    qs = jnp.dot(cqn, wqs_ref[...], preferred_element_type=F32)
    kv = jnp.dot(ckvn, wkv_ref[...], preferred_element_type=F32)
    hw = MLA_HEADS * MLA_NOPE
    for h in range(MLA_HEADS):
        hs = slice(h * LANES, (h + 1) * LANES)
        q_ref[0, h, :, 0:LANES] = (qn[:, hs] * MLA_SCALE).astype(BF16)
        q_ref[0, h, :, LANES:2 * LANES] = ((qr[:, hs] * cos + qs[:, hs] * sin) * MLA_SCALE).astype(BF16)
        k_ref[0, h, :, 0:LANES] = kv[:, hs].astype(BF16)
        k_ref[0, h, :, LANES:2 * LANES] = k_rope
        vt_ref[0, h] = kv[:, hw + h * LANES:hw + (h + 1) * LANES].T.astype(BF16)


def _proj0(x, gain, wh, wm, qg, wqn, wqr, wqs, kvg, wkv, cos_t, sin_t, tm):
    b, s, d = x.shape
    grid = (b, s // tm)
    return pl.pallas_call(
        _proj0_kernel,
        grid=grid,
        in_specs=[
            pl.BlockSpec((1, tm, d), lambda i, j: (i, j, 0)),
            _full(gain.shape), _full(wh.shape), _full(wm.shape), _full(qg.shape), _full(wqn.shape),
            _full(wqr.shape), _full(wqs.shape), _full(kvg.shape), _full(wkv.shape),
            pl.BlockSpec((tm, LANES), lambda i, j: (j, 0)),
            pl.BlockSpec((tm, LANES), lambda i, j: (j, 0)),
        ],
        out_specs=[
            pl.BlockSpec((1, tm, wh.shape[1]), lambda i, j: (i, j, 0)),
            pl.BlockSpec((1, MLA_HEADS, tm, MLA_QK_PAD), lambda i, j: (i, 0, j, 0)),
            pl.BlockSpec((1, MLA_HEADS, tm, MLA_QK_PAD), lambda i, j: (i, 0, j, 0)),
            pl.BlockSpec((1, MLA_HEADS, MLA_V, tm), lambda i, j: (i, 0, 0, j)),
        ],
        out_shape=[
            jax.ShapeDtypeStruct((b, s, wh.shape[1]), F32),
            jax.ShapeDtypeStruct((b, MLA_HEADS, s, MLA_QK_PAD), BF16),
            jax.ShapeDtypeStruct((b, MLA_HEADS, s, MLA_QK_PAD), BF16),
            jax.ShapeDtypeStruct((b, MLA_HEADS, MLA_V, s), BF16),
        ],
        compiler_params=_params("parallel", "parallel"),
        name="proj0",
    )(x, gain, wh, wm, qg, wqn, wqr, wqs, kvg, wkv, cos_t, sin_t)


def _proj1_kernel(x_ref, g_ref, w_ref, cos_ref, sin_ref, q_ref, k_ref, vt_ref):
    d = x_ref.shape[2]
    xn = _rms(x_ref[0], g_ref[...]).astype(BF16)
    z = jnp.dot(xn, w_ref[...], preferred_element_type=F32)
    cos = cos_ref[...]
    sin = sin_ref[...]
    lane = lax.broadcasted_iota(jnp.int32, cos.shape, 1)
    first_half = (lane % DIFF_DIM) < (DIFF_DIM // 2)
    half = DIFF_DIM // 2
    for h in range(DIFF_HEADS):
        hs = slice(h * LANES, (h + 1) * LANES)
        for base, ref, scale in ((0, q_ref, DIFF_SCALE), (d, k_ref, 1.0)):
            t = z[:, base + h * LANES:base + (h + 1) * LANES]
            partner = jnp.where(first_half, pltpu.roll(t, LANES - half, axis=1), pltpu.roll(t, half, axis=1))
            ref[0, :, hs] = ((t * cos + partner * sin) * scale).astype(BF16)
        vt_ref[0, h] = z[:, 2 * d + h * LANES:2 * d + (h + 1) * LANES].T.astype(BF16)


def _proj1(x, gain, w, cos_t, sin_t, tm):
    b, s, d = x.shape
    return pl.pallas_call(
        _proj1_kernel,
        grid=(b, s // tm),
        in_specs=[
            pl.BlockSpec((1, tm, d), lambda i, j: (i, j, 0)),
            _full(gain.shape), _full(w.shape),
            pl.BlockSpec((tm, LANES), lambda i, j: (j, 0)),
            pl.BlockSpec((tm, LANES), lambda i, j: (j, 0)),
        ],
        out_specs=[
            pl.BlockSpec((1, tm, d), lambda i, j: (i, j, 0)),
            pl.BlockSpec((1, tm, d), lambda i, j: (i, j, 0)),
            pl.BlockSpec((1, DIFF_HEADS, 2 * DIFF_DIM, tm), lambda i, j: (i, 0, 0, j)),
        ],
        out_shape=[
            jax.ShapeDtypeStruct((b, s, d), BF16),
            jax.ShapeDtypeStruct((b, s, d), BF16),
            jax.ShapeDtypeStruct((b, DIFF_HEADS, 2 * DIFF_DIM, s), BF16),
        ],
        compiler_params=_params("parallel", "parallel"),
        name="proj1",
    )(x, gain, w, cos_t, sin_t)


def _hgrn_chunk(q, k, v, g, st_ref, rev):
    n = HGRN_CHUNK
    row = lax.broadcasted_iota(jnp.int32, g.shape, 0)
    blk = lax.broadcasted_iota(jnp.int32, (n, n), 0)
    blk_t = lax.broadcasted_iota(jnp.int32, (n, n), 1)
    p_sum = g
    x_sum = jnp.zeros_like(g)
    levels = []
    m = 1
    while m < n:
        in_right = (row % (2 * m)) >= m
        q_rows = jnp.logical_not(in_right) if rev else in_right
        qt = jnp.where(q_rows, q * jnp.exp(p_sum), 0.0).astype(BF16)
        kt = jnp.where(q_rows, 0.0, k * jnp.exp(x_sum)).astype(BF16)
        same_block = (blk // (2 * m)) == (blk_t // (2 * m)) if 2 * m < n else None
        levels.append((qt, kt, same_block))
        total = p_sum + x_sum
        from_left = pltpu.roll(total, m, axis=0)
        from_right = pltpu.roll(total, n - m, axis=0)
        if rev:
            p_sum = p_sum + jnp.where(in_right, 0.0, from_right)
            x_sum = x_sum + jnp.where(in_right, from_left, 0.0)
        else:
            p_sum = p_sum + jnp.where(in_right, from_left, 0.0)
            x_sum = x_sum + jnp.where(in_right, 0.0, from_right)
        m *= 2
    q_in = (q * jnp.exp(p_sum)).astype(BF16)
    k_out = (k * jnp.exp(x_sum)).astype(BF16)
    chunk_decay = jnp.exp(p_sum[0:1] + x_sum[0:1])
    qk = q * k
    vb = v.astype(BF16)
    nt = (((1,), (1,)), ((), ()))
    tn = (((0,), (0,)), ((), ()))
    outs = []
    for h in range(HGRN_HEADS):
        hs = slice(h * HGRN_DIM, (h + 1) * HGRN_DIM)
        a = None
        for qt, kt, same_block in levels:
            part = lax.dot_general(qt[:, hs], kt[:, hs], nt, preferred_element_type=F32)
            if same_block is not None:
                part = jnp.where(same_block, part, 0.0)
            a = part if a is None else a + part
        st = st_ref[h]
        o = jnp.dot(a.astype(BF16), vb[:, hs], preferred_element_type=F32)
        o = o + lax.dot_general(q_in[:, hs], st.astype(BF16), nt, preferred_element_type=F32)
        o = o + jnp.sum(qk[:, hs], axis=1, keepdims=True) * v[:, hs]
        st_ref[h] = st * chunk_decay[:, hs] + lax.dot_general(vb[:, hs], k_out[:, hs], tn,
                                                               preferred_element_type=F32)
        outs.append(o)
    return outs


def _hgrn_gates(qh, fp, lb):
    q = qh * jax.nn.sigmoid(qh) * (HGRN_DIM ** -0.5)
    g = jnp.log(lb + (1.0 - lb) * jax.nn.sigmoid(fp))
    k = (1.0 - lb) * jax.nn.sigmoid(-fp)
    return q, k, g


def _hgrn_fwd_kernel(q_ref, f_ref, v_ref, lb_ref, o_ref, st_ref):
    @pl.when(pl.program_id(1) == 0)
    def _():
        st_ref[...] = jnp.zeros_like(st_ref)

    lb = lb_ref[...]
    for c in range(HGRN_BLOCK // HGRN_CHUNK):
        rs = slice(c * HGRN_CHUNK, (c + 1) * HGRN_CHUNK)
        q, k, g = _hgrn_gates(q_ref[0, rs, :], f_ref[0, rs, :], lb)
        outs = _hgrn_chunk(q, k, v_ref[0, rs, :], g, st_ref, False)
        for h, o in enumerate(outs):
            o_ref[0, rs, h * HGRN_DIM:(h + 1) * HGRN_DIM] = o


def _hgrn_bwd_kernel(q_ref, f_ref, v_ref, gate_ref, of_ref, lb_ref, ng_ref, o_ref, st_ref):
    @pl.when(pl.program_id(1) == 0)
    def _():
        st_ref[...] = jnp.zeros_like(st_ref)

    lb = lb_ref[...]
    for c in reversed(range(HGRN_BLOCK // HGRN_CHUNK)):
        rs = slice(c * HGRN_CHUNK, (c + 1) * HGRN_CHUNK)
        q, k, g = _hgrn_gates(q_ref[0, rs, :], f_ref[0, rs, :], lb)
        outs = _hgrn_chunk(q, k, v_ref[0, rs, :], g, st_ref, True)
        gate = gate_ref[0, rs, :]
        gate = gate * jax.nn.sigmoid(gate)
        for h, o in enumerate(outs):
            hs = slice(h * HGRN_DIM, (h + 1) * HGRN_DIM)
            y = _rms(o + of_ref[0, rs, hs], ng_ref[:, hs])
            o_ref[0, rs, hs] = (y * gate[:, hs]).astype(BF16)


def _hgrn(zh, lb_fwd, lb_bwd, norm_gain):
    b, s, _ = zh.shape
    nb = s // HGRN_BLOCK
    state = pltpu.VMEM((HGRN_HEADS, HGRN_DIM, HGRN_DIM), F32)

    def col(c, rev):
        if rev:
            return pl.BlockSpec((1, HGRN_BLOCK, HGRN_W), lambda i, j: (i, nb - 1 - j, c))
        return pl.BlockSpec((1, HGRN_BLOCK, HGRN_W), lambda i, j: (i, j, c))

    o_fwd = pl.pallas_call(
        _hgrn_fwd_kernel,
        grid=(b, nb),
        in_specs=[col(0, False), col(1, False), col(3, False), _full(lb_fwd.shape)],
        out_specs=col(0, False),
        out_shape=jax.ShapeDtypeStruct((b, s, HGRN_W), F32),
        scratch_shapes=[state],
        compiler_params=_params("parallel", "arbitrary"),
        name="hgrn_fwd",
    )(zh, zh, zh, lb_fwd)
    return pl.pallas_call(
        _hgrn_bwd_kernel,
        grid=(b, nb),
        in_specs=[col(0, True), col(2, True), col(3, True), col(4, True), col(0, True),
                  _full(lb_bwd.shape), _full(norm_gain.shape)],
        out_specs=col(0, True),
        out_shape=jax.ShapeDtypeStruct((b, s, HGRN_W), BF16),
        scratch_shapes=[state],
        compiler_params=_params("parallel", "arbitrary"),
        name="hgrn_bwd",
    )(zh, zh, zh, zh, o_fwd, lb_bwd, norm_gain)


def _flash_loop(k_slice, vt_slice, qs_ref, m_ref, l_ref, acc_ref, n_kv, tk):
    m_ref[...] = jnp.full(m_ref.shape, -jnp.inf, F32)
    l_ref[...] = jnp.zeros_like(l_ref)
    acc_ref[...] = jnp.zeros_like(acc_ref)

    def body(j, carry):
        off = pl.multiple_of(j * tk, tk)
        s = lax.dot_general(k_slice(off), qs_ref[...], (((1,), (1,)), ((), ())),
                            preferred_element_type=F32)
        m_old = m_ref[...]
        m_new = jnp.maximum(m_old, jnp.max(s, axis=0, keepdims=True))
        alpha = jnp.exp(m_old - m_new)
        p = jnp.exp(s - m_new)
        l_ref[...] = alpha * l_ref[...] + jnp.sum(p, axis=0, keepdims=True)
        acc_ref[...] = alpha * acc_ref[...] + jnp.dot(vt_slice(off), p.astype(BF16),
                                                      preferred_element_type=F32)
        m_ref[...] = m_new
        return carry

    lax.fori_loop(0, n_kv, body, 0)


def _mla_attn_kernel(q_ref, k_ref, vt_ref, o_ref, qs_ref, m_ref, l_ref, acc_ref, *, tk):
    s = k_ref.shape[2]
    qs_ref[...] = q_ref[0, 0]
    _flash_loop(lambda off: k_ref[0, 0, pl.ds(off, tk), :],
                lambda off: vt_ref[0, 0, :, pl.ds(off, tk)],
                qs_ref, m_ref, l_ref, acc_ref, s // tk, tk)
    out_t = acc_ref[...] / l_ref[...]
    o_ref[0] = out_t.T.astype(BF16)


def _mla_attn(q, k, vt, tq, tk):
    b, h, s, d = q.shape
    dv = vt.shape[2]
    return pl.pallas_call(
        functools.partial(_mla_attn_kernel, tk=tk),
        grid=(b, h, s // tq),
        in_specs=[
            pl.BlockSpec((1, 1, tq, d), lambda i, j, t: (i, j, t, 0)),
            pl.BlockSpec((1, 1, s, d), lambda i, j, t: (i, j, 0, 0)),
            pl.BlockSpec((1, 1, dv, s), lambda i, j, t: (i, j, 0, 0)),
        ],
        out_specs=pl.BlockSpec((1, tq, dv), lambda i, j, t: (i, t, j)),
        out_shape=jax.ShapeDtypeStruct((b, s, h * dv), BF16),
        scratch_shapes=[pltpu.VMEM((tq, d), BF16), pltpu.VMEM((1, tq), F32), pltpu.VMEM((1, tq), F32),
                        pltpu.VMEM((dv, tq), F32)],
        compiler_params=_params("parallel", "parallel", "arbitrary"),
        name="mla_attn",
    )(q, k, vt)


def _diff_attn_kernel(q_ref, k_ref, vt_ref, lq1_ref, lk1_ref, lq2_ref, lk2_ref, ng_ref, o_ref,
                      qs_ref, m_ref, l_ref, acc_ref, *, tk, lambda_init):
    s = k_ref.shape[1]
    tq = q_ref.shape[1]
    q = q_ref[0]
    lane = lax.broadcasted_iota(jnp.int32, q.shape, 1)
    zero = jnp.zeros_like(q)
    qs_ref[0:tq, :] = jnp.where(lane < DIFF_DIM, q, zero)
    qs_ref[tq:2 * tq, :] = jnp.where(lane < DIFF_DIM, zero, q)
    _flash_loop(lambda off: k_ref[0, pl.ds(off, tk), :],
                lambda off: vt_ref[0, 0, :, pl.ds(off, tk)],
                qs_ref, m_ref, l_ref, acc_ref, s // tk, tk)
    lam = (jnp.exp(jnp.sum(lq1_ref[...] * lk1_ref[...], axis=1, keepdims=True))
           - jnp.exp(jnp.sum(lq2_ref[...] * lk2_ref[...], axis=1, keepdims=True)) + lambda_init)
    soft = acc_ref[...] / l_ref[...]
    out_t = soft[:, 0:tq] - lam * soft[:, tq:2 * tq]
    y = _rms(out_t.T, ng_ref[...]) * (1.0 - lambda_init)
    o_ref[0] = y.astype(BF16)


def _diff_attn(q, k, vt, lq1, lk1, lq2, lk2, norm_gain, lambda_init, tq, tk):
    b, s, d = q.shape
    h = vt.shape[1]
    dv = vt.shape[2]
    small = _full(lq1.shape)
    return pl.pallas_call(
        functools.partial(_diff_attn_kernel, tk=tk, lambda_init=lambda_init),
        grid=(b, h, s // tq),
        in_specs=[
            pl.BlockSpec((1, tq, dv), lambda i, j, t: (i, t, j)),
            pl.BlockSpec((1, s, dv), lambda i, j, t: (i, 0, j)),
            pl.BlockSpec((1, 1, dv, s), lambda i, j, t: (i, j, 0, 0)),
            small, small, small, small, _full(norm_gain.shape),
        ],
        out_specs=pl.BlockSpec((1, tq, dv), lambda i, j, t: (i, t, j)),
        out_shape=jax.ShapeDtypeStruct((b, s, d), BF16),
        scratch_shapes=[pltpu.VMEM((2 * tq, dv), BF16), pltpu.VMEM((1, 2 * tq), F32),
                        pltpu.VMEM((1, 2 * tq), F32), pltpu.VMEM((dv, 2 * tq), F32)],
        compiler_params=_params("parallel", "parallel", "arbitrary"),
        name="diff_attn",
    )(q, k, vt, lq1, lk1, lq2, lk2, norm_gain)


def _mix_ffn_kernel(*refs, n_mix, final):
    x_ref = refs[0]
    mix_refs = refs[1:1 + n_mix]
    wo_refs = refs[1 + n_mix:1 + 2 * n_mix]
    g_ref, wg_ref, wu_ref, wd_ref, gf_ref, out_ref, x1_ref, xn_ref, acc_ref = refs[1 + 2 * n_mix:]
    kk = pl.program_id(1)

    @pl.when(kk == 0)
    def _():
        x1 = x_ref[...]
        for m_ref, w_ref in zip(mix_refs, wo_refs):
            x1 = x1 + jnp.dot(m_ref[...], w_ref[...], preferred_element_type=F32)
        x1_ref[...] = x1
        xn_ref[...] = _rms(x1, g_ref[...]).astype(BF16)
        acc_ref[...] = jnp.zeros_like(acc_ref)

    xn = xn_ref[...]
    gate = jnp.dot(xn, wg_ref[...], preferred_element_type=F32)
    up = jnp.dot(xn, wu_ref[...], preferred_element_type=F32)
    hid = (gate * jax.nn.sigmoid(gate) * up).astype(BF16)
    acc_ref[...] += jnp.dot(hid, wd_ref[...], preferred_element_type=F32)

    @pl.when(kk == pl.num_programs(1) - 1)
    def _():
        y = x1_ref[...] + acc_ref[...]
        if final:
            y = _rms(y, gf_ref[...])
        out_ref[...] = y


def _mix_ffn(x, mixes, w_outs, gain, wg, wu, wd, final_gain, final, tm, th):
    t, d = x.shape
    hidden = wg.shape[1]
    n_mix = len(mixes)
    row = lambda i, k: (i, 0)
    in_specs = [pl.BlockSpec((tm, d), row)]
    in_specs += [pl.BlockSpec((tm, m.shape[1]), row) for m in mixes]
    in_specs += [_full(w.shape) for w in w_outs]
    in_specs += [
        _full(gain.shape),
        pl.BlockSpec((d, th), lambda i, k: (0, k)),
        pl.BlockSpec((d, th), lambda i, k: (0, k)),
        pl.BlockSpec((th, d), lambda i, k: (k, 0)),
        _full(final_gain.shape),
    ]
    return pl.pallas_call(
        functools.partial(_mix_ffn_kernel, n_mix=n_mix, final=final),
        grid=(t // tm, hidden // th),
        in_specs=in_specs,
        out_specs=pl.BlockSpec((tm, d), row),
        out_shape=jax.ShapeDtypeStruct((t, d), F32),
        scratch_shapes=[pltpu.VMEM((tm, d), F32), pltpu.VMEM((tm, d), BF16), pltpu.VMEM((tm, d), F32)],
        compiler_params=_params("parallel", "arbitrary"),
        name="mix_ffn",
    )(x, *mixes, *w_outs, gain, wg, wu, wd, final_gain)


def _rope_tables(seq_len):
    dim = MLA_ROPE
    inv = 1.0 / (ROPE_THETA ** (jnp.arange(0, dim, 2, dtype=F32) / dim))
    ang = jnp.arange(seq_len, dtype=F32)[:, None] * inv[None, :]
    cos, sin = jnp.cos(ang), jnp.sin(ang)
    cos_t = jnp.concatenate([cos, cos, cos, cos], axis=1)
    sin_t = jnp.concatenate([-sin, sin, -sin, sin], axis=1)
    return cos_t, sin_t


def _swap_halves(w):
    half = w.shape[-1] // 2
    return jnp.concatenate([w[..., half:], w[..., :half]], axis=-1)


def _pad_heads(w):
    kdim, h, r = w.shape
    return jnp.concatenate([w, jnp.zeros_like(w)], axis=-1).reshape(kdim, h * 2 * r)


def _row(v):
    return v.reshape(1, -1).astype(F32)


def kernel(x, norm_attn, norm_ffn, ffn_w_gate, ffn_w_up, ffn_w_down, ab_w_in, hgrn_lower_bound, hgrn_out_norm,
           mla_q_norm, mla_w_uq, mla_kv_norm, mla_w_ukv, ab_w_out, c_w_in, diff_lambda_q1, diff_lambda_k1,
           diff_lambda_q2, diff_lambda_k2, diff_out_norm, c_w_out, final_norm):
    b, s, d = x.shape
    depth = norm_attn.shape[0]
    assert DIFF_DIM == MLA_ROPE and d == DIFF_HEADS * 2 * DIFF_DIM and d == 2 * HGRN_W
    assert s % 512 == 0
    tm_proj = 512
    tm_ffn = 1024 if (b * s) % 1024 == 0 else 512
    th_ffn = 256
    tq_mla, tq_diff, tk = 512, 256, 512

    cos_t, sin_t = _rope_tables(s)
    lower_bounds = jnp.cumsum(jax.nn.softmax(hgrn_lower_bound.astype(F32), axis=1), axis=1)

    for layer in range(depth):
        j = layer // 2
        gain = _row(norm_attn[layer])
        if layer % 2 == 0:
            w_in = ab_w_in[j]
            c_h = 5 * HGRN_W
            c_q = c_h + MLA_Q_LORA
            c_kv = c_q + MLA_KV_LORA
            w_kr = w_in[:, c_kv:]
            w_krs = _swap_halves(w_kr)
            wh = w_in[:, :c_h].astype(BF16)
            wm = jnp.concatenate([w_in[:, c_h:c_kv], w_kr, w_kr, w_krs, w_krs], axis=1).astype(BF16)
            wuq = mla_w_uq[j].reshape(MLA_Q_LORA, MLA_HEADS, MLA_NOPE + MLA_ROPE)
            wqn = wuq[..., :MLA_NOPE].reshape(MLA_Q_LORA, MLA_HEADS * MLA_NOPE).astype(BF16)
            wqr = _pad_heads(wuq[..., MLA_NOPE:]).astype(BF16)
            wqs = _pad_heads(_swap_halves(wuq[..., MLA_NOPE:])).astype(BF16)
            wukv = mla_w_ukv[j].reshape(MLA_KV_LORA, MLA_HEADS, MLA_NOPE + MLA_V)
            wkv = jnp.concatenate([wukv[..., :MLA_NOPE].reshape(MLA_KV_LORA, -1),
                                   wukv[..., MLA_NOPE:].reshape(MLA_KV_LORA, -1)], axis=1).astype(BF16)
            zh, q, k, vt = _proj0(x, gain, wh, wm, _row(mla_q_norm[j]), wqn, wqr, wqs, _row(mla_kv_norm[j]),
                                  wkv, cos_t, sin_t, tm_proj)
            o_a = _hgrn(zh, _row(lower_bounds[0, j]), _row(lower_bounds[1, j]), _row(hgrn_out_norm[j]))
            o_b = _mla_attn(q, k, vt, tq_mla, tk)
            w_out = ab_w_out[j].astype(BF16)
            mixes = [o_a.reshape(b * s, HGRN_W), o_b.reshape(b * s, MLA_HEADS * MLA_V)]
            w_outs = [w_out[:HGRN_W], w_out[HGRN_W:]]
        else:
            lambda_init = 0.8 - 0.6 * math.exp(-0.3 * layer)
            q, k, vt = _proj1(x, gain, c_w_in[j].astype(BF16), cos_t, sin_t, tm_proj)
            o_c = _diff_attn(q, k, vt, _row(diff_lambda_q1[j]), _row(diff_lambda_k1[j]),
                             _row(diff_lambda_q2[j]), _row(diff_lambda_k2[j]), _row(diff_out_norm[j]),
                             lambda_init, tq_diff, tk)
            mixes = [o_c.reshape(b * s, d)]
            w_outs = [c_w_out[j].astype(BF16)]
        x = _mix_ffn(x.reshape(b * s, d), mixes, w_outs, _row(norm_ffn[layer]),
                     ffn_w_gate[layer].astype(BF16), ffn_w_up[layer].astype(BF16),
                     ffn_w_down[layer].astype(BF16), _row(final_norm),
                     layer == depth - 1, tm_ffn, th_ffn).reshape(b, s, d)
    return x
```

```python
import functools
import math

import jax
import jax.numpy as jnp
from jax import lax
from jax.experimental import pallas as pl
from jax.experimental.pallas import tpu as pltpu

F32 = jnp.float32
BF16 = jnp.bfloat16

NORM_EPS = 1e-6
ROPE_THETA = 10000.0
LANES = 128
VMEM_LIMIT = 56 * 1024 * 1024

HGRN_HEADS = 4
HGRN_DIM = 128
HGRN_W = HGRN_HEADS * HGRN_DIM
HGRN_CHUNK = 64
HGRN_BLOCK = 256

MLA_HEADS = 4
MLA_NOPE = 128
MLA_ROPE = 64
MLA_V = 128
MLA_Q_LORA = 384
MLA_KV_LORA = 256
MLA_QK_PAD = 256
LOG2_E = math.log2(math.e)
MLA_SCALE = (MLA_NOPE + MLA_ROPE) ** -0.5 * LOG2_E

DIFF_HEADS = 8
DIFF_DIM = 64
DIFF_SCALE = DIFF_DIM ** -0.5 * LOG2_E


def _rms(x, gain):
    ms = jnp.mean(x * x, axis=-1, keepdims=True)
    return x * lax.rsqrt(ms + NORM_EPS) * gain


def _params(*sem):
    return pltpu.CompilerParams(dimension_semantics=sem, vmem_limit_bytes=VMEM_LIMIT)


def _full(shape):
    n = len(shape)
    return pl.BlockSpec(shape, lambda *_: (0,) * n)


def _proj0_kernel(x_ref, g_ref, wh_ref, wm_ref, qg_ref, wqn_ref, wqr_ref, wqs_ref, kvg_ref, wkv_ref,
                  cos_ref, sin_ref, zh_ref, q_ref, k_ref, vt_ref):
    xn = _rms(x_ref[0], g_ref[...]).astype(BF16)
    zh_ref[0] = jnp.dot(xn, wh_ref[...], preferred_element_type=F32)
    zm = jnp.dot(xn, wm_ref[...], preferred_element_type=F32)
    cos = cos_ref[...]
    sin = sin_ref[...]
    c0 = MLA_Q_LORA
    c1 = c0 + MLA_KV_LORA
    cqn = _rms(zm[:, :c0], qg_ref[...]).astype(BF16)
    ckvn = _rms(zm[:, c0:c1], kvg_ref[...]).astype(BF16)
    k_rope = (zm[:, c1:c1 + LANES] * cos + zm[:, c1 + LANES:c1 + 2 * LANES] * sin).astype(BF16)
    qn = jnp.dot(cqn, wqn_ref[...], preferred_element_type=F32)
    qr = jnp.dot(cqn, wqr_ref[...], preferred_element_type=F32)
    qs = jnp.dot(cqn, wqs_ref[...], preferred_element_type=F32)
    kv = jnp.dot(ckvn, wkv_ref[...], preferred_element_type=F32)
    hw = MLA_HEADS * MLA_NOPE
    for h in range(MLA_HEADS):
        hs = slice(h * LANES, (h + 1) * LANES)
        q_ref[0, h, :, 0:LANES] = (qn[:, hs] * MLA_SCALE).astype(BF16)
        q_ref[0, h, :, LANES:2 * LANES] = ((qr[:, hs] * cos + qs[:, hs] * sin) * MLA_SCALE).astype(BF16)
        k_ref[0, h, :, 0:LANES] = kv[:, hs].astype(BF16)
        k_ref[0, h, :, LANES:2 * LANES] = k_rope
        vt_ref[0, h] = kv[:, hw + h * LANES:hw + (h + 1) * LANES].T.astype(BF16)


def _proj0(x, gain, wh, wm, qg, wqn, wqr, wqs, kvg, wkv, cos_t, sin_t, tm):
    b, s, d = x.shape
    grid = (b, s // tm)
    return pl.pallas_call(
        _proj0_kernel,
        grid=grid,
        in_specs=[
            pl.BlockSpec((1, tm, d), lambda i, j: (i, j, 0)),
            _full(gain.shape), _full(wh.shape), _full(wm.shape), _full(qg.shape), _full(wqn.shape),
            _full(wqr.shape), _full(wqs.shape), _full(kvg.shape), _full(wkv.shape),
            pl.BlockSpec((tm, LANES), lambda i, j: (j, 0)),
            pl.BlockSpec((tm, LANES), lambda i, j: (j, 0)),
        ],
        out_specs=[
            pl.BlockSpec((1, tm, wh.shape[1]), lambda i, j: (i, j, 0)),
            pl.BlockSpec((1, MLA_HEADS, tm, MLA_QK_PAD), lambda i, j: (i, 0, j, 0)),
            pl.BlockSpec((1, MLA_HEADS, tm, MLA_QK_PAD), lambda i, j: (i, 0, j, 0)),
            pl.BlockSpec((1, MLA_HEADS, MLA_V, tm), lambda i, j: (i, 0, 0, j)),
        ],
        out_shape=[
            jax.ShapeDtypeStruct((b, s, wh.shape[1]), F32),
            jax.ShapeDtypeStruct((b, MLA_HEADS, s, MLA_QK_PAD), BF16),
            jax.ShapeDtypeStruct((b, MLA_HEADS, s, MLA_QK_PAD), BF16),
            jax.ShapeDtypeStruct((b, MLA_HEADS, MLA_V, s), BF16),
        ],
        compiler_params=_params("parallel", "parallel"),
        name="proj0",
    )(x, gain, wh, wm, qg, wqn, wqr, wqs, kvg, wkv, cos_t, sin_t)


def _proj1_kernel(x_ref, g_ref, w_ref, cos_ref, sin_ref, q_ref, k_ref, vt_ref):
    d = x_ref.shape[2]
    xn = _rms(x_ref[0], g_ref[...]).astype(BF16)
    z = jnp.dot(xn, w_ref[...], preferred_element_type=F32)
    cos = cos_ref[...]
    sin = sin_ref[...]
    lane = lax.broadcasted_iota(jnp.int32, cos.shape, 1)
    first_half = (lane % DIFF_DIM) < (DIFF_DIM // 2)
    half = DIFF_DIM // 2
    for h in range(DIFF_HEADS):
        hs = slice(h * LANES, (h + 1) * LANES)
        for base, ref, scale in ((0, q_ref, DIFF_SCALE), (d, k_ref, 1.0)):
            t = z[:, base + h * LANES:base + (h + 1) * LANES]
            partner = jnp.where(first_half, pltpu.roll(t, LANES - half, axis=1), pltpu.roll(t, half, axis=1))
            ref[0, :, hs] = ((t * cos + partner * sin) * scale).astype(BF16)
        vt_ref[0, h] = z[:, 2 * d + h * LANES:2 * d + (h + 1) * LANES].T.astype(BF16)


def _proj1(x, gain, w, cos_t, sin_t, tm):
    b, s, d = x.shape
    return pl.pallas_call(
        _proj1_kernel,
        grid=(b, s // tm),
        in_specs=[
            pl.BlockSpec((1, tm, d), lambda i, j: (i, j, 0)),
            _full(gain.shape), _full(w.shape),
            pl.BlockSpec((tm, LANES), lambda i, j: (j, 0)),
            pl.BlockSpec((tm, LANES), lambda i, j: (j, 0)),
        ],
        out_specs=[
            pl.BlockSpec((1, tm, d), lambda i, j: (i, j, 0)),
            pl.BlockSpec((1, tm, d), lambda i, j: (i, j, 0)),
            pl.BlockSpec((1, DIFF_HEADS, 2 * DIFF_DIM, tm), lambda i, j: (i, 0, 0, j)),
        ],
        out_shape=[
            jax.ShapeDtypeStruct((b, s, d), BF16),
            jax.ShapeDtypeStruct((b, s, d), BF16),
            jax.ShapeDtypeStruct((b, DIFF_HEADS, 2 * DIFF_DIM, s), BF16),
        ],
        compiler_params=_params("parallel", "parallel"),
        name="proj1",
    )(x, gain, w, cos_t, sin_t)


def _hgrn_chunk(q, k, v, g, st_ref, rev):
    n = HGRN_CHUNK
    row = lax.broadcasted_iota(jnp.int32, g.shape, 0)
    blk = lax.broadcasted_iota(jnp.int32, (n, n), 0)
    blk_t = lax.broadcasted_iota(jnp.int32, (n, n), 1)
    p_sum = g
    x_sum = jnp.zeros_like(g)
    levels = []
    m = 1
    while m < n:
        in_right = (row % (2 * m)) >= m
        q_rows = jnp.logical_not(in_right) if rev else in_right
        qt = jnp.where(q_rows, q * jnp.exp(p_sum), 0.0).astype(BF16)
        kt = jnp.where(q_rows, 0.0, k * jnp.exp(x_sum)).astype(BF16)
        same_block = (blk // (2 * m)) == (blk_t // (2 * m)) if 2 * m < n else None
        levels.append((qt, kt, same_block))
        total = p_sum + x_sum
        from_left = pltpu.roll(total, m, axis=0)
        from_right = pltpu.roll(total, n - m, axis=0)
        if rev:
            p_sum = p_sum + jnp.where(in_right, 0.0, from_right)
            x_sum = x_sum + jnp.where(in_right, from_left, 0.0)
        else:
            p_sum = p_sum + jnp.where(in_right, from_left, 0.0)
            x_sum = x_sum + jnp.where(in_right, 0.0, from_right)
        m *= 2
    q_in = (q * jnp.exp(p_sum)).astype(BF16)
    k_out = (k * jnp.exp(x_sum)).astype(BF16)
    chunk_decay = jnp.exp(p_sum[0:1] + x_sum[0:1])
    qk = q * k
    vb = v.astype(BF16)
    nt = (((1,), (1,)), ((), ()))
    tn = (((0,), (0,)), ((), ()))
    outs = []
    for h in range(HGRN_HEADS):
        hs = slice(h * HGRN_DIM, (h + 1) * HGRN_DIM)
        a = None
        for qt, kt, same_block in levels:
            part = lax.dot_general(qt[:, hs], kt[:, hs], nt, preferred_element_type=F32)
            if same_block is not None:
                part = jnp.where(same_block, part, 0.0)
            a = part if a is None else a + part
        st = st_ref[h]
        o = jnp.dot(a.astype(BF16), vb[:, hs], preferred_element_type=F32)
        o = o + lax.dot_general(q_in[:, hs], st.astype(BF16), nt, preferred_element_type=F32)
        o = o + jnp.sum(qk[:, hs], axis=1, keepdims=True) * v[:, hs]
        st_ref[h] = st * chunk_decay[:, hs] + lax.dot_general(vb[:, hs], k_out[:, hs], tn,
                                                               preferred_element_type=F32)
        outs.append(o)
    return outs


def _hgrn_gates(qh, fp, lb):
    q = qh * jax.nn.sigmoid(qh) * (HGRN_DIM ** -0.5)
    g = jnp.log(lb + (1.0 - lb) * jax.nn.sigmoid(fp))
    k = (1.0 - lb) * jax.nn.sigmoid(-fp)
    return q, k, g


def _hgrn_fwd_kernel(q_ref, f_ref, v_ref, lb_ref, o_ref, st_ref):
    @pl.when(pl.program_id(1) == 0)
    def _():
        st_ref[...] = jnp.zeros_like(st_ref)

    lb = lb_ref[...]
    for c in range(HGRN_BLOCK // HGRN_CHUNK):
        rs = slice(c * HGRN_CHUNK, (c + 1) * HGRN_CHUNK)
        q, k, g = _hgrn_gates(q_ref[0, rs, :], f_ref[0, rs, :], lb)
        outs = _hgrn_chunk(q, k, v_ref[0, rs, :], g, st_ref, False)
        for h, o in enumerate(outs):
            o_ref[0, rs, h * HGRN_DIM:(h + 1) * HGRN_DIM] = o


def _hgrn_bwd_kernel(q_ref, f_ref, v_ref, gate_ref, of_ref, lb_ref, ng_ref, o_ref, st_ref):
    @pl.when(pl.program_id(1) == 0)
    def _():
        st_ref[...] = jnp.zeros_like(st_ref)

    lb = lb_ref[...]
    for c in reversed(range(HGRN_BLOCK // HGRN_CHUNK)):
        rs = slice(c * HGRN_CHUNK, (c + 1) * HGRN_CHUNK)
        q, k, g = _hgrn_gates(q_ref[0, rs, :], f_ref[0, rs, :], lb)
        outs = _hgrn_chunk(q, k, v_ref[0, rs, :], g, st_ref, True)
        gate = gate_ref[0, rs, :]
        gate = gate * jax.nn.sigmoid(gate)
        for h, o in enumerate(outs):
            hs = slice(h * HGRN_DIM, (h + 1) * HGRN_DIM)
            y = _rms(o + of_ref[0, rs, hs], ng_ref[:, hs])
            o_ref[0, rs, hs] = (y * gate[:, hs]).astype(BF16)


def _hgrn(zh, lb_fwd, lb_bwd, norm_gain):
    b, s, _ = zh.shape
    nb = s // HGRN_BLOCK
    state = pltpu.VMEM((HGRN_HEADS, HGRN_DIM, HGRN_DIM), F32)

    def col(c, rev):
        if rev:
            return pl.BlockSpec((1, HGRN_BLOCK, HGRN_W), lambda i, j: (i, nb - 1 - j, c))
        return pl.BlockSpec((1, HGRN_BLOCK, HGRN_W), lambda i, j: (i, j, c))

    o_fwd = pl.pallas_call(
        _hgrn_fwd_kernel,
        grid=(b, nb),
        in_specs=[col(0, False), col(1, False), col(3, False), _full(lb_fwd.shape)],
        out_specs=col(0, False),
        out_shape=jax.ShapeDtypeStruct((b, s, HGRN_W), F32),
        scratch_shapes=[state],
        compiler_params=_params("parallel", "arbitrary"),
        name="hgrn_fwd",
    )(zh, zh, zh, lb_fwd)
    return pl.pallas_call(
        _hgrn_bwd_kernel,
        grid=(b, nb),
        in_specs=[col(0, True), col(2, True), col(3, True), col(4, True), col(0, True),
                  _full(lb_bwd.shape), _full(norm_gain.shape)],
        out_specs=col(0, True),
        out_shape=jax.ShapeDtypeStruct((b, s, HGRN_W), BF16),
        scratch_shapes=[state],
        compiler_params=_params("parallel", "arbitrary"),
        name="hgrn_bwd",
    )(zh, zh, zh, zh, o_fwd, lb_bwd, norm_gain)


def _flash_loop(k_slice, vt_slice, qs_ref, m_ref, l_ref, acc_ref, s_bufs, p_bufs, a_bufs, n_kv, tk):
    assert n_kv >= 2 and n_kv % 2 == 0
    m_ref[...] = jnp.full(m_ref.shape, -jnp.inf, F32)
    l_ref[...] = jnp.zeros_like(l_ref)
    acc_ref[...] = jnp.zeros_like(acc_ref)

    def scores(j, slot):
        off = pl.multiple_of(j * tk, tk)
        s_bufs[slot][...] = lax.dot_general(k_slice(off), qs_ref[...], (((1,), (1,)), ((), ())),
                                            preferred_element_type=F32)

    def softmax(slot):
        s = s_bufs[slot][...]
        m_old = m_ref[...]
        m_new = jnp.maximum(m_old, jnp.max(s, axis=0, keepdims=True))
        alpha = jnp.exp2(m_old - m_new)
        p = jnp.exp2(s - m_new)
        l_ref[...] = alpha * l_ref[...] + jnp.sum(p, axis=0, keepdims=True)
        p_bufs[slot][...] = p.astype(BF16)
        a_bufs[slot][...] = alpha
        m_ref[...] = m_new

    def weighted(j, slot):
        off = pl.multiple_of(j * tk, tk)
        acc_ref[...] = a_bufs[slot][...] * acc_ref[...] + jnp.dot(vt_slice(off), p_bufs[slot][...],
                                                                  preferred_element_type=F32)

    scores(0, 0)
    softmax(0)
    scores(1, 1)

    def body(i, carry):
        j = 2 * i + 1
        weighted(j - 1, 0)
        softmax(1)
        scores(j + 1, 0)
        weighted(j, 1)
        softmax(0)
        scores(j + 2, 1)
        return carry

    lax.fori_loop(0, (n_kv - 2) // 2, body, 0)
    weighted(n_kv - 2, 0)
    softmax(1)
    weighted(n_kv - 1, 1)


def _flash_scratch(tk, w, d, dv):
    return [pltpu.VMEM((w, d), BF16), pltpu.VMEM((1, w), F32), pltpu.VMEM((1, w), F32),
            pltpu.VMEM((dv, w), F32), pltpu.VMEM((tk, w), F32), pltpu.VMEM((tk, w), F32),
            pltpu.VMEM((tk, w), BF16), pltpu.VMEM((tk, w), BF16), pltpu.VMEM((1, w), F32),
            pltpu.VMEM((1, w), F32)]


def _mla_attn_kernel(q_ref, k_ref, vt_ref, o_ref, qs_ref, m_ref, l_ref, acc_ref, s0, s1, p0, p1, a0, a1,
                     *, tk):
    s = k_ref.shape[2]
    qs_ref[...] = q_ref[0, 0]
    _flash_loop(lambda off: k_ref[0, 0, pl.ds(off, tk), :],
                lambda off: vt_ref[0, 0, :, pl.ds(off, tk)],
                qs_ref, m_ref, l_ref, acc_ref, (s0, s1), (p0, p1), (a0, a1), s // tk, tk)
    out_t = acc_ref[...] / l_ref[...]
    o_ref[0] = out_t.T.astype(BF16)


def _mla_attn(q, k, vt, tq, tk):
    b, h, s, d = q.shape
    dv = vt.shape[2]
    return pl.pallas_call(
        functools.partial(_mla_attn_kernel, tk=tk),
        grid=(b, h, s // tq),
        in_specs=[
            pl.BlockSpec((1, 1, tq, d), lambda i, j, t: (i, j, t, 0)),
            pl.BlockSpec((1, 1, s, d), lambda i, j, t: (i, j, 0, 0)),
            pl.BlockSpec((1, 1, dv, s), lambda i, j, t: (i, j, 0, 0)),
        ],
        out_specs=pl.BlockSpec((1, tq, dv), lambda i, j, t: (i, t, j)),
        out_shape=jax.ShapeDtypeStruct((b, s, h * dv), BF16),
        scratch_shapes=_flash_scratch(tk, tq, d, dv),
        compiler_params=_params("parallel", "parallel", "arbitrary"),
        name="mla_attn",
    )(q, k, vt)


def _diff_attn_kernel(q_ref, k_ref, vt_ref, lq1_ref, lk1_ref, lq2_ref, lk2_ref, ng_ref, o_ref,
                      qs_ref, m_ref, l_ref, acc_ref, s0, s1, p0, p1, a0, a1, *, tk, lambda_init):
    s = k_ref.shape[1]
    tq = q_ref.shape[1]
    q = q_ref[0]
    lane = lax.broadcasted_iota(jnp.int32, q.shape, 1)
    zero = jnp.zeros_like(q)
    qs_ref[0:tq, :] = jnp.where(lane < DIFF_DIM, q, zero)
    qs_ref[tq:2 * tq, :] = jnp.where(lane < DIFF_DIM, zero, q)
    _flash_loop(lambda off: k_ref[0, pl.ds(off, tk), :],
                lambda off: vt_ref[0, 0, :, pl.ds(off, tk)],
                qs_ref, m_ref, l_ref, acc_ref, (s0, s1), (p0, p1), (a0, a1), s // tk, tk)
    lam =(jnp.exp(jnp.sum(lq1_ref[...] * lk1_ref[...], axis=1, keepdims=True))
           - jnp.exp(jnp.sum(lq2_ref[...] * lk2_ref[...], axis=1, keepdims=True)) + lambda_init)
    soft = acc_ref[...] / l_ref[...]
    out_t = soft[:, 0:tq] - lam * soft[:, tq:2 * tq]
    y = _rms(out_t.T, ng_ref[...]) * (1.0 - lambda_init)
    o_ref[0] = y.astype(BF16)


def _diff_attn(q, k, vt, lq1, lk1, lq2, lk2, norm_gain, lambda_init, tq, tk):
    b, s, d = q.shape
    h = vt.shape[1]
    dv = vt.shape[2]
    small = _full(lq1.shape)
    return pl.pallas_call(
        functools.partial(_diff_attn_kernel, tk=tk, lambda_init=lambda_init),
        grid=(b, h, s // tq),
        in_specs=[
            pl.BlockSpec((1, tq, dv), lambda i, j, t: (i, t, j)),
            pl.BlockSpec((1, s, dv), lambda i, j, t: (i, 0, j)),
            pl.BlockSpec((1, 1, dv, s), lambda i, j, t: (i, j, 0, 0)),
            small, small, small, small, _full(norm_gain.shape),
        ],
        out_specs=pl.BlockSpec((1, tq, dv), lambda i, j, t: (i, t, j)),
        out_shape=jax.ShapeDtypeStruct((b, s, d), BF16),
        scratch_shapes=_flash_scratch(tk, 2 * tq, dv, dv),
        compiler_params=_params("parallel", "parallel", "arbitrary"),
        name="diff_attn",
    )(q, k, vt, lq1, lk1, lq2, lk2, norm_gain)


def _mix_ffn_kernel(*refs, n_mix, final):
    x_ref = refs[0]
    mix_refs = refs[1:1 + n_mix]
    wo_refs = refs[1 + n_mix:1 + 2 * n_mix]
    g_ref, wg_ref, wu_ref, wd_ref, gf_ref, out_ref, x1_ref, xn_ref, acc_ref = refs[1 + 2 * n_mix:]
    kk = pl.program_id(1)

    @pl.when(kk == 0)
    def _():
        x1 = x_ref[...]
        for m_ref, w_ref in zip(mix_refs, wo_refs):
            x1 = x1 + jnp.dot(m_ref[...], w_ref[...], preferred_element_type=F32)
        x1_ref[...] = x1
        xn_ref[...] = _rms(x1, g_ref[...]).astype(BF16)
        acc_ref[...] = jnp.zeros_like(acc_ref)

    xn = xn_ref[...]
    gate = jnp.dot(xn, wg_ref[...], preferred_element_type=F32)
    up = jnp.dot(xn, wu_ref[...], preferred_element_type=F32)
    hid = (gate * jax.nn.sigmoid(gate) * up).astype(BF16)
    acc_ref[...] += jnp.dot(hid, wd_ref[...], preferred_element_type=F32)

    @pl.when(kk == pl.num_programs(1) - 1)
    def _():
        y = x1_ref[...] + acc_ref[...]
        if final:
            y = _rms(y, gf_ref[...])
        out_ref[...] = y


def _mix_ffn(x, mixes, w_outs, gain, wg, wu, wd, final_gain, final, tm, th):
    t, d = x.shape
    hidden = wg.shape[1]
    n_mix = len(mixes)
    row = lambda i, k: (i, 0)
    in_specs = [pl.BlockSpec((tm, d), row)]
    in_specs += [pl.BlockSpec((tm, m.shape[1]), row) for m in mixes]
    in_specs += [_full(w.shape) for w in w_outs]
    in_specs += [
        _full(gain.shape),
        pl.BlockSpec((d, th), lambda i, k: (0, k)),
        pl.BlockSpec((d, th), lambda i, k: (0, k)),
        pl.BlockSpec((th, d), lambda i, k: (k, 0)),
        _full(final_gain.shape),
    ]
    return pl.pallas_call(
        functools.partial(_mix_ffn_kernel, n_mix=n_mix, final=final),
        grid=(t // tm, hidden // th),
        in_specs=in_specs,
        out_specs=pl.BlockSpec((tm, d), row),
        out_shape=jax.ShapeDtypeStruct((t, d), F32),
        scratch_shapes=[pltpu.VMEM((tm, d), F32), pltpu.VMEM((tm, d), BF16), pltpu.VMEM((tm, d), F32)],
        compiler_params=_params("parallel", "arbitrary"),
        name="mix_ffn",
    )(x, *mixes, *w_outs, gain, wg, wu, wd, final_gain)


def _rope_tables(seq_len):
    dim = MLA_ROPE
    inv = 1.0 / (ROPE_THETA ** (jnp.arange(0, dim, 2, dtype=F32) / dim))
    ang = jnp.arange(seq_len, dtype=F32)[:, None] * inv[None, :]
    cos, sin = jnp.cos(ang), jnp.sin(ang)
    cos_t = jnp.concatenate([cos, cos, cos, cos], axis=1)
    sin_t = jnp.concatenate([-sin, sin, -sin, sin], axis=1)
    return cos_t, sin_t


def _swap_halves(w):
    half = w.shape[-1] // 2
    return jnp.concatenate([w[..., half:], w[..., :half]], axis=-1)


def _pad_heads(w):
    kdim, h, r = w.shape
    return jnp.concatenate([w, jnp.zeros_like(w)], axis=-1).reshape(kdim, h * 2 * r)


def _row(v):
    return v.reshape(1, -1).astype(F32)


def kernel(x, norm_attn, norm_ffn, ffn_w_gate, ffn_w_up, ffn_w_down, ab_w_in, hgrn_lower_bound, hgrn_out_norm,
           mla_q_norm, mla_w_uq, mla_kv_norm, mla_w_ukv, ab_w_out, c_w_in, diff_lambda_q1, diff_lambda_k1,
           diff_lambda_q2, diff_lambda_k2, diff_out_norm, c_w_out, final_norm):
    b, s, d = x.shape
    depth = norm_attn.shape[0]
    assert DIFF_DIM == MLA_ROPE and d == DIFF_HEADS * 2 * DIFF_DIM and d == 2 * HGRN_W
    assert s % 512 == 0
    tm_proj = 512
    tm_ffn = 1024 if (b * s) % 1024 == 0 else 512
    th_ffn = 256
    tq_mla, tq_diff, tk = 512, 256, min(512, s // 2)

    cos_t, sin_t = _rope_tables(s)
    lower_bounds = jnp.cumsum(jax.nn.softmax(hgrn_lower_bound.astype(F32), axis=1), axis=1)

    for layer in range(depth):
        j = layer // 2
        gain = _row(norm_attn[layer])
        if layer % 2 == 0:
            w_in = ab_w_in[j]
            c_h = 5 * HGRN_W
            c_q = c_h + MLA_Q_LORA
            c_kv = c_q + MLA_KV_LORA
            w_kr = w_in[:, c_kv:]
            w_krs = _swap_halves(w_kr)
            wh = w_in[:, :c_h].astype(BF16)
            wm = jnp.concatenate([w_in[:, c_h:c_kv], w_kr, w_kr, w_krs, w_krs], axis=1).astype(BF16)
            wuq = mla_w_uq[j].reshape(MLA_Q_LORA, MLA_HEADS, MLA_NOPE + MLA_ROPE)
            wqn = wuq[..., :MLA_NOPE].reshape(MLA_Q_LORA, MLA_HEADS * MLA_NOPE).astype(BF16)
            wqr = _pad_heads(wuq[..., MLA_NOPE:]).astype(BF16)
            wqs = _pad_heads(_swap_halves(wuq[..., MLA_NOPE:])).astype(BF16)
            wukv = mla_w_ukv[j].reshape(MLA_KV_LORA, MLA_HEADS, MLA_NOPE + MLA_V)
            wkv = jnp.concatenate([wukv[..., :MLA_NOPE].reshape(MLA_KV_LORA, -1),
                                   wukv[..., MLA_NOPE:].reshape(MLA_KV_LORA, -1)], axis=1).astype(BF16)
            zh, q, k, vt = _proj0(x, gain, wh, wm, _row(mla_q_norm[j]), wqn, wqr, wqs, _row(mla_kv_norm[j]),
                                  wkv, cos_t, sin_t, tm_proj)
            o_a = _hgrn(zh, _row(lower_bounds[0, j]), _row(lower_bounds[1, j]), _row(hgrn_out_norm[j]))
            o_b = _mla_attn(q, k, vt, tq_mla, tk)
            w_out = ab_w_out[j].astype(BF16)
            mixes = [o_a.reshape(b * s, HGRN_W), o_b.reshape(b * s, MLA_HEADS * MLA_V)]
            w_outs = [w_out[:HGRN_W], w_out[HGRN_W:]]
        else:
            lambda_init = 0.8 - 0.6 * math.exp(-0.3 * layer)
            q, k, vt = _proj1(x, gain, c_w_in[j].astype(BF16), cos_t, sin_t, tm_proj)
            o_c = _diff_attn(q, k, vt, _row(diff_lambda_q1[j]), _row(diff_lambda_k1[j]),
                             _row(diff_lambda_q2[j]), _row(diff_lambda_k2[j]), _row(diff_out_norm[j]),
                             lambda_init, tq_diff, tk)
            mixes = [o_c.reshape(b * s, d)]
            w_outs = [c_w_out[j].astype(BF16)]
        x = _mix_ffn(x.reshape(b * s, d), mixes, w_outs, _row(norm_ffn[layer]),
                     ffn_w_gate[layer].astype(BF16), ffn_w_up[layer].astype(BF16),
                     ffn_w_down[layer].astype(BF16), _row(final_norm),
                     layer == depth - 1, tm_ffn, th_ffn).reshape(b, s, d)
    return x
```

```python
import functools
import math

import jax
import jax.numpy as jnp
from jax import lax
from jax.experimental import pallas as pl
from jax.experimental.pallas import tpu as pltpu

F32 = jnp.float32
BF16 = jnp.bfloat16

NORM_EPS = 1e-6
ROPE_THETA = 10000.0
LANES = 128
VMEM_LIMIT = 56 * 1024 * 1024

HGRN_HEADS = 4
HGRN_DIM = 128
HGRN_W = HGRN_HEADS * HGRN_DIM
HGRN_CHUNK = 64
HGRN_BLOCK = 256

MLA_HEADS = 4
MLA_NOPE = 128
MLA_ROPE = 64
MLA_V = 128
MLA_Q_LORA = 384
MLA_KV_LORA = 256
MLA_QK_PAD = 256
LOG2_E = math.log2(math.e)
MLA_SCALE = (MLA_NOPE + MLA_ROPE) ** -0.5 * LOG2_E

DIFF_HEADS = 8
DIFF_DIM = 64
DIFF_SCALE = DIFF_DIM ** -0.5 * LOG2_E


def _rms(x, gain):
    ms = jnp.mean(x * x, axis=-1, keepdims=True)
    return x * lax.rsqrt(ms + NORM_EPS) * gain


def _params(*sem):
    return pltpu.CompilerParams(dimension_semantics=sem, vmem_limit_bytes=VMEM_LIMIT)


def _full(shape):
    n = len(shape)
    return pl.BlockSpec(shape, lambda *_: (0,) * n)


def _proj0_kernel(x_ref, g_ref, wh_ref, wm_ref, qg_ref, wqn_ref, wqr_ref, wqs_ref, kvg_ref, wkv_ref,
                  cos_ref, sin_ref, zh_ref, q_ref, k_ref, vt_ref):
    xn = _rms(x_ref[0], g_ref[...]).astype(BF16)
    zh_ref[0] = jnp.dot(xn, wh_ref[...], preferred_element_type=F32)
    zm = jnp.dot(xn, wm_ref[...], preferred_element_type=F32)
    cos = cos_ref[...]
    sin = sin_ref[...]
    c0 = MLA_Q_LORA
    c1 = c0 + MLA_KV_LORA
    cqn = _rms(zm[:, :c0], qg_ref[...]).astype(BF16)
    ckvn = _rms(zm[:, c0:c1], kvg_ref[...]).astype(BF16)
    k_rope = (zm[:, c1:c1 + LANES] * cos + zm[:, c1 + LANES:c1 + 2 * LANES] * sin).astype(BF16)
    qn = jnp.dot(cqn, wqn_ref[...], preferred_element_type=F32)
    qr = jnp.dot(cqn, wqr_ref[...], preferred_element_type=F32)
    qs = jnp.dot(cqn, wqs_ref[...], preferred_element_type=F32)
    kv = jnp.dot(ckvn, wkv_ref[...], preferred_element_type=F32)
    hw = MLA_HEADS * MLA_NOPE
    for h in range(MLA_HEADS):
        hs = slice(h * LANES, (h + 1) * LANES)
        q_ref[0, h, :, 0:LANES] = (qn[:, hs] * MLA_SCALE).astype(BF16)
        q_ref[0, h, :, LANES:2 * LANES] = ((qr[:, hs] * cos + qs[:, hs] * sin) * MLA_SCALE).astype(BF16)
        k_ref[0, h, :, 0:LANES] = kv[:, hs].astype(BF16)
        k_ref[0, h, :, LANES:2 * LANES] = k_rope
        vt_ref[0, h] = kv[:, hw + h * LANES:hw + (h + 1) * LANES].T.astype(BF16)


def _proj0(x, gain, wh, wm, qg, wqn, wqr, wqs, kvg, wkv, cos_t, sin_t, tm):
    b, s, d = x.shape
    grid = (b, s // tm)
    return pl.pallas_call(
        _proj0_kernel,
        grid=grid,
        in_specs=[
            pl.BlockSpec((1, tm, d), lambda i, j: (i, j, 0)),
            _full(gain.shape), _full(wh.shape), _full(wm.shape), _full(qg.shape), _full(wqn.shape),
            _full(wqr.shape), _full(wqs.shape), _full(kvg.shape), _full(wkv.shape),
            pl.BlockSpec((tm, LANES), lambda i, j: (j, 0)),
            pl.BlockSpec((tm, LANES), lambda i, j: (j, 0)),
        ],
        out_specs=[
            pl.BlockSpec((1, tm, wh.shape[1]), lambda i, j: (i, j, 0)),
            pl.BlockSpec((1, MLA_HEADS, tm, MLA_QK_PAD), lambda i, j: (i, 0, j, 0)),
            pl.BlockSpec((1, MLA_HEADS, tm, MLA_QK_PAD), lambda i, j: (i, 0, j, 0)),
            pl.BlockSpec((1, MLA_HEADS, MLA_V, tm), lambda i, j: (i, 0, 0, j)),
        ],
        out_shape=[
            jax.ShapeDtypeStruct((b, s, wh.shape[1]), F32),
            jax.ShapeDtypeStruct((b, MLA_HEADS, s, MLA_QK_PAD), BF16),
            jax.ShapeDtypeStruct((b, MLA_HEADS, s, MLA_QK_PAD), BF16),
            jax.ShapeDtypeStruct((b, MLA_HEADS, MLA_V, s), BF16),
        ],
        compiler_params=_params("parallel", "parallel"),
        name="proj0",
    )(x, gain, wh, wm, qg, wqn, wqr, wqs, kvg, wkv, cos_t, sin_t)


def _proj1_kernel(x_ref, g_ref, w_ref, cos_ref, sin_ref, q_ref, k_ref, vt_ref):
    d = x_ref.shape[2]
    xn = _rms(x_ref[0], g_ref[...]).astype(BF16)
    z = jnp.dot(xn, w_ref[...], preferred_element_type=F32)
    cos = cos_ref[...]
    sin = sin_ref[...]
    lane = lax.broadcasted_iota(jnp.int32, cos.shape, 1)
    first_half = (lane % DIFF_DIM) < (DIFF_DIM // 2)
    half = DIFF_DIM // 2
    for h in range(DIFF_HEADS):
        hs = slice(h * LANES, (h + 1) * LANES)
        for base, ref, scale in ((0, q_ref, DIFF_SCALE), (d, k_ref, 1.0)):
            t = z[:, base + h * LANES:base + (h + 1) * LANES]
            partner = jnp.where(first_half, pltpu.roll(t, LANES - half, axis=1), pltpu.roll(t, half, axis=1))
            ref[0, :, hs] = ((t * cos + partner * sin) * scale).astype(BF16)
        vt_ref[0, h] = z[:, 2 * d + h * LANES:2 * d + (h + 1) * LANES].T.astype(BF16)


def _proj1(x, gain, w, cos_t, sin_t, tm):
    b, s, d = x.shape
    return pl.pallas_call(
        _proj1_kernel,
        grid=(b, s // tm),
        in_specs=[
            pl.BlockSpec((1, tm, d), lambda i, j: (i, j, 0)),
            _full(gain.shape), _full(w.shape),
            pl.BlockSpec((tm, LANES), lambda i, j: (j, 0)),
            pl.BlockSpec((tm, LANES), lambda i, j: (j, 0)),
        ],
        out_specs=[
            pl.BlockSpec((1, tm, d), lambda i, j: (i, j, 0)),
            pl.BlockSpec((1, tm, d), lambda i, j: (i, j, 0)),
            pl.BlockSpec((1, DIFF_HEADS, 2 * DIFF_DIM, tm), lambda i, j: (i, 0, 0, j)),
        ],
        out_shape=[
            jax.ShapeDtypeStruct((b, s, d), BF16),
            jax.ShapeDtypeStruct((b, s, d), BF16),
            jax.ShapeDtypeStruct((b, DIFF_HEADS, 2 * DIFF_DIM, s), BF16),
        ],
        compiler_params=_params("parallel", "parallel"),
        name="proj1",
    )(x, gain, w, cos_t, sin_t)


def _hgrn_chunk(q, k, v, g, st_ref, rev):
    n = HGRN_CHUNK
    row = lax.broadcasted_iota(jnp.int32, g.shape, 0)
    blk = lax.broadcasted_iota(jnp.int32, (n, n), 0)
    blk_t = lax.broadcasted_iota(jnp.int32, (n, n), 1)
    p_sum = g
    x_sum = jnp.zeros_like(g)
    levels = []
    m = 1
    while m < n:
        in_right = (row % (2 * m)) >= m
        q_rows = jnp.logical_not(in_right) if rev else in_right
        qt = jnp.where(q_rows, q * jnp.exp(p_sum), 0.0).astype(BF16)
        kt = jnp.where(q_rows, 0.0, k * jnp.exp(x_sum)).astype(BF16)
        same_block = (blk // (2 * m)) == (blk_t // (2 * m)) if 2 * m < n else None
        levels.append((qt, kt, same_block))
        total = p_sum + x_sum
        from_left = pltpu.roll(total, m, axis=0)
        from_right = pltpu.roll(total, n - m, axis=0)
        if rev:
            p_sum = p_sum + jnp.where(in_right, 0.0, from_right)
            x_sum = x_sum + jnp.where(in_right, from_left, 0.0)
        else:
            p_sum = p_sum + jnp.where(in_right, from_left, 0.0)
            x_sum = x_sum + jnp.where(in_right, 0.0, from_right)
        m *= 2
    q_in = (q * jnp.exp(p_sum)).astype(BF16)
    k_out = (k * jnp.exp(x_sum)).astype(BF16)
    chunk_decay = jnp.exp(p_sum[0:1] + x_sum[0:1])
    qk = q * k
    vb = v.astype(BF16)
    nt = (((1,), (1,)), ((), ()))
    tn = (((0,), (0,)), ((), ()))
    outs = []
    for h in range(HGRN_HEADS):
        hs = slice(h * HGRN_DIM, (h + 1) * HGRN_DIM)
        a = None
        for qt, kt, same_block in levels:
            part = lax.dot_general(qt[:, hs], kt[:, hs], nt, preferred_element_type=F32)
            if same_block is not None:
                part = jnp.where(same_block, part, 0.0)
            a = part if a is None else a + part
        st = st_ref[h]
        o = jnp.dot(a.astype(BF16), vb[:, hs], preferred_element_type=F32)
        o = o + lax.dot_general(q_in[:, hs], st.astype(BF16), nt, preferred_element_type=F32)
        o = o + jnp.sum(qk[:, hs], axis=1, keepdims=True) * v[:, hs]
        st_ref[h] = st * chunk_decay[:, hs] + lax.dot_general(vb[:, hs], k_out[:, hs], tn,
                                                               preferred_element_type=F32)
        outs.append(o)
    return outs


def _hgrn_gates(qh, fp, lb):
    q = qh * jax.nn.sigmoid(qh) * (HGRN_DIM ** -0.5)
    g = jnp.log(lb + (1.0 - lb) * jax.nn.sigmoid(fp))
    k = (1.0 - lb) * jax.nn.sigmoid(-fp)
    return q, k, g


def _hgrn_fwd_kernel(q_ref, f_ref, v_ref, lb_ref, o_ref, st_ref):
    @pl.when(pl.program_id(1) == 0)
    def _():
        st_ref[...] = jnp.zeros_like(st_ref)

    lb = lb_ref[...]
    for c in range(HGRN_BLOCK // HGRN_CHUNK):
        rs = slice(c * HGRN_CHUNK, (c + 1) * HGRN_CHUNK)
        q, k, g = _hgrn_gates(q_ref[0, rs, :], f_ref[0, rs, :], lb)
        outs = _hgrn_chunk(q, k, v_ref[0, rs, :], g, st_ref, False)
        for h, o in enumerate(outs):
            o_ref[0, rs, h * HGRN_DIM:(h + 1) * HGRN_DIM] = o


def _hgrn_bwd_kernel(q_ref, f_ref, v_ref, gate_ref, of_ref, lb_ref, ng_ref, o_ref, st_ref):
    @pl.when(pl.program_id(1) == 0)
    def _():
        st_ref[...] = jnp.zeros_like(st_ref)

    lb = lb_ref[...]
    for c in reversed(range(HGRN_BLOCK // HGRN_CHUNK)):
        rs = slice(c * HGRN_CHUNK, (c + 1) * HGRN_CHUNK)
        q, k, g = _hgrn_gates(q_ref[0, rs, :], f_ref[0, rs, :], lb)
        outs = _hgrn_chunk(q, k, v_ref[0, rs, :], g, st_ref, True)
        gate = gate_ref[0, rs, :]
        gate = gate * jax.nn.sigmoid(gate)
        for h, o in enumerate(outs):
            hs = slice(h * HGRN_DIM, (h + 1) * HGRN_DIM)
            y = _rms(o + of_ref[0, rs, hs], ng_ref[:, hs])
            o_ref[0, rs, hs] = (y * gate[:, hs]).astype(BF16)


def _hgrn(zh, lb_fwd, lb_bwd, norm_gain):
    b, s, _ = zh.shape
    nb = s // HGRN_BLOCK
    state = pltpu.VMEM((HGRN_HEADS, HGRN_DIM, HGRN_DIM), F32)

    def col(c, rev):
        if rev:
            return pl.BlockSpec((1, HGRN_BLOCK, HGRN_W), lambda i, j: (i, nb - 1 - j, c))
        return pl.BlockSpec((1, HGRN_BLOCK, HGRN_W), lambda i, j: (i, j, c))

    o_fwd = pl.pallas_call(
        _hgrn_fwd_kernel,
        grid=(b, nb),
        in_specs=[col(0, False), col(1, False), col(3, False), _full(lb_fwd.shape)],
        out_specs=col(0, False),
        out_shape=jax.ShapeDtypeStruct((b, s, HGRN_W), F32),
        scratch_shapes=[state],
        compiler_params=_params("parallel", "arbitrary"),
        name="hgrn_fwd",
    )(zh, zh, zh, lb_fwd)
    return pl.pallas_call(
        _hgrn_bwd_kernel,
        grid=(b, nb),
        in_specs=[col(0, True), col(2, True), col(3, True), col(4, True), col(0, True),
                  _full(lb_bwd.shape), _full(norm_gain.shape)],
        out_specs=col(0, True),
        out_shape=jax.ShapeDtypeStruct((b, s, HGRN_W), BF16),
        scratch_shapes=[state],
        compiler_params=_params("parallel", "arbitrary"),
        name="hgrn_bwd",
    )(zh, zh, zh, zh, o_fwd, lb_bwd, norm_gain)


def _flash_loop(k_slice, vt_slice, qs_ref, m_ref, l_ref, acc_ref, s_bufs, p_bufs, a_bufs, n_kv, tk):
    assert n_kv >= 2 and n_kv % 2 == 0
    zero = pl.multiple_of(jnp.minimum(pl.program_id(0), 0), tk)
    m_ref[...] = jnp.full(m_ref.shape, -jnp.inf, F32)
    l_ref[...] = jnp.zeros_like(l_ref)
    acc_ref[...] = jnp.zeros_like(acc_ref)

    def scores(j, slot):
        off = j * tk
        s_bufs[slot][...] = lax.dot_general(k_slice(off), qs_ref[...], (((1,), (1,)), ((), ())),
                                            preferred_element_type=F32)

    def softmax(slot):
        s_ref, p_ref, a_ref = s_bufs[slot], p_bufs[slot], a_bufs[slot]
        for c in range(s_ref.shape[1] // LANES):
            cs = slice(c * LANES, (c + 1) * LANES)
            m_old = m_ref[:, cs]
            m_new = jnp.maximum(m_old, jnp.max(s_ref[:, cs], axis=0, keepdims=True))
            alpha = jnp.exp2(m_old - m_new)
            p = jnp.exp2(s_ref[pl.ds(zero, tk), cs] - m_new)
            l_ref[:, cs] = alpha * l_ref[:, cs] + jnp.sum(p, axis=0, keepdims=True)
            p_ref[:, cs] = p.astype(BF16)
            a_ref[:, cs] = alpha
            m_ref[:, cs] = m_new

    def weighted(j, slot):
        off = j * tk
        acc_ref[...] = a_bufs[slot][...] * acc_ref[...] + jnp.dot(vt_slice(off), p_bufs[slot][...],
                                                                  preferred_element_type=F32)

    scores(0, 0)
    softmax(0)
    scores(1, 1)

    for j in range(1, n_kv - 1, 2):
        weighted(j - 1, 0)
        softmax(1)
        scores(j + 1, 0)
        weighted(j, 1)
        softmax(0)
        scores(j + 2, 1)
    weighted(n_kv - 2, 0)
    softmax(1)
    weighted(n_kv - 1, 1)


def _flash_scratch(tk, w, d, dv):
    return [pltpu.VMEM((w, d), BF16), pltpu.VMEM((1, w), F32), pltpu.VMEM((1, w), F32),
            pltpu.VMEM((dv, w), F32), pltpu.VMEM((tk, w), F32), pltpu.VMEM((tk, w), F32),
            pltpu.VMEM((tk, w), BF16), pltpu.VMEM((tk, w), BF16), pltpu.VMEM((1, w), F32),
            pltpu.VMEM((1, w), F32)]


def _mla_attn_kernel(q_ref, k_ref, vt_ref, o_ref, qs_ref, m_ref, l_ref, acc_ref, s0, s1, p0, p1, a0, a1,
                     *, tk):
    s = k_ref.shape[2]
    qs_ref[...] = q_ref[0, 0]
    _flash_loop(lambda off: k_ref[0, 0, pl.ds(off, tk), :],
                lambda off: vt_ref[0, 0, :, pl.ds(off, tk)],
                qs_ref, m_ref, l_ref, acc_ref, (s0, s1), (p0, p1), (a0, a1), s // tk, tk)
    out_t = acc_ref[...] / l_ref[...]
    o_ref[0] = out_t.T.astype(BF16)


def _mla_attn(q, k, vt, tq, tk):
    b, h, s, d = q.shape
    dv = vt.shape[2]
    return pl.pallas_call(
        functools.partial(_mla_attn_kernel, tk=tk),
        grid=(b, h, s // tq),
        in_specs=[
            pl.BlockSpec((1, 1, tq, d), lambda i, j, t: (i, j, t, 0)),
            pl.BlockSpec((1, 1, s, d), lambda i, j, t: (i, j, 0, 0)),
            pl.BlockSpec((1, 1, dv, s), lambda i, j, t: (i, j, 0, 0)),
        ],
        out_specs=pl.BlockSpec((1, tq, dv), lambda i, j, t: (i, t, j)),
        out_shape=jax.ShapeDtypeStruct((b, s, h * dv), BF16),
        scratch_shapes=_flash_scratch(tk, tq, d, dv),
        compiler_params=_params("parallel", "parallel", "arbitrary"),
        name="mla_attn",
    )(q, k, vt)


def _diff_attn_kernel(q_ref, k_ref, vt_ref, lq1_ref, lk1_ref, lq2_ref, lk2_ref, ng_ref, o_ref,
                      qs_ref, m_ref, l_ref, acc_ref, s0, s1, p0, p1, a0, a1, *, tk, lambda_init):
    s = k_ref.shape[1]
    tq = q_ref.shape[1]
    q = q_ref[0]
    lane = lax.broadcasted_iota(jnp.int32, q.shape, 1)
    zero = jnp.zeros_like(q)
    qs_ref[0:tq, :] = jnp.where(lane < DIFF_DIM, q, zero)
    qs_ref[tq:2 * tq, :] = jnp.where(lane < DIFF_DIM, zero, q)
    _flash_loop(lambda off: k_ref[0, pl.ds(off, tk), :],
                lambda off: vt_ref[0, 0, :, pl.ds(off, tk)],
                qs_ref, m_ref, l_ref, acc_ref, (s0, s1), (p0, p1), (a0, a1), s // tk, tk)
    lam =(jnp.exp(jnp.sum(lq1_ref[...] * lk1_ref[...], axis=1, keepdims=True))
           - jnp.exp(jnp.sum(lq2_ref[...] * lk2_ref[...], axis=1, keepdims=True)) + lambda_init)
    soft = acc_ref[...] / l_ref[...]
    out_t = soft[:, 0:tq] - lam * soft[:, tq:2 * tq]
    y = _rms(out_t.T, ng_ref[...]) * (1.0 - lambda_init)
    o_ref[0] = y.astype(BF16)


def _diff_attn(q, k, vt, lq1, lk1, lq2, lk2, norm_gain, lambda_init, tq, tk):
    b, s, d = q.shape
    h = vt.shape[1]
    dv = vt.shape[2]
    small = _full(lq1.shape)
    return pl.pallas_call(
        functools.partial(_diff_attn_kernel, tk=tk, lambda_init=lambda_init),
        grid=(b, h, s // tq),
        in_specs=[
            pl.BlockSpec((1, tq, dv), lambda i, j, t: (i, t, j)),
            pl.BlockSpec((1, s, dv), lambda i, j, t: (i, 0, j)),
            pl.BlockSpec((1, 1, dv, s), lambda i, j, t: (i, j, 0, 0)),
            small, small, small, small, _full(norm_gain.shape),
        ],
        out_specs=pl.BlockSpec((1, tq, dv), lambda i, j, t: (i, t, j)),
        out_shape=jax.ShapeDtypeStruct((b, s, d), BF16),
        scratch_shapes=_flash_scratch(tk, 2 * tq, dv, dv),
        compiler_params=_params("parallel", "parallel", "arbitrary"),
        name="diff_attn",
    )(q, k, vt, lq1, lk1, lq2, lk2, norm_gain)


def _mix_ffn_kernel(*refs, n_mix, final):
    x_ref = refs[0]
    mix_refs = refs[1:1 + n_mix]
    wo_refs = refs[1 + n_mix:1 + 2 * n_mix]
    g_ref, wg_ref, wu_ref, wd_ref, gf_ref, out_ref, x1_ref, xn_ref, acc_ref = refs[1 + 2 * n_mix:]
    kk = pl.program_id(1)

    @pl.when(kk == 0)
    def _():
        x1 = x_ref[...]
        for m_ref, w_ref in zip(mix_refs, wo_refs):
            x1 = x1 + jnp.dot(m_ref[...], w_ref[...], preferred_element_type=F32)
        x1_ref[...] = x1
        xn_ref[...] = _rms(x1, g_ref[...]).astype(BF16)
        acc_ref[...] = jnp.zeros_like(acc_ref)

    xn = xn_ref[...]
    gate = jnp.dot(xn, wg_ref[...], preferred_element_type=F32)
    up = jnp.dot(xn, wu_ref[...], preferred_element_type=F32)
    hid = (gate * jax.nn.sigmoid(gate) * up).astype(BF16)
    acc_ref[...] += jnp.dot(hid, wd_ref[...], preferred_element_type=F32)

    @pl.when(kk == pl.num_programs(1) - 1)
    def _():
        y = x1_ref[...] + acc_ref[...]
        if final:
            y = _rms(y, gf_ref[...])
        out_ref[...] = y


def _mix_ffn(x, mixes, w_outs, gain, wg, wu, wd, final_gain, final, tm, th):
    t, d = x.shape
    hidden = wg.shape[1]
    n_mix = len(mixes)
    row = lambda i, k: (i, 0)
    in_specs = [pl.BlockSpec((tm, d), row)]
    in_specs += [pl.BlockSpec((tm, m.shape[1]), row) for m in mixes]
    in_specs += [_full(w.shape) for w in w_outs]
    in_specs += [
        _full(gain.shape),
        pl.BlockSpec((d, th), lambda i, k: (0, k)),
        pl.BlockSpec((d, th), lambda i, k: (0, k)),
        pl.BlockSpec((th, d), lambda i, k: (k, 0)),
        _full(final_gain.shape),
    ]
    return pl.pallas_call(
        functools.partial(_mix_ffn_kernel, n_mix=n_mix, final=final),
        grid=(t // tm, hidden // th),
        in_specs=in_specs,
        out_specs=pl.BlockSpec((tm, d), row),
        out_shape=jax.ShapeDtypeStruct((t, d), F32),
        scratch_shapes=[pltpu.VMEM((tm, d), F32), pltpu.VMEM((tm, d), BF16), pltpu.VMEM((tm, d), F32)],
        compiler_params=_params("parallel", "arbitrary"),
        name="mix_ffn",
    )(x, *mixes, *w_outs, gain, wg, wu, wd, final_gain)


def _rope_tables(seq_len):
    dim = MLA_ROPE
    inv = 1.0 / (ROPE_THETA ** (jnp.arange(0, dim, 2, dtype=F32) / dim))
    ang = jnp.arange(seq_len, dtype=F32)[:, None] * inv[None, :]
    cos, sin = jnp.cos(ang), jnp.sin(ang)
    cos_t = jnp.concatenate([cos, cos, cos, cos], axis=1)
    sin_t = jnp.concatenate([-sin, sin, -sin, sin], axis=1)
    return cos_t, sin_t


def _swap_halves(w):
    half = w.shape[-1] // 2
    return jnp.concatenate([w[..., half:], w[..., :half]], axis=-1)


def _pad_heads(w):
    kdim, h, r = w.shape
    return jnp.concatenate([w, jnp.zeros_like(w)], axis=-1).reshape(kdim, h * 2 * r)


def _row(v):
    return v.reshape(1, -1).astype(F32)


def kernel(x, norm_attn, norm_ffn, ffn_w_gate, ffn_w_up, ffn_w_down, ab_w_in, hgrn_lower_bound, hgrn_out_norm,
           mla_q_norm, mla_w_uq, mla_kv_norm, mla_w_ukv, ab_w_out, c_w_in, diff_lambda_q1, diff_lambda_k1,
           diff_lambda_q2, diff_lambda_k2, diff_out_norm, c_w_out, final_norm):
    b, s, d = x.shape
    depth = norm_attn.shape[0]
    assert DIFF_DIM == MLA_ROPE and d == DIFF_HEADS * 2 * DIFF_DIM and d == 2 * HGRN_W
    assert s % 512 == 0
    tm_proj = 512
    tm_ffn = 1024 if (b * s) % 1024 == 0 else 512
    th_ffn = 256
    tq_mla, tq_diff, tk = 512, 256, min(512, s // 2)

    cos_t, sin_t = _rope_tables(s)
    lower_bounds = jnp.cumsum(jax.nn.softmax(hgrn_lower_bound.astype(F32), axis=1), axis=1)

    for layer in range(depth):
        j = layer // 2
        gain = _row(norm_attn[layer])
        if layer % 2 == 0:
            w_in = ab_w_in[j]
            c_h = 5 * HGRN_W
            c_q = c_h + MLA_Q_LORA
            c_kv = c_q + MLA_KV_LORA
            w_kr = w_in[:, c_kv:]
            w_krs = _swap_halves(w_kr)
            wh = w_in[:, :c_h].astype(BF16)
            wm = jnp.concatenate([w_in[:, c_h:c_kv], w_kr, w_kr, w_krs, w_krs], axis=1).astype(BF16)
            wuq = mla_w_uq[j].reshape(MLA_Q_LORA, MLA_HEADS, MLA_NOPE + MLA_ROPE)
            wqn = wuq[..., :MLA_NOPE].reshape(MLA_Q_LORA, MLA_HEADS * MLA_NOPE).astype(BF16)
            wqr = _pad_heads(wuq[..., MLA_NOPE:]).astype(BF16)
            wqs = _pad_heads(_swap_halves(wuq[..., MLA_NOPE:])).astype(BF16)
            wukv = mla_w_ukv[j].reshape(MLA_KV_LORA, MLA_HEADS, MLA_NOPE + MLA_V)
            wkv = jnp.concatenate([wukv[..., :MLA_NOPE].reshape(MLA_KV_LORA, -1),
                                   wukv[..., MLA_NOPE:].reshape(MLA_KV_LORA, -1)], axis=1).astype(BF16)
            zh, q, k, vt = _proj0(x, gain, wh, wm, _row(mla_q_norm[j]), wqn, wqr, wqs, _row(mla_kv_norm[j]),
                                  wkv, cos_t, sin_t, tm_proj)
            o_a = _hgrn(zh, _row(lower_bounds[0, j]), _row(lower_bounds[1, j]), _row(hgrn_out_norm[j]))
            o_b = _mla_attn(q, k, vt, tq_mla, tk)
            w_out = ab_w_out[j].astype(BF16)
            mixes = [o_a.reshape(b * s, HGRN_W), o_b.reshape(b * s, MLA_HEADS * MLA_V)]
            w_outs = [w_out[:HGRN_W], w_out[HGRN_W:]]
        else:
            lambda_init = 0.8 - 0.6 * math.exp(-0.3 * layer)
            q, k, vt = _proj1(x, gain, c_w_in[j].astype(BF16), cos_t, sin_t, tm_proj)
            o_c = _diff_attn(q, k, vt, _row(diff_lambda_q1[j]), _row(diff_lambda_k1[j]),
                             _row(diff_lambda_q2[j]), _row(diff_lambda_k2[j]), _row(diff_out_norm[j]),
                             lambda_init, tq_diff, tk)
            mixes = [o_c.reshape(b * s, d)]
            w_outs = [c_w_out[j].astype(BF16)]
        x = _mix_ffn(x.reshape(b * s, d), mixes, w_outs, _row(norm_ffn[layer]),
                     ffn_w_gate[layer].astype(BF16), ffn_w_up[layer].astype(BF16),
                     ffn_w_down[layer].astype(BF16), _row(final_norm),
                     layer == depth - 1, tm_ffn, th_ffn).reshape(b, s, d)
    return x
```

```python
import functools
import math

import jax
import jax.numpy as jnp
from jax import lax
from jax.experimental import pallas as pl
from jax.experimental.pallas import tpu as pltpu

F32 = jnp.float32
BF16 = jnp.bfloat16

NORM_EPS = 1e-6
ROPE_THETA = 10000.0
LANES = 128
VMEM_LIMIT = 56 * 1024 * 1024

HGRN_HEADS = 4
HGRN_DIM = 128
HGRN_W = HGRN_HEADS * HGRN_DIM
HGRN_CHUNK = 64
HGRN_BLOCK = 256

MLA_HEADS = 4
MLA_NOPE = 128
MLA_ROPE = 64
MLA_V = 128
MLA_Q_LORA = 384
MLA_KV_LORA = 256
MLA_QK_PAD = 256
LOG2_E = math.log2(math.e)
MLA_SCALE = (MLA_NOPE + MLA_ROPE) ** -0.5 * LOG2_E

DIFF_HEADS = 8
DIFF_DIM = 64
DIFF_SCALE = DIFF_DIM ** -0.5 * LOG2_E


def _rms(x, gain):
    ms = jnp.mean(x * x, axis=-1, keepdims=True)
    return x * lax.rsqrt(ms + NORM_EPS) * gain


def _params(*sem):
    return pltpu.CompilerParams(dimension_semantics=sem, vmem_limit_bytes=VMEM_LIMIT)


def _full(shape):
    n = len(shape)
    return pl.BlockSpec(shape, lambda *_: (0,) * n)


def _proj0_kernel(x_ref, g_ref, wh_ref, wm_ref, qg_ref, wqn_ref, wqr_ref, wqs_ref, kvg_ref, wkv_ref,
                  cos_ref, sin_ref, zh_ref, q_ref, k_ref, vt_ref):
    xn = _rms(x_ref[0], g_ref[...]).astype(BF16)
    zh_ref[0] = jnp.dot(xn, wh_ref[...], preferred_element_type=F32)
    zm = jnp.dot(xn, wm_ref[...], preferred_element_type=F32)
    cos = cos_ref[...]
    sin = sin_ref[...]
    c0 = MLA_Q_LORA
    c1 = c0 + MLA_KV_LORA
    cqn = _rms(zm[:, :c0], qg_ref[...]).astype(BF16)
    ckvn = _rms(zm[:, c0:c1], kvg_ref[...]).astype(BF16)
    k_rope = (zm[:, c1:c1 + LANES] * cos + zm[:, c1 + LANES:c1 + 2 * LANES] * sin).astype(BF16)
    qn = jnp.dot(cqn, wqn_ref[...], preferred_element_type=F32)
    qr = jnp.dot(cqn, wqr_ref[...], preferred_element_type=F32)
    qs = jnp.dot(cqn, wqs_ref[...], preferred_element_type=F32)
    kv = jnp.dot(ckvn, wkv_ref[...], preferred_element_type=F32)
    hw = MLA_HEADS * MLA_NOPE
    for h in range(MLA_HEADS):
        hs = slice(h * LANES, (h + 1) * LANES)
        q_ref[0, h, :, 0:LANES] = (qn[:, hs] * MLA_SCALE).astype(BF16)
        q_ref[0, h, :, LANES:2 * LANES] = ((qr[:, hs] * cos + qs[:, hs] * sin) * MLA_SCALE).astype(BF16)
        k_ref[0, h, :, 0:LANES] = kv[:, hs].astype(BF16)
        k_ref[0, h, :, LANES:2 * LANES] = k_rope
        vt_ref[0, h] = kv[:, hw + h * LANES:hw + (h + 1) * LANES].T.astype(BF16)


def _proj0(x, gain, wh, wm, qg, wqn, wqr, wqs, kvg, wkv, cos_t, sin_t, tm):
    b, s, d = x.shape
    grid = (b, s // tm)
    return pl.pallas_call(
        _proj0_kernel,
        grid=grid,
        in_specs=[
            pl.BlockSpec((1, tm, d), lambda i, j: (i, j, 0)),
            _full(gain.shape), _full(wh.shape), _full(wm.shape), _full(qg.shape), _full(wqn.shape),
            _full(wqr.shape), _full(wqs.shape), _full(kvg.shape), _full(wkv.shape),
            pl.BlockSpec((tm, LANES), lambda i, j: (j, 0)),
            pl.BlockSpec((tm, LANES), lambda i, j: (j, 0)),
        ],
        out_specs=[
            pl.BlockSpec((1, tm, wh.shape[1]), lambda i, j: (i, j, 0)),
            pl.BlockSpec((1, MLA_HEADS, tm, MLA_QK_PAD), lambda i, j: (i, 0, j, 0)),
            pl.BlockSpec((1, MLA_HEADS, tm, MLA_QK_PAD), lambda i, j: (i, 0, j, 0)),
            pl.BlockSpec((1, MLA_HEADS, MLA_V, tm), lambda i, j: (i, 0, 0, j)),
        ],
        out_shape=[
            jax.ShapeDtypeStruct((b, s, wh.shape[1]), F32),
            jax.ShapeDtypeStruct((b, MLA_HEADS, s, MLA_QK_PAD), BF16),
            jax.ShapeDtypeStruct((b, MLA_HEADS, s, MLA_QK_PAD), BF16),
            jax.ShapeDtypeStruct((b, MLA_HEADS, MLA_V, s), BF16),
        ],
        compiler_params=_params("parallel", "parallel"),
        name="proj0",
    )(x, gain, wh, wm, qg, wqn, wqr, wqs, kvg, wkv, cos_t, sin_t)


def _proj1_kernel(x_ref, g_ref, w_ref, cos_ref, sin_ref, q_ref, k_ref, vt_ref):
    d = x_ref.shape[2]
    xn = _rms(x_ref[0], g_ref[...]).astype(BF16)
    z = jnp.dot(xn, w_ref[...], preferred_element_type=F32)
    cos = cos_ref[...]
    sin = sin_ref[...]
    lane = lax.broadcasted_iota(jnp.int32, cos.shape, 1)
    first_half = (lane % DIFF_DIM) < (DIFF_DIM // 2)
    half = DIFF_DIM // 2
    for h in range(DIFF_HEADS):
        hs = slice(h * LANES, (h + 1) * LANES)
        for base, ref, scale in ((0, q_ref, DIFF_SCALE), (d, k_ref, 1.0)):
            t = z[:, base + h * LANES:base + (h + 1) * LANES]
            partner = jnp.where(first_half, pltpu.roll(t, LANES - half, axis=1), pltpu.roll(t, half, axis=1))
            ref[0, :, hs] = ((t * cos + partner * sin) * scale).astype(BF16)
        vt_ref[0, h] = z[:, 2 * d + h * LANES:2 * d + (h + 1) * LANES].T.astype(BF16)


def _proj1(x, gain, w, cos_t, sin_t, tm):
    b, s, d = x.shape
    return pl.pallas_call(
        _proj1_kernel,
        grid=(b, s // tm),
        in_specs=[
            pl.BlockSpec((1, tm, d), lambda i, j: (i, j, 0)),
            _full(gain.shape), _full(w.shape),
            pl.BlockSpec((tm, LANES), lambda i, j: (j, 0)),
            pl.BlockSpec((tm, LANES), lambda i, j: (j, 0)),
        ],
        out_specs=[
            pl.BlockSpec((1, tm, d), lambda i, j: (i, j, 0)),
            pl.BlockSpec((1, tm, d), lambda i, j: (i, j, 0)),
            pl.BlockSpec((1, DIFF_HEADS, 2 * DIFF_DIM, tm), lambda i, j: (i, 0, 0, j)),
        ],
        out_shape=[
            jax.ShapeDtypeStruct((b, s, d), BF16),
            jax.ShapeDtypeStruct((b, s, d), BF16),
            jax.ShapeDtypeStruct((b, DIFF_HEADS, 2 * DIFF_DIM, s), BF16),
        ],
        compiler_params=_params("parallel", "parallel"),
        name="proj1",
    )(x, gain, w, cos_t, sin_t)


def _hgrn_chunk(q, k, v, g, st_ref, rev):
    n = HGRN_CHUNK
    row = lax.broadcasted_iota(jnp.int32, g.shape, 0)
    blk = lax.broadcasted_iota(jnp.int32, (n, n), 0)
    blk_t = lax.broadcasted_iota(jnp.int32, (n, n), 1)
    p_sum = g
    x_sum = jnp.zeros_like(g)
    levels = []
    m = 1
    while m < n:
        in_right = (row % (2 * m)) >= m
        q_rows = jnp.logical_not(in_right) if rev else in_right
        qt = jnp.where(q_rows, q * jnp.exp(p_sum), 0.0).astype(BF16)
        kt = jnp.where(q_rows, 0.0, k * jnp.exp(x_sum)).astype(BF16)
        same_block = (blk // (2 * m)) == (blk_t // (2 * m)) if 2 * m < n else None
        levels.append((qt, kt, same_block))
        total = p_sum + x_sum
        from_left = pltpu.roll(total, m, axis=0)
        from_right = pltpu.roll(total, n - m, axis=0)
        if rev:
            p_sum = p_sum + jnp.where(in_right, 0.0, from_right)
            x_sum = x_sum + jnp.where(in_right, from_left, 0.0)
        else:
            p_sum = p_sum + jnp.where(in_right, from_left, 0.0)
            x_sum = x_sum + jnp.where(in_right, 0.0, from_right)
        m *= 2
    q_in = (q * jnp.exp(p_sum)).astype(BF16)
    k_out = (k * jnp.exp(x_sum)).astype(BF16)
    chunk_decay = jnp.exp(p_sum[0:1] + x_sum[0:1])
    qk = q * k
    vb = v.astype(BF16)
    nt = (((1,), (1,)), ((), ()))
    tn = (((0,), (0,)), ((), ()))
    outs = []
    for h in range(HGRN_HEADS):
        hs = slice(h * HGRN_DIM, (h + 1) * HGRN_DIM)
        a = None
        for qt, kt, same_block in levels:
            part = lax.dot_general(qt[:, hs], kt[:, hs], nt, preferred_element_type=F32)
            if same_block is not None:
                part = jnp.where(same_block, part, 0.0)
            a = part if a is None else a + part
        st = st_ref[h]
        o = jnp.dot(a.astype(BF16), vb[:, hs], preferred_element_type=F32)
        o = o + lax.dot_general(q_in[:, hs], st.astype(BF16), nt, preferred_element_type=F32)
        o = o + jnp.sum(qk[:, hs], axis=1, keepdims=True) * v[:, hs]
        st_ref[h] = st * chunk_decay[:, hs] + lax.dot_general(vb[:, hs], k_out[:, hs], tn,
                                                               preferred_element_type=F32)
        outs.append(o)
    return outs


def _hgrn_gates(qh, fp, lb):
    q = qh * jax.nn.sigmoid(qh) * (HGRN_DIM ** -0.5)
    g = jnp.log(lb + (1.0 - lb) * jax.nn.sigmoid(fp))
    k = (1.0 - lb) * jax.nn.sigmoid(-fp)
    return q, k, g


def _hgrn_fwd_kernel(q_ref, f_ref, v_ref, lb_ref, o_ref, st_ref):
    @pl.when(pl.program_id(1) == 0)
    def _():
        st_ref[...] = jnp.zeros_like(st_ref)

    lb = lb_ref[...]
    for c in range(HGRN_BLOCK // HGRN_CHUNK):
        rs = slice(c * HGRN_CHUNK, (c + 1) * HGRN_CHUNK)
        q, k, g = _hgrn_gates(q_ref[0, rs, :], f_ref[0, rs, :], lb)
        outs = _hgrn_chunk(q, k, v_ref[0, rs, :], g, st_ref, False)
        for h, o in enumerate(outs):
            o_ref[0, rs, h * HGRN_DIM:(h + 1) * HGRN_DIM] = o


def _hgrn_bwd_kernel(q_ref, f_ref, v_ref, gate_ref, of_ref, lb_ref, ng_ref, o_ref, st_ref):
    @pl.when(pl.program_id(1) == 0)
    def _():
        st_ref[...] = jnp.zeros_like(st_ref)

    lb = lb_ref[...]
    for c in reversed(range(HGRN_BLOCK // HGRN_CHUNK)):
        rs = slice(c * HGRN_CHUNK, (c + 1) * HGRN_CHUNK)
        q, k, g = _hgrn_gates(q_ref[0, rs, :], f_ref[0, rs, :], lb)
        outs = _hgrn_chunk(q, k, v_ref[0, rs, :], g, st_ref, True)
        gate = gate_ref[0, rs, :]
        gate = gate * jax.nn.sigmoid(gate)
        for h, o in enumerate(outs):
            hs = slice(h * HGRN_DIM, (h + 1) * HGRN_DIM)
            y = _rms(o + of_ref[0, rs, hs], ng_ref[:, hs])
            o_ref[0, rs, hs] = (y * gate[:, hs]).astype(BF16)


def _hgrn(zh, lb_fwd, lb_bwd, norm_gain):
    b, s, _ = zh.shape
    nb = s // HGRN_BLOCK
    state = pltpu.VMEM((HGRN_HEADS, HGRN_DIM, HGRN_DIM), F32)

    def col(c, rev):
        if rev:
            return pl.BlockSpec((1, HGRN_BLOCK, HGRN_W), lambda i, j: (i, nb - 1 - j, c))
        return pl.BlockSpec((1, HGRN_BLOCK, HGRN_W), lambda i, j: (i, j, c))

    o_fwd = pl.pallas_call(
        _hgrn_fwd_kernel,
        grid=(b, nb),
        in_specs=[col(0, False), col(1, False), col(3, False), _full(lb_fwd.shape)],
        out_specs=col(0, False),
        out_shape=jax.ShapeDtypeStruct((b, s, HGRN_W), F32),
        scratch_shapes=[state],
        compiler_params=_params("parallel", "arbitrary"),
        name="hgrn_fwd",
    )(zh, zh, zh, lb_fwd)
    return pl.pallas_call(
        _hgrn_bwd_kernel,
        grid=(b, nb),
        in_specs=[col(0, True), col(2, True), col(3, True), col(4, True), col(0, True),
                  _full(lb_bwd.shape), _full(norm_gain.shape)],
        out_specs=col(0, True),
        out_shape=jax.ShapeDtypeStruct((b, s, HGRN_W), BF16),
        scratch_shapes=[state],
        compiler_params=_params("parallel", "arbitrary"),
        name="hgrn_bwd",
    )(zh, zh, zh, zh, o_fwd, lb_bwd, norm_gain)


def _flash_loop(k_slice, vt_slice, qs_ref, m_ref, l_ref, acc_ref, s_bufs, p_bufs, a_bufs, n_kv, tk):
    assert n_kv >= 2 and n_kv % 2 == 0
    zero = pl.multiple_of(jnp.minimum(pl.program_id(0), 0), tk)
    m_ref[...] = jnp.full(m_ref.shape, -jnp.inf, F32)
    l_ref[...] = jnp.zeros_like(l_ref)
    acc_ref[...] = jnp.zeros_like(acc_ref)

    n_strips = s_bufs[0].shape[0]

    def scores(j, slot):
        off = j * tk
        s = lax.dot_general(k_slice(off), qs_ref[...], (((1,), (1,)), ((), ())),
                            preferred_element_type=F32)
        for c in range(n_strips):
            s_bufs[slot][c] = s[:, c * LANES:(c + 1) * LANES]

    def softmax(slot):
        s_ref, p_ref, a_ref = s_bufs[slot], p_bufs[slot], a_bufs[slot]
        for c in range(n_strips):
            cs = slice(c * LANES, (c + 1) * LANES)
            m_old = m_ref[:, cs]
            m_new = jnp.maximum(m_old, jnp.max(s_ref[c], axis=0, keepdims=True))
            alpha = jnp.exp2(m_old - m_new)
            p = jnp.exp2(s_ref[c, pl.ds(zero, tk), :] - m_new)
            l_ref[:, cs] = alpha * l_ref[:, cs] + jnp.sum(p, axis=0, keepdims=True)
            p_ref[c] = p.astype(BF16)
            a_ref[:, cs] = alpha
            m_ref[:, cs] = m_new

    def weighted(j, slot):
        off = j * tk
        p = jnp.concatenate([p_bufs[slot][c] for c in range(n_strips)], axis=1)
        acc_ref[...] = a_bufs[slot][...] * acc_ref[...] + jnp.dot(vt_slice(off), p,
                                                                  preferred_element_type=F32)

    scores(0, 0)
    softmax(0)
    scores(1, 1)

    for j in range(1, n_kv - 1, 2):
        weighted(j - 1, 0)
        softmax(1)
        scores(j + 1, 0)
        weighted(j, 1)
        softmax(0)
        scores(j + 2, 1)
    weighted(n_kv - 2, 0)
    softmax(1)
    weighted(n_kv - 1, 1)


def _flash_scratch(tk, w, d, dv):
    return [pltpu.VMEM((w, d), BF16), pltpu.VMEM((1, w), F32), pltpu.VMEM((1, w), F32),
            pltpu.VMEM((dv, w), F32), pltpu.VMEM((w // LANES, tk, LANES), F32),
            pltpu.VMEM((w // LANES, tk, LANES), F32), pltpu.VMEM((w // LANES, tk, LANES), BF16),
            pltpu.VMEM((w // LANES, tk, LANES), BF16), pltpu.VMEM((1, w), F32), pltpu.VMEM((1, w), F32)]


def _mla_attn_kernel(q_ref, k_ref, vt_ref, o_ref, qs_ref, m_ref, l_ref, acc_ref, s0, s1, p0, p1, a0, a1,
                     *, tk):
    s = k_ref.shape[2]
    qs_ref[...] = q_ref[0, 0]
    _flash_loop(lambda off: k_ref[0, 0, pl.ds(off, tk), :],
                lambda off: vt_ref[0, 0, :, pl.ds(off, tk)],
                qs_ref, m_ref, l_ref, acc_ref, (s0, s1), (p0, p1), (a0, a1), s // tk, tk)
    out_t = acc_ref[...] / l_ref[...]
    o_ref[0] = out_t.T.astype(BF16)


def _mla_attn(q, k, vt, tq, tk):
    b, h, s, d = q.shape
    dv = vt.shape[2]
    return pl.pallas_call(
        functools.partial(_mla_attn_kernel, tk=tk),
        grid=(b, h, s // tq),
        in_specs=[
            pl.BlockSpec((1, 1, tq, d), lambda i, j, t: (i, j, t, 0)),
            pl.BlockSpec((1, 1, s, d), lambda i, j, t: (i, j, 0, 0)),
            pl.BlockSpec((1, 1, dv, s), lambda i, j, t: (i, j, 0, 0)),
        ],
        out_specs=pl.BlockSpec((1, tq, dv), lambda i, j, t: (i, t, j)),
        out_shape=jax.ShapeDtypeStruct((b, s, h * dv), BF16),
        scratch_shapes=_flash_scratch(tk, tq, d, dv),
        compiler_params=_params("parallel", "parallel", "arbitrary"),
        name="mla_attn",
    )(q, k, vt)


def _diff_attn_kernel(q_ref, k_ref, vt_ref, lq1_ref, lk1_ref, lq2_ref, lk2_ref, ng_ref, o_ref,
                      qs_ref, m_ref, l_ref, acc_ref, s0, s1, p0, p1, a0, a1, *, tk, lambda_init):
    s = k_ref.shape[1]
    tq = q_ref.shape[1]
    q = q_ref[0]
    lane = lax.broadcasted_iota(jnp.int32, q.shape, 1)
    zero = jnp.zeros_like(q)
    qs_ref[0:tq, :] = jnp.where(lane < DIFF_DIM, q, zero)
    qs_ref[tq:2 * tq, :] = jnp.where(lane < DIFF_DIM, zero, q)
    _flash_loop(lambda off: k_ref[0, pl.ds(off, tk), :],
                lambda off: vt_ref[0, 0, :, pl.ds(off, tk)],
                qs_ref, m_ref, l_ref, acc_ref, (s0, s1), (p0, p1), (a0, a1), s // tk, tk)
    lam =(jnp.exp(jnp.sum(lq1_ref[...] * lk1_ref[...], axis=1, keepdims=True))
           - jnp.exp(jnp.sum(lq2_ref[...] * lk2_ref[...], axis=1, keepdims=True)) + lambda_init)
    soft = acc_ref[...] / l_ref[...]
    out_t = soft[:, 0:tq] - lam * soft[:, tq:2 * tq]
    y = _rms(out_t.T, ng_ref[...]) * (1.0 - lambda_init)
    o_ref[0] = y.astype(BF16)


def _diff_attn(q, k, vt, lq1, lk1, lq2, lk2, norm_gain, lambda_init, tq, tk):
    b, s, d = q.shape
    h = vt.shape[1]
    dv = vt.shape[2]
    small = _full(lq1.shape)
    return pl.pallas_call(
        functools.partial(_diff_attn_kernel, tk=tk, lambda_init=lambda_init),
        grid=(b, h, s // tq),
        in_specs=[
            pl.BlockSpec((1, tq, dv), lambda i, j, t: (i, t, j)),
            pl.BlockSpec((1, s, dv), lambda i, j, t: (i, 0, j)),
            pl.BlockSpec((1, 1, dv, s), lambda i, j, t: (i, j, 0, 0)),
            small, small, small, small, _full(norm_gain.shape),
        ],
        out_specs=pl.BlockSpec((1, tq, dv), lambda i, j, t: (i, t, j)),
        out_shape=jax.ShapeDtypeStruct((b, s, d), BF16),
        scratch_shapes=_flash_scratch(tk, 2 * tq, dv, dv),
        compiler_params=_params("parallel", "parallel", "arbitrary"),
        name="diff_attn",
    )(q, k, vt, lq1, lk1, lq2, lk2, norm_gain)


def _mix_ffn_kernel(*refs, n_mix, final):
    x_ref = refs[0]
    mix_refs = refs[1:1 + n_mix]
    wo_refs = refs[1 + n_mix:1 + 2 * n_mix]
    g_ref, wg_ref, wu_ref, wd_ref, gf_ref, out_ref, x1_ref, xn_ref, acc_ref = refs[1 + 2 * n_mix:]
    kk = pl.program_id(1)

    @pl.when(kk == 0)
    def _():
        x1 = x_ref[...]
        for m_ref, w_ref in zip(mix_refs, wo_refs):
            x1 = x1 + jnp.dot(m_ref[...], w_ref[...], preferred_element_type=F32)
        x1_ref[...] = x1
        xn_ref[...] = _rms(x1, g_ref[...]).astype(BF16)
        acc_ref[...] = jnp.zeros_like(acc_ref)

    xn = xn_ref[...]
    gate = jnp.dot(xn, wg_ref[...], preferred_element_type=F32)
    up = jnp.dot(xn, wu_ref[...], preferred_element_type=F32)
    hid = (gate * jax.nn.sigmoid(gate) * up).astype(BF16)
    acc_ref[...] += jnp.dot(hid, wd_ref[...], preferred_element_type=F32)

    @pl.when(kk == pl.num_programs(1) - 1)
    def _():
        y = x1_ref[...] + acc_ref[...]
        if final:
            y = _rms(y, gf_ref[...])
        out_ref[...] = y


def _mix_ffn(x, mixes, w_outs, gain, wg, wu, wd, final_gain, final, tm, th):
    t, d = x.shape
    hidden = wg.shape[1]
    n_mix = len(mixes)
    row = lambda i, k: (i, 0)
    in_specs = [pl.BlockSpec((tm, d), row)]
    in_specs += [pl.BlockSpec((tm, m.shape[1]), row) for m in mixes]
    in_specs += [_full(w.shape) for w in w_outs]
    in_specs += [
        _full(gain.shape),
        pl.BlockSpec((d, th), lambda i, k: (0, k)),
        pl.BlockSpec((d, th), lambda i, k: (0, k)),
        pl.BlockSpec((th, d), lambda i, k: (k, 0)),
        _full(final_gain.shape),
    ]
    return pl.pallas_call(
        functools.partial(_mix_ffn_kernel, n_mix=n_mix, final=final),
        grid=(t // tm, hidden // th),
        in_specs=in_specs,
        out_specs=pl.BlockSpec((tm, d), row),
        out_shape=jax.ShapeDtypeStruct((t, d), F32),
        scratch_shapes=[pltpu.VMEM((tm, d), F32), pltpu.VMEM((tm, d), BF16), pltpu.VMEM((tm, d), F32)],
        compiler_params=_params("parallel", "arbitrary"),
        name="mix_ffn",
    )(x, *mixes, *w_outs, gain, wg, wu, wd, final_gain)


def _rope_tables(seq_len):
    dim = MLA_ROPE
    inv = 1.0 / (ROPE_THETA ** (jnp.arange(0, dim, 2, dtype=F32) / dim))
    ang = jnp.arange(seq_len, dtype=F32)[:, None] * inv[None, :]
    cos, sin = jnp.cos(ang), jnp.sin(ang)
    cos_t = jnp.concatenate([cos, cos, cos, cos], axis=1)
    sin_t = jnp.concatenate([-sin, sin, -sin, sin], axis=1)
    return cos_t, sin_t


def _swap_halves(w):
    half = w.shape[-1] // 2
    return jnp.concatenate([w[..., half:], w[..., :half]], axis=-1)


def _pad_heads(w):
    kdim, h, r = w.shape
    return jnp.concatenate([w, jnp.zeros_like(w)], axis=-1).reshape(kdim, h * 2 * r)


def _row(v):
    return v.reshape(1, -1).astype(F32)


def kernel(x, norm_attn, norm_ffn, ffn_w_gate, ffn_w_up, ffn_w_down, ab_w_in, hgrn_lower_bound, hgrn_out_norm,
           mla_q_norm, mla_w_uq, mla_kv_norm, mla_w_ukv, ab_w_out, c_w_in, diff_lambda_q1, diff_lambda_k1,
           diff_lambda_q2, diff_lambda_k2, diff_out_norm, c_w_out, final_norm):
    b, s, d = x.shape
    depth = norm_attn.shape[0]
    assert DIFF_DIM == MLA_ROPE and d == DIFF_HEADS * 2 * DIFF_DIM and d == 2 * HGRN_W
    assert s % 512 == 0
    tm_proj = 512
    tm_ffn = 1024 if (b * s) % 1024 == 0 else 512
    th_ffn = 256
    tq_mla, tq_diff, tk = 512, 256, min(512, s // 2)

    cos_t, sin_t = _rope_tables(s)
    lower_bounds = jnp.cumsum(jax.nn.softmax(hgrn_lower_bound.astype(F32), axis=1), axis=1)

    for layer in range(depth):
        j = layer // 2
        gain = _row(norm_attn[layer])
        if layer % 2 == 0:
            w_in = ab_w_in[j]
            c_h = 5 * HGRN_W
            c_q = c_h + MLA_Q_LORA
            c_kv = c_q + MLA_KV_LORA
            w_kr = w_in[:, c_kv:]
            w_krs = _swap_halves(w_kr)
            wh = w_in[:, :c_h].astype(BF16)
            wm = jnp.concatenate([w_in[:, c_h:c_kv], w_kr, w_kr, w_krs, w_krs], axis=1).astype(BF16)
            wuq = mla_w_uq[j].reshape(MLA_Q_LORA, MLA_HEADS, MLA_NOPE + MLA_ROPE)
            wqn = wuq[..., :MLA_NOPE].reshape(MLA_Q_LORA, MLA_HEADS * MLA_NOPE).astype(BF16)
            wqr = _pad_heads(wuq[..., MLA_NOPE:]).astype(BF16)
            wqs = _pad_heads(_swap_halves(wuq[..., MLA_NOPE:])).astype(BF16)
            wukv = mla_w_ukv[j].reshape(MLA_KV_LORA, MLA_HEADS, MLA_NOPE + MLA_V)
            wkv = jnp.concatenate([wukv[..., :MLA_NOPE].reshape(MLA_KV_LORA, -1),
                                   wukv[..., MLA_NOPE:].reshape(MLA_KV_LORA, -1)], axis=1).astype(BF16)
            zh, q, k, vt = _proj0(x, gain, wh, wm, _row(mla_q_norm[j]), wqn, wqr, wqs, _row(mla_kv_norm[j]),
                                  wkv, cos_t, sin_t, tm_proj)
            o_a = _hgrn(zh, _row(lower_bounds[0, j]), _row(lower_bounds[1, j]), _row(hgrn_out_norm[j]))
            o_b = _mla_attn(q, k, vt, tq_mla, tk)
            w_out = ab_w_out[j].astype(BF16)
            mixes = [o_a.reshape(b * s, HGRN_W), o_b.reshape(b * s, MLA_HEADS * MLA_V)]
            w_outs = [w_out[:HGRN_W], w_out[HGRN_W:]]
        else:
            lambda_init = 0.8 - 0.6 * math.exp(-0.3 * layer)
            q, k, vt = _proj1(x, gain, c_w_in[j].astype(BF16), cos_t, sin_t, tm_proj)
            o_c = _diff_attn(q, k, vt, _row(diff_lambda_q1[j]), _row(diff_lambda_k1[j]),
                             _row(diff_lambda_q2[j]), _row(diff_lambda_k2[j]), _row(diff_out_norm[j]),
                             lambda_init, tq_diff, tk)
            mixes = [o_c.reshape(b * s, d)]
            w_outs = [c_w_out[j].astype(BF16)]
        x = _mix_ffn(x.reshape(b * s, d), mixes, w_outs, _row(norm_ffn[layer]),
                     ffn_w_gate[layer].astype(BF16), ffn_w_up[layer].astype(BF16),
                     ffn_w_down[layer].astype(BF16), _row(final_norm),
                     layer == depth - 1, tm_ffn, th_ffn).reshape(b, s, d)
    return x
```

```python
import functools
import math

import jax
import jax.numpy as jnp
from jax import lax
from jax.experimental import pallas as pl
from jax.experimental.pallas import tpu as pltpu

F32 = jnp.float32
BF16 = jnp.bfloat16

NORM_EPS = 1e-6
ROPE_THETA = 10000.0
LANES = 128
VMEM_LIMIT = 56 * 1024 * 1024

HGRN_HEADS = 4
HGRN_DIM = 128
HGRN_W = HGRN_HEADS * HGRN_DIM
HGRN_CHUNK = 64
HGRN_BLOCK = 256

MLA_HEADS = 4
MLA_NOPE = 128
MLA_ROPE = 64
MLA_V = 128
MLA_Q_LORA = 384
MLA_KV_LORA = 256
MLA_QK_PAD = 256
LOG2_E = math.log2(math.e)
MLA_SCALE = (MLA_NOPE + MLA_ROPE) ** -0.5 * LOG2_E

DIFF_HEADS = 8
DIFF_DIM = 64
DIFF_SCALE = DIFF_DIM ** -0.5 * LOG2_E


def _rms(x, gain):
    ms = jnp.mean(x * x, axis=-1, keepdims=True)
    return x * lax.rsqrt(ms + NORM_EPS) * gain


def _params(*sem):
    return pltpu.CompilerParams(dimension_semantics=sem, vmem_limit_bytes=VMEM_LIMIT)


def _full(shape):
    n = len(shape)
    return pl.BlockSpec(shape, lambda *_: (0,) * n)


def _proj0_kernel(x_ref, g_ref, wh_ref, wm_ref, qg_ref, wqn_ref, wqr_ref, wqs_ref, kvg_ref, wkv_ref,
                  cos_ref, sin_ref, zh_ref, q_ref, k_ref, vt_ref):
    xn = _rms(x_ref[0], g_ref[...]).astype(BF16)
    zh_ref[0] = jnp.dot(xn, wh_ref[...], preferred_element_type=F32)
    zm = jnp.dot(xn, wm_ref[...], preferred_element_type=F32)
    cos = cos_ref[...]
    sin = sin_ref[...]
    c0 = MLA_Q_LORA
    c1 = c0 + MLA_KV_LORA
    cqn = _rms(zm[:, :c0], qg_ref[...]).astype(BF16)
    ckvn = _rms(zm[:, c0:c1], kvg_ref[...]).astype(BF16)
    k_rope = zm[:, c1:c1 + LANES] * cos + zm[:, c1 + LANES:c1 + 2 * LANES] * sin
    k_rope = jnp.where(lax.broadcasted_iota(jnp.int32, k_rope.shape, 1) < MLA_ROPE, k_rope, 0.0).astype(BF16)
    qn = jnp.dot(cqn, wqn_ref[...], preferred_element_type=F32)
    qr = jnp.dot(cqn, wqr_ref[...], preferred_element_type=F32)
    qs = jnp.dot(cqn, wqs_ref[...], preferred_element_type=F32)
    kv = jnp.dot(ckvn, wkv_ref[...], preferred_element_type=F32)
    hw = MLA_HEADS * MLA_NOPE
    for h in range(MLA_HEADS):
        hs = slice(h * LANES, (h + 1) * LANES)
        q_ref[0, h, :, 0:LANES] = (qn[:, hs] * MLA_SCALE).astype(BF16)
        q_ref[0, h, :, LANES:2 * LANES] = ((qr[:, hs] * cos + qs[:, hs] * sin) * MLA_SCALE).astype(BF16)
        k_ref[0, h, :, 0:LANES] = kv[:, hs].astype(BF16)
        k_ref[0, h, :, LANES:2 * LANES] = k_rope
        vt_ref[0, h] = kv[:, hw + h * LANES:hw + (h + 1) * LANES].T.astype(BF16)


def _proj0(x, gain, wh, wm, qg, wqn, wqr, wqs, kvg, wkv, cos_t, sin_t, tm):
    b, s, d = x.shape
    grid = (b, s // tm)
    return pl.pallas_call(
        _proj0_kernel,
        grid=grid,
        in_specs=[
            pl.BlockSpec((1, tm, d), lambda i, j: (i, j, 0)),
            _full(gain.shape), _full(wh.shape), _full(wm.shape), _full(qg.shape), _full(wqn.shape),
            _full(wqr.shape), _full(wqs.shape), _full(kvg.shape), _full(wkv.shape),
            pl.BlockSpec((tm, LANES), lambda i, j: (j, 0)),
            pl.BlockSpec((tm, LANES), lambda i, j: (j, 0)),
        ],
        out_specs=[
            pl.BlockSpec((1, tm, wh.shape[1]), lambda i, j: (i, j, 0)),
            pl.BlockSpec((1, MLA_HEADS, tm, MLA_QK_PAD), lambda i, j: (i, 0, j, 0)),
            pl.BlockSpec((1, MLA_HEADS, tm, MLA_QK_PAD), lambda i, j: (i, 0, j, 0)),
            pl.BlockSpec((1, MLA_HEADS, MLA_V, tm), lambda i, j: (i, 0, 0, j)),
        ],
        out_shape=[
            jax.ShapeDtypeStruct((b, s, wh.shape[1]), F32),
            jax.ShapeDtypeStruct((b, MLA_HEADS, s, MLA_QK_PAD), BF16),
            jax.ShapeDtypeStruct((b, MLA_HEADS, s, MLA_QK_PAD), BF16),
            jax.ShapeDtypeStruct((b, MLA_HEADS, MLA_V, s), BF16),
        ],
        compiler_params=_params("parallel", "parallel"),
        name="proj0",
    )(x, gain, wh, wm, qg, wqn, wqr, wqs, kvg, wkv, cos_t, sin_t)


def _proj1_kernel(x_ref, g_ref, w_ref, cos_ref, sin_ref, q_ref, k_ref, vt_ref):
    d = x_ref.shape[2]
    xn = _rms(x_ref[0], g_ref[...]).astype(BF16)
    z = jnp.dot(xn, w_ref[...], preferred_element_type=F32)
    cos = cos_ref[...]
    sin = sin_ref[...]
    lane = lax.broadcasted_iota(jnp.int32, cos.shape, 1)
    first_half = (lane % DIFF_DIM) < (DIFF_DIM // 2)
    half = DIFF_DIM // 2
    for h in range(DIFF_HEADS):
        hs = slice(h * LANES, (h + 1) * LANES)
        for base, ref, scale in ((0, q_ref, DIFF_SCALE), (d, k_ref, 1.0)):
            t = z[:, base + h * LANES:base + (h + 1) * LANES]
            partner = jnp.where(first_half, pltpu.roll(t, LANES - half, axis=1), pltpu.roll(t, half, axis=1))
            ref[0, :, hs] = ((t * cos + partner * sin) * scale).astype(BF16)
        vt_ref[0, h] = z[:, 2 * d + h * LANES:2 * d + (h + 1) * LANES].T.astype(BF16)


def _proj1(x, gain, w, cos_t, sin_t, tm):
    b, s, d = x.shape
    return pl.pallas_call(
        _proj1_kernel,
        grid=(b, s // tm),
        in_specs=[
            pl.BlockSpec((1, tm, d), lambda i, j: (i, j, 0)),
            _full(gain.shape), _full(w.shape),
            pl.BlockSpec((tm, LANES), lambda i, j: (j, 0)),
            pl.BlockSpec((tm, LANES), lambda i, j: (j, 0)),
        ],
        out_specs=[
            pl.BlockSpec((1, tm, d), lambda i, j: (i, j, 0)),
            pl.BlockSpec((1, tm, d), lambda i, j: (i, j, 0)),
            pl.BlockSpec((1, DIFF_HEADS, 2 * DIFF_DIM, tm), lambda i, j: (i, 0, 0, j)),
        ],
        out_shape=[
            jax.ShapeDtypeStruct((b, s, d), BF16),
            jax.ShapeDtypeStruct((b, s, d), BF16),
            jax.ShapeDtypeStruct((b, DIFF_HEADS, 2 * DIFF_DIM, s), BF16),
        ],
        compiler_params=_params("parallel", "parallel"),
        name="proj1",
    )(x, gain, w, cos_t, sin_t)


def _hgrn_chunk(q, k, v, g, st_ref, rev):
    n = HGRN_CHUNK
    row = lax.broadcasted_iota(jnp.int32, g.shape, 0)
    blk = lax.broadcasted_iota(jnp.int32, (n, n), 0)
    blk_t = lax.broadcasted_iota(jnp.int32, (n, n), 1)
    p_sum = g
    x_sum = jnp.zeros_like(g)
    levels = []
    m = 1
    while m < n:
        in_right = (row % (2 * m)) >= m
        q_rows = jnp.logical_not(in_right) if rev else in_right
        qt = jnp.where(q_rows, q * jnp.exp(p_sum), 0.0).astype(BF16)
        kt = jnp.where(q_rows, 0.0, k * jnp.exp(x_sum)).astype(BF16)
        same_block = (blk // (2 * m)) == (blk_t // (2 * m)) if 2 * m < n else None
        levels.append((qt, kt, same_block))
        total = p_sum + x_sum
        from_left = pltpu.roll(total, m, axis=0)
        from_right = pltpu.roll(total, n - m, axis=0)
        if rev:
            p_sum = p_sum + jnp.where(in_right, 0.0, from_right)
            x_sum = x_sum + jnp.where(in_right, from_left, 0.0)
        else:
            p_sum = p_sum + jnp.where(in_right, from_left, 0.0)
            x_sum = x_sum + jnp.where(in_right, 0.0, from_right)
        m *= 2
    q_in = (q * jnp.exp(p_sum)).astype(BF16)
    k_out = (k * jnp.exp(x_sum)).astype(BF16)
    chunk_decay = jnp.exp(p_sum[0:1] + x_sum[0:1])
    qk = q * k
    vb = v.astype(BF16)
    nt = (((1,), (1,)), ((), ()))
    tn = (((0,), (0,)), ((), ()))
    outs = []
    for h in range(HGRN_HEADS):
        hs = slice(h * HGRN_DIM, (h + 1) * HGRN_DIM)
        a = None
        for qt, kt, same_block in levels:
            part = lax.dot_general(qt[:, hs], kt[:, hs], nt, preferred_element_type=F32)
            if same_block is not None:
                part = jnp.where(same_block, part, 0.0)
            a = part if a is None else a + part
        st = st_ref[h]
        o = jnp.dot(a.astype(BF16), vb[:, hs], preferred_element_type=F32)
        o = o + lax.dot_general(q_in[:, hs], st.astype(BF16), nt, preferred_element_type=F32)
        o = o + jnp.sum(qk[:, hs], axis=1, keepdims=True) * v[:, hs]
        st_ref[h] = st * chunk_decay[:, hs] + lax.dot_general(vb[:, hs], k_out[:, hs], tn,
                                                               preferred_element_type=F32)
        outs.append(o)
    return outs


def _hgrn_gates(qh, fp, lb):
    q = qh * jax.nn.sigmoid(qh) * (HGRN_DIM ** -0.5)
    g = jnp.log(lb + (1.0 - lb) * jax.nn.sigmoid(fp))
    k = (1.0 - lb) * jax.nn.sigmoid(-fp)
    return q, k, g


def _hgrn_fwd_kernel(q_ref, f_ref, v_ref, lb_ref, o_ref, st_ref):
    @pl.when(pl.program_id(1) == 0)
    def _():
        st_ref[...] = jnp.zeros_like(st_ref)

    lb = lb_ref[...]
    for c in range(HGRN_BLOCK // HGRN_CHUNK):
        rs = slice(c * HGRN_CHUNK, (c + 1) * HGRN_CHUNK)
        q, k, g = _hgrn_gates(q_ref[0, rs, :], f_ref[0, rs, :], lb)
        outs = _hgrn_chunk(q, k, v_ref[0, rs, :], g, st_ref, False)
        for h, o in enumerate(outs):
            o_ref[0, rs, h * HGRN_DIM:(h + 1) * HGRN_DIM] = o


def _hgrn_bwd_kernel(q_ref, f_ref, v_ref, gate_ref, of_ref, lb_ref, ng_ref, o_ref, st_ref):
    @pl.when(pl.program_id(1) == 0)
    def _():
        st_ref[...] = jnp.zeros_like(st_ref)

    lb = lb_ref[...]
    for c in reversed(range(HGRN_BLOCK // HGRN_CHUNK)):
        rs = slice(c * HGRN_CHUNK, (c + 1) * HGRN_CHUNK)
        q, k, g = _hgrn_gates(q_ref[0, rs, :], f_ref[0, rs, :], lb)
        outs = _hgrn_chunk(q, k, v_ref[0, rs, :], g, st_ref, True)
        gate = gate_ref[0, rs, :]
        gate = gate * jax.nn.sigmoid(gate)
        for h, o in enumerate(outs):
            hs = slice(h * HGRN_DIM, (h + 1) * HGRN_DIM)
            y = _rms(o + of_ref[0, rs, hs], ng_ref[:, hs])
            o_ref[0, rs, hs] = (y * gate[:, hs]).astype(BF16)


def _hgrn(zh, lb_fwd, lb_bwd, norm_gain):
    b, s, _ = zh.shape
    nb = s // HGRN_BLOCK
    state = pltpu.VMEM((HGRN_HEADS, HGRN_DIM, HGRN_DIM), F32)

    def col(c, rev):
        if rev:
            return pl.BlockSpec((1, HGRN_BLOCK, HGRN_W), lambda i, j: (i, nb - 1 - j, c))
        return pl.BlockSpec((1, HGRN_BLOCK, HGRN_W), lambda i, j: (i, j, c))

    o_fwd = pl.pallas_call(
        _hgrn_fwd_kernel,
        grid=(b, nb),
        in_specs=[col(0, False), col(1, False), col(3, False), _full(lb_fwd.shape)],
        out_specs=col(0, False),
        out_shape=jax.ShapeDtypeStruct((b, s, HGRN_W), F32),
        scratch_shapes=[state],
        compiler_params=_params("parallel", "arbitrary"),
        name="hgrn_fwd",
    )(zh, zh, zh, lb_fwd)
    return pl.pallas_call(
        _hgrn_bwd_kernel,
        grid=(b, nb),
        in_specs=[col(0, True), col(2, True), col(3, True), col(4, True), col(0, True),
                  _full(lb_bwd.shape), _full(norm_gain.shape)],
        out_specs=col(0, True),
        out_shape=jax.ShapeDtypeStruct((b, s, HGRN_W), BF16),
        scratch_shapes=[state],
        compiler_params=_params("parallel", "arbitrary"),
        name="hgrn_bwd",
    )(zh, zh, zh, zh, o_fwd, lb_bwd, norm_gain)


MIN_FAST_ROW_SUM = 2.0 ** -60


def _fast_sweep(k_slice, vt_slice, qst_ref, shift, l8_ref, acc_ref, p_bufs, n_kv, tk):
    w = qst_ref.shape[1]
    l8_ref[...] = jnp.zeros_like(l8_ref)
    acc_ref[...] = jnp.zeros_like(acc_ref)

    def probs(j, slot):
        s = jnp.dot(k_slice(j * tk, tk), qst_ref[...], preferred_element_type=F32)
        p = jnp.exp2(s - shift)
        l8_ref[...] += jnp.sum(p.reshape(tk // 8, 8, w), axis=0)
        p_bufs[slot][...] = p.astype(BF16)

    def weighted(j, slot):
        acc_ref[...] += jnp.dot(vt_slice(j * tk, tk), p_bufs[slot][...], preferred_element_type=F32)

    probs(0, 0)
    for j in range(1, n_kv):
        weighted(j - 1, (j - 1) % 2)
        probs(j, j % 2)
    weighted(n_kv - 1, (n_kv - 1) % 2)


def _online_sweep(k_slice, vt_slice, qst_ref, m_ref, l_ref, acc_ref, n_kv, tk):
    m_ref[...] = jnp.full(m_ref.shape, -jnp.inf, F32)
    l_ref[...] = jnp.zeros_like(l_ref)
    acc_ref[...] = jnp.zeros_like(acc_ref)

    def body(j, carry):
        off = pl.multiple_of(j * tk, tk)
        s = jnp.dot(k_slice(off, tk), qst_ref[...], preferred_element_type=F32)
        m_old = m_ref[...]
        m_new = jnp.maximum(m_old, jnp.max(s, axis=0, keepdims=True))
        alpha = jnp.exp2(m_old - m_new)
        p = jnp.exp2(s - m_new)
        l_ref[...] = alpha * l_ref[...] + jnp.sum(p, axis=0, keepdims=True)
        acc_ref[...] = alpha * acc_ref[...] + jnp.dot(vt_slice(off, tk), p.astype(BF16),
                                                      preferred_element_type=F32)
        m_ref[...] = m_new
        return carry

    lax.fori_loop(0, n_kv, body, 0)


def _softmax_sweep(k_slice, vt_slice, qst_ref, shift, m_ref, l_ref, l8_ref, acc_ref, p_bufs, n_keys, tk):
    _fast_sweep(k_slice, vt_slice, qst_ref, shift, l8_ref, acc_ref, p_bufs, n_keys // tk, tk)
    row_sum = jnp.sum(l8_ref[...], axis=0, keepdims=True)
    l_ref[...] = row_sum
    n_bad = jnp.sum(jnp.where(row_sum >= MIN_FAST_ROW_SUM, 0.0, 1.0))

    @pl.when(n_bad > 0.0)
    def _():
        tk_online = min(tk, 256)
        _online_sweep(k_slice, vt_slice, qst_ref, m_ref, l_ref, acc_ref, n_keys // tk_online, tk_online)


def _attn_scratch(tk, w, d, dv):
    return [pltpu.VMEM((d, w), BF16), pltpu.VMEM((1, w), F32), pltpu.VMEM((1, w), F32), pltpu.VMEM((8, w), F32),
            pltpu.VMEM((dv, w), F32), pltpu.VMEM((tk, w), BF16), pltpu.VMEM((tk, w), BF16),
            pltpu.VMEM((8, LANES), F32)]


def _max_row_norm(k, lane_mask):
    k2 = k.astype(F32)
    k2 = k2 * k2
    if lane_mask is not None:
        k2 = jnp.where(lane_mask, k2, 0.0)
    return jnp.sqrt(jnp.max(jnp.sum(k2, axis=1, keepdims=True), axis=0, keepdims=True))


def _mla_attn_kernel(q_ref, k_ref, vt_ref, o_ref, qst_ref, m_ref, l_ref, l8_ref, acc_ref, p0, p1, kn_ref,
                     *, tk):
    @pl.when(pl.program_id(2) == 0)
    def _():
        kn_ref[...] = jnp.broadcast_to(_max_row_norm(k_ref[0, 0], None), kn_ref.shape)

    qt = q_ref[0, 0].astype(F32).T
    qst_ref[...] = qt.astype(BF16)
    shift = jnp.sqrt(jnp.sum(qt * qt, axis=0, keepdims=True)) * kn_ref[0:1, 0:1]
    _softmax_sweep(lambda off, n: k_ref[0, 0, pl.ds(off, n), :],
                   lambda off, n: vt_ref[0, 0, :, pl.ds(off, n)],
                   qst_ref, shift, m_ref, l_ref, l8_ref, acc_ref, (p0, p1), k_ref.shape[2], tk)
    out_t = acc_ref[...] / l_ref[...]
    o_ref[0] = out_t.T.astype(BF16)


def _mla_attn(q, k, vt, tq, tk):
    b, h, s, d = q.shape
    dv = vt.shape[2]
    return pl.pallas_call(
        functools.partial(_mla_attn_kernel, tk=tk),
        grid=(b, h, s // tq),
        in_specs=[
            pl.BlockSpec((1, 1, tq, d), lambda i, j, t: (i, j, t, 0)),
            pl.BlockSpec((1, 1, s, d), lambda i, j, t: (i, j, 0, 0)),
            pl.BlockSpec((1, 1, dv, s), lambda i, j, t: (i, j, 0, 0)),
        ],
        out_specs=pl.BlockSpec((1, tq, dv), lambda i, j, t: (i, t, j)),
        out_shape=jax.ShapeDtypeStruct((b, s, h * dv), BF16),
        scratch_shapes=_attn_scratch(tk, tq, d, dv),
        compiler_params=_params("parallel", "parallel", "arbitrary"),
        name="mla_attn",
    )(q, k, vt)


def _diff_attn_kernel(q_ref, k_ref, vt_ref, lq1_ref, lk1_ref, lq2_ref, lk2_ref, ng_ref, o_ref,
                      qst_ref, m_ref, l_ref, l8_ref, acc_ref, p0, p1, kn_ref, *, tk, lambda_init):
    tq = q_ref.shape[1]

    @pl.when(pl.program_id(2) == 0)
    def _():
        k_all = k_ref[0]
        first = lax.broadcasted_iota(jnp.int32, k_all.shape, 1) < DIFF_DIM
        kn_ref[0:1, :] = jnp.broadcast_to(_max_row_norm(k_all, first), (1, LANES))
        kn_ref[1:2, :] = jnp.broadcast_to(_max_row_norm(k_all, jnp.logical_not(first)), (1, LANES))

    q = q_ref[0].astype(F32)
    first = lax.broadcasted_iota(jnp.int32, q.shape, 1) < DIFF_DIM
    q1t = jnp.where(first, q, 0.0).T
    q2t = jnp.where(first, 0.0, q).T
    qst_ref[:, 0:tq] = q1t.astype(BF16)
    qst_ref[:, tq:2 * tq] = q2t.astype(BF16)
    shift = jnp.concatenate([jnp.sqrt(jnp.sum(q1t * q1t, axis=0, keepdims=True)) * kn_ref[0:1, 0:1],
                             jnp.sqrt(jnp.sum(q2t * q2t, axis=0, keepdims=True)) * kn_ref[1:2, 0:1]], axis=1)
    _softmax_sweep(lambda off, n: k_ref[0, pl.ds(off, n), :],
                   lambda off, n: vt_ref[0, 0, :, pl.ds(off, n)],
                   qst_ref, shift, m_ref, l_ref, l8_ref, acc_ref, (p0, p1), k_ref.shape[1], tk)
    lam = (jnp.exp(jnp.sum(lq1_ref[...] * lk1_ref[...], axis=1, keepdims=True))
           - jnp.exp(jnp.sum(lq2_ref[...] * lk2_ref[...], axis=1, keepdims=True)) + lambda_init)
    soft = acc_ref[...] / l_ref[...]
    out_t = soft[:, 0:tq] - lam * soft[:, tq:2 * tq]
    y = _rms(out_t.T, ng_ref[...]) * (1.0 - lambda_init)
    o_ref[0] = y.astype(BF16)


def _diff_attn(q, k, vt, lq1, lk1, lq2, lk2, norm_gain, lambda_init, tq, tk):
    b, s, d = q.shape
    h = vt.shape[1]
    dv = vt.shape[2]
    small = _full(lq1.shape)
    return pl.pallas_call(
        functools.partial(_diff_attn_kernel, tk=tk, lambda_init=lambda_init),
        grid=(b, h, s // tq),
        in_specs=[
            pl.BlockSpec((1, tq, dv), lambda i, j, t: (i, t, j)),
            pl.BlockSpec((1, s, dv), lambda i, j, t: (i, 0, j)),
            pl.BlockSpec((1, 1, dv, s), lambda i, j, t: (i, j, 0, 0)),
            small, small, small, small, _full(norm_gain.shape),
        ],
        out_specs=pl.BlockSpec((1, tq, dv), lambda i, j, t: (i, t, j)),
        out_shape=jax.ShapeDtypeStruct((b, s, d), BF16),
        scratch_shapes=_attn_scratch(tk, 2 * tq, dv, dv),
        compiler_params=_params("parallel", "parallel", "arbitrary"),
        name="diff_attn",
    )(q, k, vt, lq1, lk1, lq2, lk2, norm_gain)


def _mix_ffn_kernel(*refs, n_mix, final):
    x_ref = refs[0]
    mix_refs = refs[1:1 + n_mix]
    wo_refs = refs[1 + n_mix:1 + 2 * n_mix]
    g_ref, wg_ref, wu_ref, wd_ref, gf_ref, out_ref, x1_ref, xn_ref, acc_ref = refs[1 + 2 * n_mix:]
    kk = pl.program_id(1)

    @pl.when(kk == 0)
    def _():
        x1 = x_ref[...]
        for m_ref, w_ref in zip(mix_refs, wo_refs):
            x1 = x1 + jnp.dot(m_ref[...], w_ref[...], preferred_element_type=F32)
        x1_ref[...] = x1
        xn_ref[...] = _rms(x1, g_ref[...]).astype(BF16)
        acc_ref[...] = jnp.zeros_like(acc_ref)

    xn = xn_ref[...]
    gate = jnp.dot(xn, wg_ref[...], preferred_element_type=F32)
    up = jnp.dot(xn, wu_ref[...], preferred_element_type=F32)
    hid = (gate * jax.nn.sigmoid(gate) * up).astype(BF16)
    acc_ref[...] += jnp.dot(hid, wd_ref[...], preferred_element_type=F32)

    @pl.when(kk == pl.num_programs(1) - 1)
    def _():
        y = x1_ref[...] + acc_ref[...]
        if final:
            y = _rms(y, gf_ref[...])
        out_ref[...] = y


def _mix_ffn(x, mixes, w_outs, gain, wg, wu, wd, final_gain, final, tm, th):
    t, d = x.shape
    hidden = wg.shape[1]
    n_mix = len(mixes)
    row = lambda i, k: (i, 0)
    in_specs = [pl.BlockSpec((tm, d), row)]
    in_specs += [pl.BlockSpec((tm, m.shape[1]), row) for m in mixes]
    in_specs += [_full(w.shape) for w in w_outs]
    in_specs += [
        _full(gain.shape),
        pl.BlockSpec((d, th), lambda i, k: (0, k)),
        pl.BlockSpec((d, th), lambda i, k: (0, k)),
        pl.BlockSpec((th, d), lambda i, k: (k, 0)),
        _full(final_gain.shape),
    ]
    return pl.pallas_call(
        functools.partial(_mix_ffn_kernel, n_mix=n_mix, final=final),
        grid=(t // tm, hidden // th),
        in_specs=in_specs,
        out_specs=pl.BlockSpec((tm, d), row),
        out_shape=jax.ShapeDtypeStruct((t, d), F32),
        scratch_shapes=[pltpu.VMEM((tm, d), F32), pltpu.VMEM((tm, d), BF16), pltpu.VMEM((tm, d), F32)],
        compiler_params=_params("parallel", "arbitrary"),
        name="mix_ffn",
    )(x, *mixes, *w_outs, gain, wg, wu, wd, final_gain)


def _rope_tables(seq_len):
    dim = MLA_ROPE
    inv = 1.0 / (ROPE_THETA ** (jnp.arange(0, dim, 2, dtype=F32) / dim))
    ang = jnp.arange(seq_len, dtype=F32)[:, None] * inv[None, :]
    cos, sin = jnp.cos(ang), jnp.sin(ang)
    cos_t = jnp.concatenate([cos, cos, cos, cos], axis=1)
    sin_t = jnp.concatenate([-sin, sin, -sin, sin], axis=1)
    return cos_t, sin_t


def _swap_halves(w):
    half = w.shape[-1] // 2
    return jnp.concatenate([w[..., half:], w[..., :half]], axis=-1)


def _pad_heads(w):
    kdim, h, r = w.shape
    return jnp.concatenate([w, jnp.zeros_like(w)], axis=-1).reshape(kdim, h * 2 * r)


def _row(v):
    return v.reshape(1, -1).astype(F32)


def kernel(x, norm_attn, norm_ffn, ffn_w_gate, ffn_w_up, ffn_w_down, ab_w_in, hgrn_lower_bound, hgrn_out_norm,
           mla_q_norm, mla_w_uq, mla_kv_norm, mla_w_ukv, ab_w_out, c_w_in, diff_lambda_q1, diff_lambda_k1,
           diff_lambda_q2, diff_lambda_k2, diff_out_norm, c_w_out, final_norm):
    b, s, d = x.shape
    depth = norm_attn.shape[0]
    assert DIFF_DIM == MLA_ROPE and d == DIFF_HEADS * 2 * DIFF_DIM and d == 2 * HGRN_W
    assert s % 512 == 0
    tm_proj = 512
    tm_ffn = 1024 if (b * s) % 1024 == 0 else 512
    th_ffn = 256
    tq_mla, tq_diff, tk = 512, 512, min(1024, s)

    cos_t, sin_t = _rope_tables(s)
    lower_bounds = jnp.cumsum(jax.nn.softmax(hgrn_lower_bound.astype(F32), axis=1), axis=1)

    for layer in range(depth):
        j = layer // 2
        gain = _row(norm_attn[layer])
        if layer % 2 == 0:
            w_in = ab_w_in[j]
            c_h = 5 * HGRN_W
            c_q = c_h + MLA_Q_LORA
            c_kv = c_q + MLA_KV_LORA
            w_kr = w_in[:, c_kv:]
            w_krs = _swap_halves(w_kr)
            wh = w_in[:, :c_h].astype(BF16)
            wm = jnp.concatenate([w_in[:, c_h:c_kv], w_kr, w_kr, w_krs, w_krs], axis=1).astype(BF16)
            wuq = mla_w_uq[j].reshape(MLA_Q_LORA, MLA_HEADS, MLA_NOPE + MLA_ROPE)
            wqn = wuq[..., :MLA_NOPE].reshape(MLA_Q_LORA, MLA_HEADS * MLA_NOPE).astype(BF16)
            wqr = _pad_heads(wuq[..., MLA_NOPE:]).astype(BF16)
            wqs = _pad_heads(_swap_halves(wuq[..., MLA_NOPE:])).astype(BF16)
            wukv = mla_w_ukv[j].reshape(MLA_KV_LORA, MLA_HEADS, MLA_NOPE + MLA_V)
            wkv = jnp.concatenate([wukv[..., :MLA_NOPE].reshape(MLA_KV_LORA, -1),
                                   wukv[..., MLA_NOPE:].reshape(MLA_KV_LORA, -1)], axis=1).astype(BF16)
            zh, q, k, vt = _proj0(x, gain, wh, wm, _row(mla_q_norm[j]), wqn, wqr, wqs, _row(mla_kv_norm[j]),
                                  wkv, cos_t, sin_t, tm_proj)
            o_a = _hgrn(zh, _row(lower_bounds[0, j]), _row(lower_bounds[1, j]), _row(hgrn_out_norm[j]))
            o_b = _mla_attn(q, k, vt, tq_mla, tk)
            w_out = ab_w_out[j].astype(BF16)
            mixes = [o_a.reshape(b * s, HGRN_W), o_b.reshape(b * s, MLA_HEADS * MLA_V)]
            w_outs = [w_out[:HGRN_W], w_out[HGRN_W:]]
        else:
            lambda_init = 0.8 - 0.6 * math.exp(-0.3 * layer)
            q, k, vt = _proj1(x, gain, c_w_in[j].astype(BF16), cos_t, sin_t, tm_proj)
            o_c = _diff_attn(q, k, vt, _row(diff_lambda_q1[j]), _row(diff_lambda_k1[j]),
                             _row(diff_lambda_q2[j]), _row(diff_lambda_k2[j]), _row(diff_out_norm[j]),
                             lambda_init, tq_diff, tk)
            mixes = [o_c.reshape(b * s, d)]
            w_outs = [c_w_out[j].astype(BF16)]
        x = _mix_ffn(x.reshape(b * s, d), mixes, w_outs, _row(norm_ffn[layer]),
                     ffn_w_gate[layer].astype(BF16), ffn_w_up[layer].astype(BF16),
                     ffn_w_down[layer].astype(BF16), _row(final_norm),
                     layer == depth - 1, tm_ffn, th_ffn).reshape(b, s, d)
    return x
```

```python
import functools
import math

import jax
import jax.numpy as jnp
from jax import lax
from jax.experimental import pallas as pl
from jax.experimental.pallas import tpu as pltpu

F32 = jnp.float32
BF16 = jnp.bfloat16

NORM_EPS = 1e-6
ROPE_THETA = 10000.0
LANES = 128
VMEM_LIMIT = 56 * 1024 * 1024

HGRN_HEADS = 4
HGRN_DIM = 128
HGRN_W = HGRN_HEADS * HGRN_DIM
HGRN_CHUNK = 64
HGRN_BLOCK = 256

MLA_HEADS = 4
MLA_NOPE = 128
MLA_ROPE = 64
MLA_V = 128
MLA_Q_LORA = 384
MLA_KV_LORA = 256
MLA_QK_PAD = 256
LOG2_E = math.log2(math.e)
MLA_SCALE = (MLA_NOPE + MLA_ROPE) ** -0.5 * LOG2_E

DIFF_HEADS = 8
DIFF_DIM = 64
DIFF_SCALE = DIFF_DIM ** -0.5 * LOG2_E


def _rms(x, gain):
    ms = jnp.mean(x * x, axis=-1, keepdims=True)
    return x * lax.rsqrt(ms + NORM_EPS) * gain


def _params(*sem):
    return pltpu.CompilerParams(dimension_semantics=sem, vmem_limit_bytes=VMEM_LIMIT)


def _full(shape):
    n = len(shape)
    return pl.BlockSpec(shape, lambda *_: (0,) * n)


def _proj0_kernel(x_ref, g_ref, wh_ref, wm_ref, qg_ref, wqn_ref, wqr_ref, wqs_ref, kvg_ref, wkv_ref,
                  cos_ref, sin_ref, zh_ref, q_ref, k_ref, vt_ref):
    xn = _rms(x_ref[0], g_ref[...]).astype(BF16)
    zh_ref[0] = jnp.dot(xn, wh_ref[...], preferred_element_type=F32)
    zm = jnp.dot(xn, wm_ref[...], preferred_element_type=F32)
    cos = cos_ref[...]
    sin = sin_ref[...]
    c0 = MLA_Q_LORA
    c1 = c0 + MLA_KV_LORA
    cqn = _rms(zm[:, :c0], qg_ref[...]).astype(BF16)
    ckvn = _rms(zm[:, c0:c1], kvg_ref[...]).astype(BF16)
    k_rope = zm[:, c1:c1 + LANES] * cos + zm[:, c1 + LANES:c1 + 2 * LANES] * sin
    k_rope = jnp.where(lax.broadcasted_iota(jnp.int32, k_rope.shape, 1) < MLA_ROPE, k_rope, 0.0).astype(BF16)
    qn = jnp.dot(cqn, wqn_ref[...], preferred_element_type=F32)
    qr = jnp.dot(cqn, wqr_ref[...], preferred_element_type=F32)
    qs = jnp.dot(cqn, wqs_ref[...], preferred_element_type=F32)
    kv = jnp.dot(ckvn, wkv_ref[...], preferred_element_type=F32)
    hw = MLA_HEADS * MLA_NOPE
    for h in range(MLA_HEADS):
        hs = slice(h * LANES, (h + 1) * LANES)
        q_ref[0, h, :, 0:LANES] = (qn[:, hs] * MLA_SCALE).astype(BF16)
        q_ref[0, h, :, LANES:2 * LANES] = ((qr[:, hs] * cos + qs[:, hs] * sin) * MLA_SCALE).astype(BF16)
        k_ref[0, h, :, 0:LANES] = kv[:, hs].astype(BF16)
        k_ref[0, h, :, LANES:2 * LANES] = k_rope
        vt_ref[0, h] = kv[:, hw + h * LANES:hw + (h + 1) * LANES].T.astype(BF16)


def _proj0(x, gain, wh, wm, qg, wqn, wqr, wqs, kvg, wkv, cos_t, sin_t, tm):
    b, s, d = x.shape
    grid = (b, s // tm)
    return pl.pallas_call(
        _proj0_kernel,
        grid=grid,
        in_specs=[
            pl.BlockSpec((1, tm, d), lambda i, j: (i, j, 0)),
            _full(gain.shape), _full(wh.shape), _full(wm.shape), _full(qg.shape), _full(wqn.shape),
            _full(wqr.shape), _full(wqs.shape), _full(kvg.shape), _full(wkv.shape),
            pl.BlockSpec((tm, LANES), lambda i, j: (j, 0)),
            pl.BlockSpec((tm, LANES), lambda i, j: (j, 0)),
        ],
        out_specs=[
            pl.BlockSpec((1, tm, wh.shape[1]), lambda i, j: (i, j, 0)),
            pl.BlockSpec((1, MLA_HEADS, tm, MLA_QK_PAD), lambda i, j: (i, 0, j, 0)),
            pl.BlockSpec((1, MLA_HEADS, tm, MLA_QK_PAD), lambda i, j: (i, 0, j, 0)),
            pl.BlockSpec((1, MLA_HEADS, MLA_V, tm), lambda i, j: (i, 0, 0, j)),
        ],
        out_shape=[
            jax.ShapeDtypeStruct((b, s, wh.shape[1]), F32),
            jax.ShapeDtypeStruct((b, MLA_HEADS, s, MLA_QK_PAD), BF16),
            jax.ShapeDtypeStruct((b, MLA_HEADS, s, MLA_QK_PAD), BF16),
            jax.ShapeDtypeStruct((b, MLA_HEADS, MLA_V, s), BF16),
        ],
        compiler_params=_params("parallel", "parallel"),
        name="proj0",
    )(x, gain, wh, wm, qg, wqn, wqr, wqs, kvg, wkv, cos_t, sin_t)


def _proj1_kernel(x_ref, g_ref, w_ref, cos_ref, sin_ref, q_ref, k_ref, vt_ref):
    d = x_ref.shape[2]
    xn = _rms(x_ref[0], g_ref[...]).astype(BF16)
    z = jnp.dot(xn, w_ref[...], preferred_element_type=F32)
    cos = cos_ref[...]
    sin = sin_ref[...]
    lane = lax.broadcasted_iota(jnp.int32, cos.shape, 1)
    first_half = (lane % DIFF_DIM) < (DIFF_DIM // 2)
    half = DIFF_DIM // 2
    for h in range(DIFF_HEADS):
        hs = slice(h * LANES, (h + 1) * LANES)
        for base, ref, scale in ((0, q_ref, DIFF_SCALE), (d, k_ref, 1.0)):
            t = z[:, base + h * LANES:base + (h + 1) * LANES]
            partner = jnp.where(first_half, pltpu.roll(t, LANES - half, axis=1), pltpu.roll(t, half, axis=1))
            ref[0, :, hs] = ((t * cos + partner * sin) * scale).astype(BF16)
        vt_ref[0, h] = z[:, 2 * d + h * LANES:2 * d + (h + 1) * LANES].T.astype(BF16)


def _proj1(x, gain, w, cos_t, sin_t, tm):
    b, s, d = x.shape
    return pl.pallas_call(
        _proj1_kernel,
        grid=(b, s // tm),
        in_specs=[
            pl.BlockSpec((1, tm, d), lambda i, j: (i, j, 0)),
            _full(gain.shape), _full(w.shape),
            pl.BlockSpec((tm, LANES), lambda i, j: (j, 0)),
            pl.BlockSpec((tm, LANES), lambda i, j: (j, 0)),
        ],
        out_specs=[
            pl.BlockSpec((1, tm, d), lambda i, j: (i, j, 0)),
            pl.BlockSpec((1, tm, d), lambda i, j: (i, j, 0)),
            pl.BlockSpec((1, DIFF_HEADS, 2 * DIFF_DIM, tm), lambda i, j: (i, 0, 0, j)),
        ],
        out_shape=[
            jax.ShapeDtypeStruct((b, s, d), BF16),
            jax.ShapeDtypeStruct((b, s, d), BF16),
            jax.ShapeDtypeStruct((b, DIFF_HEADS, 2 * DIFF_DIM, s), BF16),
        ],
        compiler_params=_params("parallel", "parallel"),
        name="proj1",
    )(x, gain, w, cos_t, sin_t)


def _hgrn_chunk(q, k, v, g, st_ref, rev):
    n = HGRN_CHUNK
    row = lax.broadcasted_iota(jnp.int32, g.shape, 0)
    blk = lax.broadcasted_iota(jnp.int32, (n, n), 0)
    blk_t = lax.broadcasted_iota(jnp.int32, (n, n), 1)
    p_sum = g
    x_sum = jnp.zeros_like(g)
    levels = []
    m = 1
    while m < n:
        in_right = (row % (2 * m)) >= m
        q_rows = jnp.logical_not(in_right) if rev else in_right
        qt = jnp.where(q_rows, q * jnp.exp(p_sum), 0.0).astype(BF16)
        kt = jnp.where(q_rows, 0.0, k * jnp.exp(x_sum)).astype(BF16)
        same_block = (blk // (2 * m)) == (blk_t // (2 * m)) if 2 * m < n else None
        levels.append((qt, kt, same_block))
        total = p_sum + x_sum
        from_left = pltpu.roll(total, m, axis=0)
        from_right = pltpu.roll(total, n - m, axis=0)
        if rev:
            p_sum = p_sum + jnp.where(in_right, 0.0, from_right)
            x_sum = x_sum + jnp.where(in_right, from_left, 0.0)
        else:
            p_sum = p_sum + jnp.where(in_right, from_left, 0.0)
            x_sum = x_sum + jnp.where(in_right, 0.0, from_right)
        m *= 2
    q_in = (q * jnp.exp(p_sum)).astype(BF16)
    k_out = (k * jnp.exp(x_sum)).astype(BF16)
    chunk_decay = jnp.exp(p_sum[0:1] + x_sum[0:1])
    qk = q * k
    vb = v.astype(BF16)
    nt = (((1,), (1,)), ((), ()))
    tn = (((0,), (0,)), ((), ()))
    outs = []
    for h in range(HGRN_HEADS):
        hs = slice(h * HGRN_DIM, (h + 1) * HGRN_DIM)
        a = None
        for qt, kt, same_block in levels:
            part = lax.dot_general(qt[:, hs], kt[:, hs], nt, preferred_element_type=F32)
            if same_block is not None:
                part = jnp.where(same_block, part, 0.0)
            a = part if a is None else a + part
        st = st_ref[h]
        o = jnp.dot(a.astype(BF16), vb[:, hs], preferred_element_type=F32)
        o = o + lax.dot_general(q_in[:, hs], st.astype(BF16), nt, preferred_element_type=F32)
        o = o + jnp.sum(qk[:, hs], axis=1, keepdims=True) * v[:, hs]
        st_ref[h] = st * chunk_decay[:, hs] + lax.dot_general(vb[:, hs], k_out[:, hs], tn,
                                                               preferred_element_type=F32)
        outs.append(o)
    return outs


def _hgrn_gates(qh, fp, lb):
    q = qh * jax.nn.sigmoid(qh) * (HGRN_DIM ** -0.5)
    g = jnp.log(lb + (1.0 - lb) * jax.nn.sigmoid(fp))
    k = (1.0 - lb) * jax.nn.sigmoid(-fp)
    return q, k, g


def _hgrn_fwd_kernel(q_ref, f_ref, v_ref, lb_ref, o_ref, st_ref):
    @pl.when(pl.program_id(1) == 0)
    def _():
        st_ref[...] = jnp.zeros_like(st_ref)

    lb = lb_ref[...]
    for c in range(HGRN_BLOCK // HGRN_CHUNK):
        rs = slice(c * HGRN_CHUNK, (c + 1) * HGRN_CHUNK)
        q, k, g = _hgrn_gates(q_ref[0, rs, :], f_ref[0, rs, :], lb)
        outs = _hgrn_chunk(q, k, v_ref[0, rs, :], g, st_ref, False)
        for h, o in enumerate(outs):
            o_ref[0, rs, h * HGRN_DIM:(h + 1) * HGRN_DIM] = o


def _hgrn_bwd_kernel(q_ref, f_ref, v_ref, gate_ref, of_ref, lb_ref, ng_ref, o_ref, st_ref):
    @pl.when(pl.program_id(1) == 0)
    def _():
        st_ref[...] = jnp.zeros_like(st_ref)

    lb = lb_ref[...]
    for c in reversed(range(HGRN_BLOCK // HGRN_CHUNK)):
        rs = slice(c * HGRN_CHUNK, (c + 1) * HGRN_CHUNK)
        q, k, g = _hgrn_gates(q_ref[0, rs, :], f_ref[0, rs, :], lb)
        outs = _hgrn_chunk(q, k, v_ref[0, rs, :], g, st_ref, True)
        gate = gate_ref[0, rs, :]
        gate = gate * jax.nn.sigmoid(gate)
        for h, o in enumerate(outs):
            hs = slice(h * HGRN_DIM, (h + 1) * HGRN_DIM)
            y = _rms(o + of_ref[0, rs, hs], ng_ref[:, hs])
            o_ref[0, rs, hs] = (y * gate[:, hs]).astype(BF16)


def _hgrn(zh, lb_fwd, lb_bwd, norm_gain):
    b, s, _ = zh.shape
    nb = s // HGRN_BLOCK
    state = pltpu.VMEM((HGRN_HEADS, HGRN_DIM, HGRN_DIM), F32)

    def col(c, rev):
        if rev:
            return pl.BlockSpec((1, HGRN_BLOCK, HGRN_W), lambda i, j: (i, nb - 1 - j, c))
        return pl.BlockSpec((1, HGRN_BLOCK, HGRN_W), lambda i, j: (i, j, c))

    o_fwd = pl.pallas_call(
        _hgrn_fwd_kernel,
        grid=(b, nb),
        in_specs=[col(0, False), col(1, False), col(3, False), _full(lb_fwd.shape)],
        out_specs=col(0, False),
        out_shape=jax.ShapeDtypeStruct((b, s, HGRN_W), F32),
        scratch_shapes=[state],
        compiler_params=_params("parallel", "arbitrary"),
        name="hgrn_fwd",
    )(zh, zh, zh, lb_fwd)
    return pl.pallas_call(
        _hgrn_bwd_kernel,
        grid=(b, nb),
        in_specs=[col(0, True), col(2, True), col(3, True), col(4, True), col(0, True),
                  _full(lb_bwd.shape), _full(norm_gain.shape)],
        out_specs=col(0, True),
        out_shape=jax.ShapeDtypeStruct((b, s, HGRN_W), BF16),
        scratch_shapes=[state],
        compiler_params=_params("parallel", "arbitrary"),
        name="hgrn_bwd",
    )(zh, zh, zh, zh, o_fwd, lb_bwd, norm_gain)


MIN_FAST_ROW_SUM = 2.0 ** -60


def _fast_sweep(k_slice, vt_slice, qst_ref, shift, l8_ref, acc_ref, p_bufs, n_kv, tk):
    w = qst_ref.shape[1]
    l8_ref[...] = jnp.zeros_like(l8_ref)
    acc_ref[...] = jnp.zeros_like(acc_ref)

    def probs(j, slot):
        s = jnp.dot(k_slice(j * tk, tk), qst_ref[...], preferred_element_type=F32)
        p = jnp.exp2(s - shift)
        l8_ref[...] += jnp.sum(p.reshape(tk // 8, 8, w), axis=0)
        p_bufs[slot][...] = p.astype(BF16)

    def weighted(j, slot):
        acc_ref[...] += jnp.dot(vt_slice(j * tk, tk), p_bufs[slot][...], preferred_element_type=F32)

    probs(0, 0)
    for j in range(1, n_kv):
        weighted(j - 1, (j - 1) % 2)
        probs(j, j % 2)
    weighted(n_kv - 1, (n_kv - 1) % 2)


def _online_sweep(k_slice, vt_slice, qst_ref, m_ref, l_ref, acc_ref, n_kv, tk):
    m_ref[...] = jnp.full(m_ref.shape, -jnp.inf, F32)
    l_ref[...] = jnp.zeros_like(l_ref)
    acc_ref[...] = jnp.zeros_like(acc_ref)

    def body(j, carry):
        off = pl.multiple_of(j * tk, tk)
        s = jnp.dot(k_slice(off, tk), qst_ref[...], preferred_element_type=F32)
        m_old = m_ref[...]
        m_new = jnp.maximum(m_old, jnp.max(s, axis=0, keepdims=True))
        alpha = jnp.exp2(m_old - m_new)
        p = jnp.exp2(s - m_new)
        l_ref[...] = alpha * l_ref[...] + jnp.sum(p, axis=0, keepdims=True)
        acc_ref[...] = alpha * acc_ref[...] + jnp.dot(vt_slice(off, tk), p.astype(BF16),
                                                      preferred_element_type=F32)
        m_ref[...] = m_new
        return carry

    lax.fori_loop(0, n_kv, body, 0)


def _softmax_sweep(k_slice, vt_slice, qst_ref, shift, m_ref, l_ref, l8_ref, acc_ref, p_bufs, n_keys, tk):
    _fast_sweep(k_slice, vt_slice, qst_ref, shift, l8_ref, acc_ref, p_bufs, n_keys // tk, tk)
    row_sum = jnp.sum(l8_ref[...], axis=0, keepdims=True)
    l_ref[...] = row_sum
    n_bad = jnp.sum(jnp.where(row_sum >= MIN_FAST_ROW_SUM, 0.0, 1.0))

    @pl.when(n_bad > 0.0)
    def _():
        tk_online = min(tk, 256)
        _online_sweep(k_slice, vt_slice, qst_ref, m_ref, l_ref, acc_ref, n_keys // tk_online, tk_online)


def _attn_scratch(tk, w, d, dv):
    return [pltpu.VMEM((d, w), BF16), pltpu.VMEM((1, w), F32), pltpu.VMEM((1, w), F32), pltpu.VMEM((8, w), F32),
            pltpu.VMEM((dv, w), F32), pltpu.VMEM((tk, w), BF16), pltpu.VMEM((tk, w), BF16),
            pltpu.VMEM((8, LANES), F32)]


def _max_row_norm(k, lane_mask):
    k2 = k.astype(F32)
    k2 = k2 * k2
    if lane_mask is not None:
        k2 = jnp.where(lane_mask, k2, 0.0)
    return jnp.sqrt(jnp.max(jnp.sum(k2, axis=1, keepdims=True), axis=0, keepdims=True))


def _mla_attn_kernel(q_ref, k_ref, vt_ref, o_ref, qst_ref, m_ref, l_ref, l8_ref, acc_ref, p0, p1, kn_ref,
                     *, tk):
    @pl.when(pl.program_id(2) == 0)
    def _():
        kn_ref[...] = jnp.broadcast_to(_max_row_norm(k_ref[0, 0], None), kn_ref.shape)

    qt = q_ref[0, 0].astype(F32).T
    qst_ref[...] = qt.astype(BF16)
    shift = jnp.sqrt(jnp.sum(qt * qt, axis=0, keepdims=True)) * kn_ref[0:1, 0:1]
    _softmax_sweep(lambda off, n: k_ref[0, 0, pl.ds(off, n), :],
                   lambda off, n: vt_ref[0, 0, :, pl.ds(off, n)],
                   qst_ref, shift, m_ref, l_ref, l8_ref, acc_ref, (p0, p1), k_ref.shape[2], tk)
    out_t = acc_ref[...] / l_ref[...]
    o_ref[0] = out_t.T.astype(BF16)


def _mla_attn(q, k, vt, tq, tk):
    b, h, s, d = q.shape
    dv = vt.shape[2]
    return pl.pallas_call(
        functools.partial(_mla_attn_kernel, tk=tk),
        grid=(b, h, s // tq),
        in_specs=[
            pl.BlockSpec((1, 1, tq, d), lambda i, j, t: (i, j, t, 0)),
            pl.BlockSpec((1, 1, s, d), lambda i, j, t: (i, j, 0, 0)),
            pl.BlockSpec((1, 1, dv, s), lambda i, j, t: (i, j, 0, 0)),
        ],
        out_specs=pl.BlockSpec((1, tq, dv), lambda i, j, t: (i, t, j)),
        out_shape=jax.ShapeDtypeStruct((b, s, h * dv), BF16),
        scratch_shapes=_attn_scratch(tk, tq, d, dv),
        compiler_params=_params("parallel", "parallel", "arbitrary"),
        name="mla_attn",
    )(q, k, vt)


def _diff_attn_kernel(q_ref, k_ref, vt_ref, lq1_ref, lk1_ref, lq2_ref, lk2_ref, ng_ref, o_ref,
                      qst_ref, m_ref, l_ref, l8_ref, acc_ref, p0, p1, kn_ref, *, tk, lambda_init):
    tq = q_ref.shape[1]

    @pl.when(pl.program_id(2) == 0)
    def _():
        k_all = k_ref[0]
        first = lax.broadcasted_iota(jnp.int32, k_all.shape, 1) < DIFF_DIM
        kn_ref[0:1, :] = jnp.broadcast_to(_max_row_norm(k_all, first), (1, LANES))
        kn_ref[1:2, :] = jnp.broadcast_to(_max_row_norm(k_all, jnp.logical_not(first)), (1, LANES))

    q = q_ref[0].astype(F32)
    first = lax.broadcasted_iota(jnp.int32, q.shape, 1) < DIFF_DIM
    q1t = jnp.where(first, q, 0.0).T
    q2t = jnp.where(first, 0.0, q).T
    qst_ref[:, 0:tq] = q1t.astype(BF16)
    qst_ref[:, tq:2 * tq] = q2t.astype(BF16)
    shift = jnp.concatenate([jnp.sqrt(jnp.sum(q1t * q1t, axis=0, keepdims=True)) * kn_ref[0:1, 0:1],
                             jnp.sqrt(jnp.sum(q2t * q2t, axis=0, keepdims=True)) * kn_ref[1:2, 0:1]], axis=1)
    _softmax_sweep(lambda off, n: k_ref[0, pl.ds(off, n), :],
                   lambda off, n: vt_ref[0, 0, :, pl.ds(off, n)],
                   qst_ref, shift, m_ref, l_ref, l8_ref, acc_ref, (p0, p1), k_ref.shape[1], tk)
    lam = (jnp.exp(jnp.sum(lq1_ref[...] * lk1_ref[...], axis=1, keepdims=True))
           - jnp.exp(jnp.sum(lq2_ref[...] * lk2_ref[...], axis=1, keepdims=True)) + lambda_init)
    soft = acc_ref[...] / l_ref[...]
    out_t = soft[:, 0:tq] - lam * soft[:, tq:2 * tq]
    y = _rms(out_t.T, ng_ref[...]) * (1.0 - lambda_init)
    o_ref[0] = y.astype(BF16)


def _diff_attn(q, k, vt, lq1, lk1, lq2, lk2, norm_gain, lambda_init, tq, tk):
    b, s, d = q.shape
    h = vt.shape[1]
    dv = vt.shape[2]
    small = _full(lq1.shape)
    return pl.pallas_call(
        functools.partial(_diff_attn_kernel, tk=tk, lambda_init=lambda_init),
        grid=(b, h, s // tq),
        in_specs=[
            pl.BlockSpec((1, tq, dv), lambda i, j, t: (i, t, j)),
            pl.BlockSpec((1, s, dv), lambda i, j, t: (i, 0, j)),
            pl.BlockSpec((1, 1, dv, s), lambda i, j, t: (i, j, 0, 0)),
            small, small, small, small, _full(norm_gain.shape),
        ],
        out_specs=pl.BlockSpec((1, tq, dv), lambda i, j, t: (i, t, j)),
        out_shape=jax.ShapeDtypeStruct((b, s, d), BF16),
        scratch_shapes=_attn_scratch(tk, 2 * tq, dv, dv),
        compiler_params=_params("parallel", "parallel", "arbitrary"),
        name="diff_attn",
    )(q, k, vt, lq1, lk1, lq2, lk2, norm_gain)


def _mix_ffn_kernel(*refs, n_mix, final, th):
    x_ref = refs[0]
    mix_refs = refs[1:1 + n_mix]
    wo_ref, g_ref, wg_ref, wu_ref, wd_ref, gf_ref, out_ref = refs[1 + n_mix:]
    mix = jnp.concatenate([m_ref[...] for m_ref in mix_refs], axis=1)
    x1 = x_ref[...] + jnp.dot(mix, wo_ref[...], preferred_element_type=F32)
    xn = _rms(x1, g_ref[...]).astype(BF16)
    y = x1
    for c in range(wg_ref.shape[1] // th):
        cs = slice(c * th, (c + 1) * th)
        gate = jnp.dot(xn, wg_ref[:, cs], preferred_element_type=F32)
        up = jnp.dot(xn, wu_ref[:, cs], preferred_element_type=F32)
        hid = (gate * jax.nn.sigmoid(gate) * up).astype(BF16)
        y = y + jnp.dot(hid, wd_ref[cs, :], preferred_element_type=F32)
    if final:
        y = _rms(y, gf_ref[...])
    out_ref[...] = y


def _resident(shape):
    n = len(shape)
    return pl.BlockSpec(shape, lambda *_: (0,) * n, pipeline_mode=pl.Buffered(1))


def _mix_ffn(x, mixes, w_out, gain, wg, wu, wd, final_gain, final, tm, th):
    t, d = x.shape
    assert wg.shape[1] % th == 0 and sum(m.shape[1] for m in mixes) == w_out.shape[0]
    n_mix = len(mixes)
    row = lambda i: (i, 0)
    in_specs = [pl.BlockSpec((tm, d), row)]
    in_specs += [pl.BlockSpec((tm, m.shape[1]), row) for m in mixes]
    in_specs += [_resident(w_out.shape), _full(gain.shape), _resident(wg.shape), _resident(wu.shape),
                 _resident(wd.shape), _full(final_gain.shape)]
    return pl.pallas_call(
        functools.partial(_mix_ffn_kernel, n_mix=n_mix, final=final, th=th),
        grid=(t // tm,),
        in_specs=in_specs,
        out_specs=pl.BlockSpec((tm, d), row),
        out_shape=jax.ShapeDtypeStruct((t, d), F32),
        compiler_params=_params("parallel"),
        name="mix_ffn",
    )(x, *mixes, w_out, gain, wg, wu, wd, final_gain)


def _rope_tables(seq_len):
    dim = MLA_ROPE
    inv = 1.0 / (ROPE_THETA ** (jnp.arange(0, dim, 2, dtype=F32) / dim))
    ang = jnp.arange(seq_len, dtype=F32)[:, None] * inv[None, :]
    cos, sin = jnp.cos(ang), jnp.sin(ang)
    cos_t = jnp.concatenate([cos, cos, cos, cos], axis=1)
    sin_t = jnp.concatenate([-sin, sin, -sin, sin], axis=1)
    return cos_t, sin_t


def _swap_halves(w):
    half = w.shape[-1] // 2
    return jnp.concatenate([w[..., half:], w[..., :half]], axis=-1)


def _pad_heads(w):
    kdim, h, r = w.shape
    return jnp.concatenate([w, jnp.zeros_like(w)], axis=-1).reshape(kdim, h * 2 * r)


def _row(v):
    return v.reshape(1, -1).astype(F32)


def kernel(x, norm_attn, norm_ffn, ffn_w_gate, ffn_w_up, ffn_w_down, ab_w_in, hgrn_lower_bound, hgrn_out_norm,
           mla_q_norm, mla_w_uq, mla_kv_norm, mla_w_ukv, ab_w_out, c_w_in, diff_lambda_q1, diff_lambda_k1,
           diff_lambda_q2, diff_lambda_k2, diff_out_norm, c_w_out, final_norm):
    b, s, d = x.shape
    depth = norm_attn.shape[0]
    assert DIFF_DIM == MLA_ROPE and d == DIFF_HEADS * 2 * DIFF_DIM and d == 2 * HGRN_W
    assert s % 512 == 0
    tm_proj = 512
    tm_ffn = 1024 if (b * s) % 1024 == 0 else 512
    th_ffn = 256
    tq_mla, tq_diff, tk = 512, 512, min(1024, s)

    cos_t, sin_t = _rope_tables(s)
    lower_bounds = jnp.cumsum(jax.nn.softmax(hgrn_lower_bound.astype(F32), axis=1), axis=1)

    for layer in range(depth):
        j = layer // 2
        gain = _row(norm_attn[layer])
        if layer % 2 == 0:
            w_in = ab_w_in[j]
            c_h = 5 * HGRN_W
            c_q = c_h + MLA_Q_LORA
            c_kv = c_q + MLA_KV_LORA
            w_kr = w_in[:, c_kv:]
            w_krs = _swap_halves(w_kr)
            wh = w_in[:, :c_h].astype(BF16)
            wm = jnp.concatenate([w_in[:, c_h:c_kv], w_kr, w_kr, w_krs, w_krs], axis=1).astype(BF16)
            wuq = mla_w_uq[j].reshape(MLA_Q_LORA, MLA_HEADS, MLA_NOPE + MLA_ROPE)
            wqn = wuq[..., :MLA_NOPE].reshape(MLA_Q_LORA, MLA_HEADS * MLA_NOPE).astype(BF16)
            wqr = _pad_heads(wuq[..., MLA_NOPE:]).astype(BF16)
            wqs = _pad_heads(_swap_halves(wuq[..., MLA_NOPE:])).astype(BF16)
            wukv = mla_w_ukv[j].reshape(MLA_KV_LORA, MLA_HEADS, MLA_NOPE + MLA_V)
            wkv = jnp.concatenate([wukv[..., :MLA_NOPE].reshape(MLA_KV_LORA, -1),
                                   wukv[..., MLA_NOPE:].reshape(MLA_KV_LORA, -1)], axis=1).astype(BF16)
            zh, q, k, vt = _proj0(x, gain, wh, wm, _row(mla_q_norm[j]), wqn, wqr, wqs, _row(mla_kv_norm[j]),
                                  wkv, cos_t, sin_t, tm_proj)
            o_a = _hgrn(zh, _row(lower_bounds[0, j]), _row(lower_bounds[1, j]), _row(hgrn_out_norm[j]))
            o_b = _mla_attn(q, k, vt, tq_mla, tk)
            w_out = ab_w_out[j].astype(BF16)
            mixes = [o_a.reshape(b * s, HGRN_W), o_b.reshape(b * s, MLA_HEADS * MLA_V)]
        else:
            lambda_init = 0.8 - 0.6 * math.exp(-0.3 * layer)
            q, k, vt = _proj1(x, gain, c_w_in[j].astype(BF16), cos_t, sin_t, tm_proj)
            o_c = _diff_attn(q, k, vt, _row(diff_lambda_q1[j]), _row(diff_lambda_k1[j]),
                             _row(diff_lambda_q2[j]), _row(diff_lambda_k2[j]), _row(diff_out_norm[j]),
                             lambda_init, tq_diff, tk)
            mixes = [o_c.reshape(b * s, d)]
            w_out = c_w_out[j].astype(BF16)
        x = _mix_ffn(x.reshape(b * s, d), mixes, w_out, _row(norm_ffn[layer]),
                     ffn_w_gate[layer].astype(BF16), ffn_w_up[layer].astype(BF16),
                     ffn_w_down[layer].astype(BF16), _row(final_norm),
                     layer == depth - 1, tm_ffn, th_ffn).reshape(b, s, d)
    return x
```

```python
import functools
import math

import jax
import jax.numpy as jnp
import numpy as np
from jax import lax
from jax.experimental import pallas as pl
from jax.experimental.pallas import tpu as pltpu

F32 = jnp.float32
BF16 = jnp.bfloat16

NORM_EPS = 1e-6
ROPE_THETA = 10000.0
LANES = 128
VMEM_LIMIT = 56 * 1024 * 1024

HGRN_HEADS = 4
HGRN_DIM = 128
HGRN_W = HGRN_HEADS * HGRN_DIM
HGRN_CHUNK = 64
HGRN_BLOCK = 256

MLA_HEADS = 4
MLA_NOPE = 128
MLA_ROPE = 64
MLA_V = 128
MLA_Q_LORA = 384
MLA_KV_LORA = 256
MLA_QK_PAD = 256
LOG2_E = math.log2(math.e)
MLA_SCALE = (MLA_NOPE + MLA_ROPE) ** -0.5 * LOG2_E

DIFF_HEADS = 8
DIFF_DIM = 64
DIFF_SCALE = DIFF_DIM ** -0.5 * LOG2_E


def _rms(x, gain):
    ms = jnp.mean(x * x, axis=-1, keepdims=True)
    return x * lax.rsqrt(ms + NORM_EPS) * gain


def _params(*sem):
    return pltpu.CompilerParams(dimension_semantics=sem, vmem_limit_bytes=VMEM_LIMIT)


def _full(shape):
    n = len(shape)
    return pl.BlockSpec(shape, lambda *_: (0,) * n)


def _resident(shape):
    n = len(shape)
    return pl.BlockSpec(shape, lambda *_: (0,) * n, pipeline_mode=pl.Buffered(1))


def _proj0_kernel(x_ref, g_ref, wh_ref, wm_ref, qg_ref, wqn_ref, wqr_ref, wqs_ref, kvg_ref, wkv_ref,
                  cos_ref, sin_ref, zh_ref, q_ref, k_ref, vt_ref):
    xn = _rms(x_ref[0], g_ref[...]).astype(BF16)
    zh_ref[0] = jnp.dot(xn, wh_ref[...], preferred_element_type=F32)
    zm = jnp.dot(xn, wm_ref[...], preferred_element_type=F32)
    cos = cos_ref[...]
    sin = sin_ref[...]
    c0 = MLA_Q_LORA
    c1 = c0 + MLA_KV_LORA
    cqn = _rms(zm[:, :c0], qg_ref[...]).astype(BF16)
    ckvn = _rms(zm[:, c0:c1], kvg_ref[...]).astype(BF16)
    k_rope = zm[:, c1:c1 + LANES] * cos + zm[:, c1 + LANES:c1 + 2 * LANES] * sin
    k_rope = jnp.where(lax.broadcasted_iota(jnp.int32, k_rope.shape, 1) < MLA_ROPE, k_rope, 0.0).astype(BF16)
    qn = jnp.dot(cqn, wqn_ref[...], preferred_element_type=F32)
    qr = jnp.dot(cqn, wqr_ref[...], preferred_element_type=F32)
    qs = jnp.dot(cqn, wqs_ref[...], preferred_element_type=F32)
    kv = jnp.dot(ckvn, wkv_ref[...], preferred_element_type=F32)
    hw = MLA_HEADS * MLA_NOPE
    for h in range(MLA_HEADS):
        hs = slice(h * LANES, (h + 1) * LANES)
        q_ref[0, h, :, 0:LANES] = (qn[:, hs] * MLA_SCALE).astype(BF16)
        q_ref[0, h, :, LANES:2 * LANES] = ((qr[:, hs] * cos + qs[:, hs] * sin) * MLA_SCALE).astype(BF16)
        k_ref[0, h, :, 0:LANES] = kv[:, hs].astype(BF16)
        k_ref[0, h, :, LANES:2 * LANES] = k_rope
        vt_ref[0, h] = kv[:, hw + h * LANES:hw + (h + 1) * LANES].T.astype(BF16)


def _proj0(x, gain, wh, wm, qg, wqn, wqr, wqs, kvg, wkv, cos_t, sin_t, tm):
    b, s, d = x.shape
    grid = (b, s // tm)
    return pl.pallas_call(
        _proj0_kernel,
        grid=grid,
        in_specs=[
            pl.BlockSpec((1, tm, d), lambda i, j: (i, j, 0)),
            _full(gain.shape), _resident(wh.shape), _resident(wm.shape), _full(qg.shape), _resident(wqn.shape),
            _resident(wqr.shape), _resident(wqs.shape), _full(kvg.shape), _resident(wkv.shape),
            pl.BlockSpec((tm, LANES), lambda i, j: (j, 0)),
            pl.BlockSpec((tm, LANES), lambda i, j: (j, 0)),
        ],
        out_specs=[
            pl.BlockSpec((1, tm, wh.shape[1]), lambda i, j: (i, j, 0)),
            pl.BlockSpec((1, MLA_HEADS, tm, MLA_QK_PAD), lambda i, j: (i, 0, j, 0)),
            pl.BlockSpec((1, MLA_HEADS, tm, MLA_QK_PAD), lambda i, j: (i, 0, j, 0)),
            pl.BlockSpec((1, MLA_HEADS, MLA_V, tm), lambda i, j: (i, 0, 0, j)),
        ],
        out_shape=[
            jax.ShapeDtypeStruct((b, s, wh.shape[1]), F32),
            jax.ShapeDtypeStruct((b, MLA_HEADS, s, MLA_QK_PAD), BF16),
            jax.ShapeDtypeStruct((b, MLA_HEADS, s, MLA_QK_PAD), BF16),
            jax.ShapeDtypeStruct((b, MLA_HEADS, MLA_V, s), BF16),
        ],
        compiler_params=_params("parallel", "parallel"),
        name="proj0",
    )(x, gain, wh, wm, qg, wqn, wqr, wqs, kvg, wkv, cos_t, sin_t)


def _proj1_kernel(x_ref, g_ref, w_ref, cos_ref, sin_ref, q_ref, k_ref, vt_ref):
    d = x_ref.shape[2]
    xn = _rms(x_ref[0], g_ref[...]).astype(BF16)
    z = jnp.dot(xn, w_ref[...], preferred_element_type=F32)
    cos = cos_ref[...]
    sin = sin_ref[...]
    lane = lax.broadcasted_iota(jnp.int32, cos.shape, 1)
    first_half = (lane % DIFF_DIM) < (DIFF_DIM // 2)
    half = DIFF_DIM // 2
    for h in range(DIFF_HEADS):
        hs = slice(h * LANES, (h + 1) * LANES)
        for base, ref, scale in ((0, q_ref, DIFF_SCALE), (d, k_ref, 1.0)):
            t = z[:, base + h * LANES:base + (h + 1) * LANES]
            partner = jnp.where(first_half, pltpu.roll(t, LANES - half, axis=1), pltpu.roll(t, half, axis=1))
            ref[0, :, hs] = ((t * cos + partner * sin) * scale).astype(BF16)
        vt_ref[0, h] = z[:, 2 * d + h * LANES:2 * d + (h + 1) * LANES].T.astype(BF16)


def _proj1(x, gain, w, cos_t, sin_t, tm):
    b, s, d = x.shape
    return pl.pallas_call(
        _proj1_kernel,
        grid=(b, s // tm),
        in_specs=[
            pl.BlockSpec((1, tm, d), lambda i, j: (i, j, 0)),
            _full(gain.shape), _resident(w.shape),
            pl.BlockSpec((tm, LANES), lambda i, j: (j, 0)),
            pl.BlockSpec((tm, LANES), lambda i, j: (j, 0)),
        ],
        out_specs=[
            pl.BlockSpec((1, tm, d), lambda i, j: (i, j, 0)),
            pl.BlockSpec((1, tm, d), lambda i, j: (i, j, 0)),
            pl.BlockSpec((1, DIFF_HEADS, 2 * DIFF_DIM, tm), lambda i, j: (i, 0, 0, j)),
        ],
        out_shape=[
            jax.ShapeDtypeStruct((b, s, d), BF16),
            jax.ShapeDtypeStruct((b, s, d), BF16),
            jax.ShapeDtypeStruct((b, DIFF_HEADS, 2 * DIFF_DIM, s), BF16),
        ],
        compiler_params=_params("parallel", "parallel"),
        name="proj1",
    )(x, gain, w, cos_t, sin_t)


def _hgrn_pair_masks(rev):
    n = HGRN_CHUNK
    t = np.arange(n)[:, None]
    s = np.arange(n)[None, :]
    masks = [t == s]
    m = 1
    while m < n:
        t_blk, s_blk = t // m, s // m
        if rev:
            masks.append((s_blk - t_blk == 1) & (t_blk % 2 == 0))
        else:
            masks.append((t_blk - s_blk == 1) & (t_blk % 2 == 1))
        m *= 2
    return jnp.asarray(np.stack(masks).astype(np.float32))


def _hgrn_chunk(q, k, v, g, pair_ref, st_ref, rev):
    n = HGRN_CHUNK
    row = lax.broadcasted_iota(jnp.int32, g.shape, 0)
    p_sum = g
    total = g
    levels = [(q.astype(BF16), k.astype(BF16))]
    m = 1
    while m < n:
        levels.append(((q * jnp.exp2(p_sum)).astype(BF16), (k * jnp.exp2(total - p_sum)).astype(BF16)))
        in_right = (row % (2 * m)) >= m
        other = jnp.where(in_right, pltpu.roll(total, m, axis=0), pltpu.roll(total, n - m, axis=0))
        p_sum = p_sum + (jnp.where(in_right, 0.0, other) if rev else jnp.where(in_right, other, 0.0))
        total = total + other
        m *= 2
    q_in = (q * jnp.exp2(p_sum)).astype(BF16)
    k_out = (k * jnp.exp2(total - p_sum)).astype(BF16)
    chunk_decay = jnp.exp2(total[0:1])
    vb = v.astype(BF16)
    nt = (((1,), (1,)), ((), ()))
    tn = (((0,), (0,)), ((), ()))
    outs = []
    for h in range(HGRN_HEADS):
        hs = slice(h * HGRN_DIM, (h + 1) * HGRN_DIM)
        a = None
        for lvl, (qt, kt) in enumerate(levels):
            part = lax.dot_general(qt[:, hs], kt[:, hs], nt, preferred_element_type=F32) * pair_ref[lvl]
            a = part if a is None else a + part
        st = st_ref[h]
        o = jnp.dot(a.astype(BF16), vb[:, hs], preferred_element_type=F32)
        o = o + lax.dot_general(q_in[:, hs], st.astype(BF16), nt, preferred_element_type=F32)
        st_ref[h] = st * chunk_decay[:, hs] + lax.dot_general(vb[:, hs], k_out[:, hs], tn,
                                                               preferred_element_type=F32)
        outs.append(o)
    return outs


def _hgrn_gates(qh, fp, lb):
    q = qh * jax.nn.sigmoid(qh) * (HGRN_DIM ** -0.5)
    g = jnp.log2(lb + (1.0 - lb) * jax.nn.sigmoid(fp))
    k = (1.0 - lb) * jax.nn.sigmoid(-fp)
    return q, k, g


def _hgrn_fwd_kernel(q_ref, f_ref, v_ref, lb_ref, pair_ref, o_ref, st_ref):
    @pl.when(pl.program_id(1) == 0)
    def _():
        st_ref[...] = jnp.zeros_like(st_ref)

    lb = lb_ref[...]
    for c in range(HGRN_BLOCK // HGRN_CHUNK):
        rs = slice(c * HGRN_CHUNK, (c + 1) * HGRN_CHUNK)
        q, k, g = _hgrn_gates(q_ref[0, rs, :], f_ref[0, rs, :], lb)
        outs = _hgrn_chunk(q, k, v_ref[0, rs, :], g, pair_ref, st_ref, False)
        for h, o in enumerate(outs):
            o_ref[0, rs, h * HGRN_DIM:(h + 1) * HGRN_DIM] = o


def _hgrn_bwd_kernel(q_ref, f_ref, v_ref, gate_ref, of_ref, lb_ref, ng_ref, pair_ref, o_ref, st_ref):
    @pl.when(pl.program_id(1) == 0)
    def _():
        st_ref[...] = jnp.zeros_like(st_ref)

    lb = lb_ref[...]
    for c in reversed(range(HGRN_BLOCK // HGRN_CHUNK)):
        rs = slice(c * HGRN_CHUNK, (c + 1) * HGRN_CHUNK)
        q, k, g = _hgrn_gates(q_ref[0, rs, :], f_ref[0, rs, :], lb)
        outs = _hgrn_chunk(q, k, v_ref[0, rs, :], g, pair_ref, st_ref, True)
        gate = gate_ref[0, rs, :]
        gate = gate * jax.nn.sigmoid(gate)
        for h, o in enumerate(outs):
            hs = slice(h * HGRN_DIM, (h + 1) * HGRN_DIM)
            y = _rms(o + of_ref[0, rs, hs], ng_ref[:, hs])
            o_ref[0, rs, hs] = (y * gate[:, hs]).astype(BF16)


def _hgrn(zh, lb_fwd, lb_bwd, norm_gain):
    b, s, _ = zh.shape
    nb = s // HGRN_BLOCK
    state = pltpu.VMEM((HGRN_HEADS, HGRN_DIM, HGRN_DIM), F32)

    def col(c, rev):
        if rev:
            return pl.BlockSpec((1, HGRN_BLOCK, HGRN_W), lambda i, j: (i, nb - 1 - j, c))
        return pl.BlockSpec((1, HGRN_BLOCK, HGRN_W), lambda i, j: (i, j, c))

    pair_fwd, pair_bwd = _hgrn_pair_masks(False), _hgrn_pair_masks(True)
    o_fwd = pl.pallas_call(
        _hgrn_fwd_kernel,
        grid=(b, nb),
        in_specs=[col(0, False), col(1, False), col(3, False), _full(lb_fwd.shape), _full(pair_fwd.shape)],
        out_specs=col(0, False),
        out_shape=jax.ShapeDtypeStruct((b, s, HGRN_W), F32),
        scratch_shapes=[state],
        compiler_params=_params("parallel", "arbitrary"),
        name="hgrn_fwd",
    )(zh, zh, zh, lb_fwd, pair_fwd)
    return pl.pallas_call(
        _hgrn_bwd_kernel,
        grid=(b, nb),
        in_specs=[col(0, True), col(2, True), col(3, True), col(4, True), col(0, True),
                  _full(lb_bwd.shape), _full(norm_gain.shape), _full(pair_bwd.shape)],
        out_specs=col(0, True),
        out_shape=jax.ShapeDtypeStruct((b, s, HGRN_W), BF16),
        scratch_shapes=[state],
        compiler_params=_params("parallel", "arbitrary"),
        name="hgrn_bwd",
    )(zh, zh, zh, zh, o_fwd, lb_bwd, norm_gain, pair_bwd)


MIN_FAST_ROW_SUM = 2.0 ** -60


def _fast_sweep(k_slice, vt_slice, qst_ref, shift, l8_ref, acc_ref, p_bufs, n_kv, tk):
    w = qst_ref.shape[1]
    l8_ref[...] = jnp.zeros_like(l8_ref)
    acc_ref[...] = jnp.zeros_like(acc_ref)

    def probs(j, slot):
        s = jnp.dot(k_slice(j * tk, tk), qst_ref[...], preferred_element_type=F32)
        p = jnp.exp2(s - shift)
        l8_ref[...] += jnp.sum(p.reshape(tk // 8, 8, w), axis=0)
        p_bufs[slot][...] = p.astype(BF16)

    def weighted(j, slot):
        acc_ref[...] += jnp.dot(vt_slice(j * tk, tk), p_bufs[slot][...], preferred_element_type=F32)

    probs(0, 0)
    for j in range(1, n_kv):
        weighted(j - 1, (j - 1) % 2)
        probs(j, j % 2)
    weighted(n_kv - 1, (n_kv - 1) % 2)


def _online_sweep(k_slice, vt_slice, qst_ref, m_ref, l_ref, acc_ref, n_kv, tk):
    m_ref[...] = jnp.full(m_ref.shape, -jnp.inf, F32)
    l_ref[...] = jnp.zeros_like(l_ref)
    acc_ref[...] = jnp.zeros_like(acc_ref)

    def body(j, carry):
        off = pl.multiple_of(j * tk, tk)
        s = jnp.dot(k_slice(off, tk), qst_ref[...], preferred_element_type=F32)
        m_old = m_ref[...]
        m_new = jnp.maximum(m_old, jnp.max(s, axis=0, keepdims=True))
        alpha = jnp.exp2(m_old - m_new)
        p = jnp.exp2(s - m_new)
        l_ref[...] = alpha * l_ref[...] + jnp.sum(p, axis=0, keepdims=True)
        acc_ref[...] = alpha * acc_ref[...] + jnp.dot(vt_slice(off, tk), p.astype(BF16),
                                                      preferred_element_type=F32)
        m_ref[...] = m_new
        return carry

    lax.fori_loop(0, n_kv, body, 0)


def _softmax_sweep(k_slice, vt_slice, qst_ref, shift, m_ref, l_ref, l8_ref, acc_ref, p_bufs, n_keys, tk):
    _fast_sweep(k_slice, vt_slice, qst_ref, shift, l8_ref, acc_ref, p_bufs, n_keys // tk, tk)
    row_sum = jnp.sum(l8_ref[...], axis=0, keepdims=True)
    l_ref[...] = row_sum
    n_bad = jnp.sum(jnp.where(row_sum >= MIN_FAST_ROW_SUM, 0.0, 1.0))

    @pl.when(n_bad > 0.0)
    def _():
        tk_online = min(tk, 256)
        _online_sweep(k_slice, vt_slice, qst_ref, m_ref, l_ref, acc_ref, n_keys // tk_online, tk_online)


def _attn_scratch(tk, w, d, dv):
    return [pltpu.VMEM((d, w), BF16), pltpu.VMEM((1, w), F32), pltpu.VMEM((1, w), F32), pltpu.VMEM((8, w), F32),
            pltpu.VMEM((dv, w), F32), pltpu.VMEM((tk, w), BF16), pltpu.VMEM((tk, w), BF16),
            pltpu.VMEM((8, LANES), F32)]


def _max_row_norm(k, lane_mask):
    k2 = k.astype(F32)
    k2 = k2 * k2
    if lane_mask is not None:
        k2 = jnp.where(lane_mask, k2, 0.0)
    return jnp.sqrt(jnp.max(jnp.sum(k2, axis=1, keepdims=True), axis=0, keepdims=True))


def _mla_attn_kernel(q_ref, k_ref, vt_ref, o_ref, qst_ref, m_ref, l_ref, l8_ref, acc_ref, p0, p1, kn_ref,
                     *, tk):
    @pl.when(pl.program_id(2) == 0)
    def _():
        kn_ref[...] = jnp.broadcast_to(_max_row_norm(k_ref[0, 0], None), kn_ref.shape)

    qt = q_ref[0, 0].astype(F32).T
    qst_ref[...] = qt.astype(BF16)
    shift = jnp.sqrt(jnp.sum(qt * qt, axis=0, keepdims=True)) * kn_ref[0:1, 0:1]
    _softmax_sweep(lambda off, n: k_ref[0, 0, pl.ds(off, n), :],
                   lambda off, n: vt_ref[0, 0, :, pl.ds(off, n)],
                   qst_ref, shift, m_ref, l_ref, l8_ref, acc_ref, (p0, p1), k_ref.shape[2], tk)
    out_t = acc_ref[...] / l_ref[...]
    o_ref[0] = out_t.T.astype(BF16)


def _mla_attn(q, k, vt, tq, tk):
    b, h, s, d = q.shape
    dv = vt.shape[2]
    return pl.pallas_call(
        functools.partial(_mla_attn_kernel, tk=tk),
        grid=(b, h, s // tq),
        in_specs=[
            pl.BlockSpec((1, 1, tq, d), lambda i, j, t: (i, j, t, 0)),
            pl.BlockSpec((1, 1, s, d), lambda i, j, t: (i, j, 0, 0)),
            pl.BlockSpec((1, 1, dv, s), lambda i, j, t: (i, j, 0, 0)),
        ],
        out_specs=pl.BlockSpec((1, tq, dv), lambda i, j, t: (i, t, j)),
        out_shape=jax.ShapeDtypeStruct((b, s, h * dv), BF16),
        scratch_shapes=_attn_scratch(tk, tq, d, dv),
        compiler_params=_params("parallel", "parallel", "arbitrary"),
        name="mla_attn",
    )(q, k, vt)


def _diff_attn_kernel(q_ref, k_ref, vt_ref, lq1_ref, lk1_ref, lq2_ref, lk2_ref, ng_ref, o_ref,
                      qst_ref, m_ref, l_ref, l8_ref, acc_ref, p0, p1, kn_ref, *, tk, lambda_init):
    tq = q_ref.shape[1]

    @pl.when(pl.program_id(2) == 0)
    def _():
        k_all = k_ref[0]
        first = lax.broadcasted_iota(jnp.int32, k_all.shape, 1) < DIFF_DIM
        kn_ref[0:1, :] = jnp.broadcast_to(_max_row_norm(k_all, first), (1, LANES))
        kn_ref[1:2, :] = jnp.broadcast_to(_max_row_norm(k_all, jnp.logical_not(first)), (1, LANES))

    q = q_ref[0].astype(F32)
    first = lax.broadcasted_iota(jnp.int32, q.shape, 1) < DIFF_DIM
    q1t = jnp.where(first, q, 0.0).T
    q2t = jnp.where(first, 0.0, q).T
    qst_ref[:, 0:tq] = q1t.astype(BF16)
    qst_ref[:, tq:2 * tq] = q2t.astype(BF16)
    shift = jnp.concatenate([jnp.sqrt(jnp.sum(q1t * q1t, axis=0, keepdims=True)) * kn_ref[0:1, 0:1],
                             jnp.sqrt(jnp.sum(q2t * q2t, axis=0, keepdims=True)) * kn_ref[1:2, 0:1]], axis=1)
    _softmax_sweep(lambda off, n: k_ref[0, pl.ds(off, n), :],
                   lambda off, n: vt_ref[0, 0, :, pl.ds(off, n)],
                   qst_ref, shift, m_ref, l_ref, l8_ref, acc_ref, (p0, p1), k_ref.shape[1], tk)
    lam = (jnp.exp(jnp.sum(lq1_ref[...] * lk1_ref[...], axis=1, keepdims=True))
           - jnp.exp(jnp.sum(lq2_ref[...] * lk2_ref[...], axis=1, keepdims=True)) + lambda_init)
    soft = acc_ref[...] / l_ref[...]
    out_t = soft[:, 0:tq] - lam * soft[:, tq:2 * tq]
    y = _rms(out_t.T, ng_ref[...]) * (1.0 - lambda_init)
    o_ref[0] = y.astype(BF16)


def _diff_attn(q, k, vt, lq1, lk1, lq2, lk2, norm_gain, lambda_init, tq, tk):
    b, s, d = q.shape
    h = vt.shape[1]
    dv = vt.shape[2]
    small = _full(lq1.shape)
    return pl.pallas_call(
        functools.partial(_diff_attn_kernel, tk=tk, lambda_init=lambda_init),
        grid=(b, h, s // tq),
        in_specs=[
            pl.BlockSpec((1, tq, dv), lambda i, j, t: (i, t, j)),
            pl.BlockSpec((1, s, dv), lambda i, j, t: (i, 0, j)),
            pl.BlockSpec((1, 1, dv, s), lambda i, j, t: (i, j, 0, 0)),
            small, small, small, small, _full(norm_gain.shape),
        ],
        out_specs=pl.BlockSpec((1, tq, dv), lambda i, j, t: (i, t, j)),
        out_shape=jax.ShapeDtypeStruct((b, s, d), BF16),
        scratch_shapes=_attn_scratch(tk, 2 * tq, dv, dv),
        compiler_params=_params("parallel", "parallel", "arbitrary"),
        name="diff_attn",
    )(q, k, vt, lq1, lk1, lq2, lk2, norm_gain)


def _mix_ffn_kernel(*refs, n_mix, final, th):
    x_ref = refs[0]
    mix_refs = refs[1:1 + n_mix]
    wo_ref, g_ref, wg_ref, wu_ref, wd_ref, gf_ref, out_ref = refs[1 + n_mix:]
    mix = jnp.concatenate([m_ref[...] for m_ref in mix_refs], axis=1)
    x1 = x_ref[...] + jnp.dot(mix, wo_ref[...], preferred_element_type=F32)
    xn = _rms(x1, g_ref[...]).astype(BF16)
    y = x1
    for c in range(wg_ref.shape[1] // th):
        cs = slice(c * th, (c + 1) * th)
        gate = jnp.dot(xn, wg_ref[:, cs], preferred_element_type=F32)
        up = jnp.dot(xn, wu_ref[:, cs], preferred_element_type=F32)
        hid = (gate * jax.nn.sigmoid(gate) * up).astype(BF16)
        y = y + jnp.dot(hid, wd_ref[cs, :], preferred_element_type=F32)
    if final:
        y = _rms(y, gf_ref[...])
    out_ref[...] = y


def _mix_ffn(x, mixes, w_out, gain, wg, wu, wd, final_gain, final, tm, th):
    t, d = x.shape
    assert wg.shape[1] % th == 0 and sum(m.shape[1] for m in mixes) == w_out.shape[0]
    n_mix = len(mixes)
    row = lambda i: (i, 0)
    in_specs = [pl.BlockSpec((tm, d), row)]
    in_specs += [pl.BlockSpec((tm, m.shape[1]), row) for m in mixes]
    in_specs += [_resident(w_out.shape), _full(gain.shape), _resident(wg.shape), _resident(wu.shape),
                 _resident(wd.shape), _full(final_gain.shape)]
    return pl.pallas_call(
        functools.partial(_mix_ffn_kernel, n_mix=n_mix, final=final, th=th),
        grid=(t // tm,),
        in_specs=in_specs,
        out_specs=pl.BlockSpec((tm, d), row),
        out_shape=jax.ShapeDtypeStruct((t, d), F32),
        compiler_params=_params("parallel"),
        name="mix_ffn",
    )(x, *mixes, w_out, gain, wg, wu, wd, final_gain)


def _rope_tables(seq_len):
    dim = MLA_ROPE
    inv = 1.0 / (ROPE_THETA ** (jnp.arange(0, dim, 2, dtype=F32) / dim))
    ang = jnp.arange(seq_len, dtype=F32)[:, None] * inv[None, :]
    cos, sin = jnp.cos(ang), jnp.sin(ang)
    cos_t = jnp.concatenate([cos, cos, cos, cos], axis=1)
    sin_t = jnp.concatenate([-sin, sin, -sin, sin], axis=1)
    return cos_t, sin_t


def _swap_halves(w):
    half = w.shape[-1] // 2
    return jnp.concatenate([w[..., half:], w[..., :half]], axis=-1)


def _pad_heads(w):
    kdim, h, r = w.shape
    return jnp.concatenate([w, jnp.zeros_like(w)], axis=-1).reshape(kdim, h * 2 * r)


def _row(v):
    return v.reshape(1, -1).astype(F32)


def kernel(x, norm_attn, norm_ffn, ffn_w_gate, ffn_w_up, ffn_w_down, ab_w_in, hgrn_lower_bound, hgrn_out_norm,
           mla_q_norm, mla_w_uq, mla_kv_norm, mla_w_ukv, ab_w_out, c_w_in, diff_lambda_q1, diff_lambda_k1,
           diff_lambda_q2, diff_lambda_k2, diff_out_norm, c_w_out, final_norm):
    b, s, d = x.shape
    depth = norm_attn.shape[0]
    assert DIFF_DIM == MLA_ROPE and d == DIFF_HEADS * 2 * DIFF_DIM and d == 2 * HGRN_W
    assert s % 512 == 0
    tm_proj = 512
    tm_ffn = 1024 if (b * s) % 1024 == 0 else 512
    th_ffn = 256
    tq_mla, tq_diff, tk = min(1024, s), 512, min(2048, s)

    cos_t, sin_t = _rope_tables(s)
    lower_bounds = jnp.cumsum(jax.nn.softmax(hgrn_lower_bound.astype(F32), axis=1), axis=1)

    for layer in range(depth):
        j = layer // 2
        gain = _row(norm_attn[layer])
        if layer % 2 == 0:
            w_in = ab_w_in[j]
            c_h = 5 * HGRN_W
            c_q = c_h + MLA_Q_LORA
            c_kv = c_q + MLA_KV_LORA
            w_kr = w_in[:, c_kv:]
            w_krs = _swap_halves(w_kr)
            wh = w_in[:, :c_h].astype(BF16)
            wm = jnp.concatenate([w_in[:, c_h:c_kv], w_kr, w_kr, w_krs, w_krs], axis=1).astype(BF16)
            wuq = mla_w_uq[j].reshape(MLA_Q_LORA, MLA_HEADS, MLA_NOPE + MLA_ROPE)
            wqn = wuq[..., :MLA_NOPE].reshape(MLA_Q_LORA, MLA_HEADS * MLA_NOPE).astype(BF16)
            wqr = _pad_heads(wuq[..., MLA_NOPE:]).astype(BF16)
            wqs = _pad_heads(_swap_halves(wuq[..., MLA_NOPE:])).astype(BF16)
            wukv = mla_w_ukv[j].reshape(MLA_KV_LORA, MLA_HEADS, MLA_NOPE + MLA_V)
            wkv = jnp.concatenate([wukv[..., :MLA_NOPE].reshape(MLA_KV_LORA, -1),
                                   wukv[..., MLA_NOPE:].reshape(MLA_KV_LORA, -1)], axis=1).astype(BF16)
            zh, q, k, vt = _proj0(x, gain, wh, wm, _row(mla_q_norm[j]), wqn, wqr, wqs, _row(mla_kv_norm[j]),
                                  wkv, cos_t, sin_t, tm_proj)
            o_a = _hgrn(zh, _row(lower_bounds[0, j]), _row(lower_bounds[1, j]), _row(hgrn_out_norm[j]))
            o_b = _mla_attn(q, k, vt, tq_mla, tk)
            w_out = ab_w_out[j].astype(BF16)
            mixes = [o_a.reshape(b * s, HGRN_W), o_b.reshape(b * s, MLA_HEADS * MLA_V)]
        else:
            lambda_init = 0.8 - 0.6 * math.exp(-0.3 * layer)
            q, k, vt = _proj1(x, gain, c_w_in[j].astype(BF16), cos_t, sin_t, tm_proj)
            o_c = _diff_attn(q, k, vt, _row(diff_lambda_q1[j]), _row(diff_lambda_k1[j]),
                             _row(diff_lambda_q2[j]), _row(diff_lambda_k2[j]), _row(diff_out_norm[j]),
                             lambda_init, tq_diff, tk)
            mixes = [o_c.reshape(b * s, d)]
            w_out = c_w_out[j].astype(BF16)
        x = _mix_ffn(x.reshape(b * s, d), mixes, w_out, _row(norm_ffn[layer]),
                     ffn_w_gate[layer].astype(BF16), ffn_w_up[layer].astype(BF16),
                     ffn_w_down[layer].astype(BF16), _row(final_norm),
                     layer == depth - 1, tm_ffn, th_ffn).reshape(b, s, d)
    return x
```

```python
import functools
import math

import jax
import jax.numpy as jnp
import numpy as np
from jax import lax
from jax.experimental import pallas as pl
from jax.experimental.pallas import tpu as pltpu

F32 = jnp.float32
BF16 = jnp.bfloat16

NORM_EPS = 1e-6
ROPE_THETA = 10000.0
LANES = 128
VMEM_LIMIT = 56 * 1024 * 1024

HGRN_HEADS = 4
HGRN_DIM = 128
HGRN_W = HGRN_HEADS * HGRN_DIM
HGRN_CHUNK = 64
HGRN_BLOCK = 256

MLA_HEADS = 4
MLA_NOPE = 128
MLA_ROPE = 64
MLA_V = 128
MLA_Q_LORA = 384
MLA_KV_LORA = 256
MLA_QK_PAD = 256
LOG2_E = math.log2(math.e)
MLA_SCALE = (MLA_NOPE + MLA_ROPE) ** -0.5 * LOG2_E

DIFF_HEADS = 8
DIFF_DIM = 64
DIFF_SCALE = DIFF_DIM ** -0.5 * LOG2_E


def _rms(x, gain):
    ms = jnp.mean(x * x, axis=-1, keepdims=True)
    return x * lax.rsqrt(ms + NORM_EPS) * gain


def _params(*sem):
    return pltpu.CompilerParams(dimension_semantics=sem, vmem_limit_bytes=VMEM_LIMIT)


def _full(shape):
    n = len(shape)
    return pl.BlockSpec(shape, lambda *_: (0,) * n)


def _resident(shape):
    n = len(shape)
    return pl.BlockSpec(shape, lambda *_: (0,) * n, pipeline_mode=pl.Buffered(1))


def _proj0_kernel(x_ref, g_ref, wh_ref, wm_ref, qg_ref, wqn_ref, wqr_ref, wqs_ref, kvg_ref, wkv_ref,
                  cos_ref, sin_ref, zh_ref, q_ref, k_ref, vt_ref):
    xn = _rms(x_ref[0], g_ref[...]).astype(BF16)
    zh_ref[0] = jnp.dot(xn, wh_ref[...], preferred_element_type=F32)
    zm = jnp.dot(xn, wm_ref[...], preferred_element_type=F32)
    cos = cos_ref[...]
    sin = sin_ref[...]
    c0 = MLA_Q_LORA
    c1 = c0 + MLA_KV_LORA
    cqn = _rms(zm[:, :c0], qg_ref[...]).astype(BF16)
    ckvn = _rms(zm[:, c0:c1], kvg_ref[...]).astype(BF16)
    k_rope = zm[:, c1:c1 + LANES] * cos + zm[:, c1 + LANES:c1 + 2 * LANES] * sin
    k_rope = jnp.where(lax.broadcasted_iota(jnp.int32, k_rope.shape, 1) < MLA_ROPE, k_rope, 0.0).astype(BF16)
    qn = jnp.dot(cqn, wqn_ref[...], preferred_element_type=F32)
    qr = jnp.dot(cqn, wqr_ref[...], preferred_element_type=F32)
    qs = jnp.dot(cqn, wqs_ref[...], preferred_element_type=F32)
    kv = jnp.dot(ckvn, wkv_ref[...], preferred_element_type=F32)
    hw = MLA_HEADS * MLA_NOPE
    for h in range(MLA_HEADS):
        hs = slice(h * LANES, (h + 1) * LANES)
        q_ref[0, h, :, 0:LANES] = (qn[:, hs] * MLA_SCALE).astype(BF16)
        q_ref[0, h, :, LANES:2 * LANES] = ((qr[:, hs] * cos + qs[:, hs] * sin) * MLA_SCALE).astype(BF16)
        k_ref[0, h, :, 0:LANES] = kv[:, hs].astype(BF16)
        k_ref[0, h, :, LANES:2 * LANES] = k_rope
        vt_ref[0, h] = kv[:, hw + h * LANES:hw + (h + 1) * LANES].T.astype(BF16)


def _proj0(x, gain, wh, wm, qg, wqn, wqr, wqs, kvg, wkv, cos_t, sin_t, tm):
    b, s, d = x.shape
    grid = (b, s // tm)
    return pl.pallas_call(
        _proj0_kernel,
        grid=grid,
        in_specs=[
            pl.BlockSpec((1, tm, d), lambda i, j: (i, j, 0)),
            _full(gain.shape), _resident(wh.shape), _resident(wm.shape), _full(qg.shape), _resident(wqn.shape),
            _resident(wqr.shape), _resident(wqs.shape), _full(kvg.shape), _resident(wkv.shape),
            pl.BlockSpec((tm, LANES), lambda i, j: (j, 0)),
            pl.BlockSpec((tm, LANES), lambda i, j: (j, 0)),
        ],
        out_specs=[
            pl.BlockSpec((1, tm, wh.shape[1]), lambda i, j: (i, j, 0)),
            pl.BlockSpec((1, MLA_HEADS, tm, MLA_QK_PAD), lambda i, j: (i, 0, j, 0)),
            pl.BlockSpec((1, MLA_HEADS, tm, MLA_QK_PAD), lambda i, j: (i, 0, j, 0)),
            pl.BlockSpec((1, MLA_HEADS, MLA_V, tm), lambda i, j: (i, 0, 0, j)),
        ],
        out_shape=[
            jax.ShapeDtypeStruct((b, s, wh.shape[1]), F32),
            jax.ShapeDtypeStruct((b, MLA_HEADS, s, MLA_QK_PAD), BF16),
            jax.ShapeDtypeStruct((b, MLA_HEADS, s, MLA_QK_PAD), BF16),
            jax.ShapeDtypeStruct((b, MLA_HEADS, MLA_V, s), BF16),
        ],
        compiler_params=_params("parallel", "parallel"),
        name="proj0",
    )(x, gain, wh, wm, qg, wqn, wqr, wqs, kvg, wkv, cos_t, sin_t)


def _proj1_kernel(x_ref, g_ref, w_ref, cos_ref, sin_ref, q_ref, k_ref, vt_ref):
    d = x_ref.shape[2]
    xn = _rms(x_ref[0], g_ref[...]).astype(BF16)
    z = jnp.dot(xn, w_ref[...], preferred_element_type=F32)
    cos = cos_ref[...]
    sin = sin_ref[...]
    lane = lax.broadcasted_iota(jnp.int32, cos.shape, 1)
    first_half = (lane % DIFF_DIM) < (DIFF_DIM // 2)
    half = DIFF_DIM // 2
    for h in range(DIFF_HEADS):
        hs = slice(h * LANES, (h + 1) * LANES)
        for base, ref, scale in ((0, q_ref, DIFF_SCALE), (d, k_ref, 1.0)):
            t = z[:, base + h * LANES:base + (h + 1) * LANES]
            partner = jnp.where(first_half, pltpu.roll(t, LANES - half, axis=1), pltpu.roll(t, half, axis=1))
            ref[0, :, hs] = ((t * cos + partner * sin) * scale).astype(BF16)
        vt_ref[0, h] = z[:, 2 * d + h * LANES:2 * d + (h + 1) * LANES].T.astype(BF16)


def _proj1(x, gain, w, cos_t, sin_t, tm):
    b, s, d = x.shape
    return pl.pallas_call(
        _proj1_kernel,
        grid=(b, s // tm),
        in_specs=[
            pl.BlockSpec((1, tm, d), lambda i, j: (i, j, 0)),
            _full(gain.shape), _resident(w.shape),
            pl.BlockSpec((tm, LANES), lambda i, j: (j, 0)),
            pl.BlockSpec((tm, LANES), lambda i, j: (j, 0)),
        ],
        out_specs=[
            pl.BlockSpec((1, tm, d), lambda i, j: (i, j, 0)),
            pl.BlockSpec((1, tm, d), lambda i, j: (i, j, 0)),
            pl.BlockSpec((1, DIFF_HEADS, 2 * DIFF_DIM, tm), lambda i, j: (i, 0, 0, j)),
        ],
        out_shape=[
            jax.ShapeDtypeStruct((b, s, d), BF16),
            jax.ShapeDtypeStruct((b, s, d), BF16),
            jax.ShapeDtypeStruct((b, DIFF_HEADS, 2 * DIFF_DIM, s), BF16),
        ],
        compiler_params=_params("parallel", "parallel"),
        name="proj1",
    )(x, gain, w, cos_t, sin_t)


def _hgrn_pair_masks(rev):
    n = HGRN_CHUNK
    t = np.arange(n)[:, None]
    s = np.arange(n)[None, :]
    masks = [t == s]
    m = 1
    while m < n:
        t_blk, s_blk = t // m, s // m
        if rev:
            masks.append((s_blk - t_blk == 1) & (t_blk % 2 == 0))
        else:
            masks.append((t_blk - s_blk == 1) & (t_blk % 2 == 1))
        m *= 2
    return jnp.asarray(np.stack(masks).astype(np.float32))


def _hgrn_chunk(q, k, v, g, pair_ref, st_ref, rev):
    n = HGRN_CHUNK
    row = lax.broadcasted_iota(jnp.int32, g.shape, 0)
    p_sum = g
    total = g
    levels = [(q.astype(BF16), k.astype(BF16))]
    m = 1
    while m < n:
        levels.append(((q * jnp.exp2(p_sum)).astype(BF16), (k * jnp.exp2(total - p_sum)).astype(BF16)))
        in_right = (row % (2 * m)) >= m
        other = jnp.where(in_right, pltpu.roll(total, m, axis=0), pltpu.roll(total, n - m, axis=0))
        p_sum = p_sum + (jnp.where(in_right, 0.0, other) if rev else jnp.where(in_right, other, 0.0))
        total = total + other
        m *= 2
    q_in = (q * jnp.exp2(p_sum)).astype(BF16)
    k_out = (k * jnp.exp2(total - p_sum)).astype(BF16)
    chunk_decay = jnp.exp2(total[0:1])
    vb = v.astype(BF16)
    nt = (((1,), (1,)), ((), ()))
    tn = (((0,), (0,)), ((), ()))
    outs = []
    for h in range(HGRN_HEADS):
        hs = slice(h * HGRN_DIM, (h + 1) * HGRN_DIM)
        a = None
        for lvl, (qt, kt) in enumerate(levels):
            part = lax.dot_general(qt[:, hs], kt[:, hs], nt, preferred_element_type=F32) * pair_ref[lvl]
            a = part if a is None else a + part
        st = st_ref[h]
        o = jnp.dot(a.astype(BF16), vb[:, hs], preferred_element_type=F32)
        o = o + lax.dot_general(q_in[:, hs], st.astype(BF16), nt, preferred_element_type=F32)
        st_ref[h] = st * chunk_decay[:, hs] + lax.dot_general(vb[:, hs], k_out[:, hs], tn,
                                                               preferred_element_type=F32)
        outs.append(o)
    return outs


def _hgrn_gates(qh, fp, lb):
    q = qh * jax.nn.sigmoid(qh) * (HGRN_DIM ** -0.5)
    g = jnp.log2(lb + (1.0 - lb) * jax.nn.sigmoid(fp))
    k = (1.0 - lb) * jax.nn.sigmoid(-fp)
    return q, k, g


def _hgrn_fwd_kernel(q_ref, f_ref, v_ref, lb_ref, pair_ref, o_ref, st_ref):
    @pl.when(pl.program_id(1) == 0)
    def _():
        st_ref[...] = jnp.zeros_like(st_ref)

    lb = lb_ref[...]
    for c in range(HGRN_BLOCK // HGRN_CHUNK):
        rs = slice(c * HGRN_CHUNK, (c + 1) * HGRN_CHUNK)
        for b in range(q_ref.shape[0]):
            q, k, g = _hgrn_gates(q_ref[b, rs, :], f_ref[b, rs, :], lb)
            outs = _hgrn_chunk(q, k, v_ref[b, rs, :], g, pair_ref, st_ref.at[b], False)
            for h, o in enumerate(outs):
                o_ref[b, rs, h * HGRN_DIM:(h + 1) * HGRN_DIM] = o


def _hgrn_bwd_kernel(q_ref, f_ref, v_ref, gate_ref, of_ref, lb_ref, ng_ref, pair_ref, o_ref, st_ref):
    @pl.when(pl.program_id(1) == 0)
    def _():
        st_ref[...] = jnp.zeros_like(st_ref)

    lb = lb_ref[...]
    for c in reversed(range(HGRN_BLOCK // HGRN_CHUNK)):
        rs = slice(c * HGRN_CHUNK, (c + 1) * HGRN_CHUNK)
        for b in range(q_ref.shape[0]):
            q, k, g = _hgrn_gates(q_ref[b, rs, :], f_ref[b, rs, :], lb)
            outs = _hgrn_chunk(q, k, v_ref[b, rs, :], g, pair_ref, st_ref.at[b], True)
            gate = gate_ref[b, rs, :]
            gate = gate * jax.nn.sigmoid(gate)
            for h, o in enumerate(outs):
                hs = slice(h * HGRN_DIM, (h + 1) * HGRN_DIM)
                y = _rms(o + of_ref[b, rs, hs], ng_ref[:, hs])
                o_ref[b, rs, hs] = (y * gate[:, hs]).astype(BF16)


def _hgrn(zh, lb_fwd, lb_bwd, norm_gain):
    b, s, _ = zh.shape
    nb = s // HGRN_BLOCK
    group = 2 if b % 2 == 0 else 1
    state = pltpu.VMEM((group, HGRN_HEADS, HGRN_DIM, HGRN_DIM), F32)

    def col(c, rev):
        if rev:
            return pl.BlockSpec((group, HGRN_BLOCK, HGRN_W), lambda i, j: (i, nb - 1 - j, c))
        return pl.BlockSpec((group, HGRN_BLOCK, HGRN_W), lambda i, j: (i, j, c))

    pair_fwd, pair_bwd = _hgrn_pair_masks(False), _hgrn_pair_masks(True)
    o_fwd = pl.pallas_call(
        _hgrn_fwd_kernel,
        grid=(b // group, nb),
        in_specs=[col(0, False), col(1, False), col(3, False), _full(lb_fwd.shape), _full(pair_fwd.shape)],
        out_specs=col(0, False),
        out_shape=jax.ShapeDtypeStruct((b, s, HGRN_W), F32),
        scratch_shapes=[state],
        compiler_params=_params("parallel", "arbitrary"),
        name="hgrn_fwd",
    )(zh, zh, zh, lb_fwd, pair_fwd)
    return pl.pallas_call(
        _hgrn_bwd_kernel,
        grid=(b // group, nb),
        in_specs=[col(0, True), col(2, True), col(3, True), col(4, True), col(0, True),
                  _full(lb_bwd.shape), _full(norm_gain.shape), _full(pair_bwd.shape)],
        out_specs=col(0, True),
        out_shape=jax.ShapeDtypeStruct((b, s, HGRN_W), BF16),
        scratch_shapes=[state],
        compiler_params=_params("parallel", "arbitrary"),
        name="hgrn_bwd",
    )(zh, zh, zh, zh, o_fwd, lb_bwd, norm_gain, pair_bwd)


MIN_FAST_ROW_SUM = 2.0 ** -60


def _fast_sweep(k_slice, vt_slice, qst_ref, shift, l8_ref, acc_ref, p_bufs, n_kv, tk):
    w = qst_ref.shape[1]
    l8_ref[...] = jnp.zeros_like(l8_ref)
    acc_ref[...] = jnp.zeros_like(acc_ref)

    def probs(j, slot):
        s = jnp.dot(k_slice(j * tk, tk), qst_ref[...], preferred_element_type=F32)
        p = jnp.exp2(s - shift)
        l8_ref[...] += jnp.sum(p.reshape(tk // 8, 8, w), axis=0)
        p_bufs[slot][...] = p.astype(BF16)

    def weighted(j, slot):
        acc_ref[...] += jnp.dot(vt_slice(j * tk, tk), p_bufs[slot][...], preferred_element_type=F32)

    probs(0, 0)
    for j in range(1, n_kv):
        weighted(j - 1, (j - 1) % 2)
        probs(j, j % 2)
    weighted(n_kv - 1, (n_kv - 1) % 2)


def _online_sweep(k_slice, vt_slice, qst_ref, m_ref, l_ref, acc_ref, n_kv, tk):
    m_ref[...] = jnp.full(m_ref.shape, -jnp.inf, F32)
    l_ref[...] = jnp.zeros_like(l_ref)
    acc_ref[...] = jnp.zeros_like(acc_ref)

    def body(j, carry):
        off = pl.multiple_of(j * tk, tk)
        s = jnp.dot(k_slice(off, tk), qst_ref[...], preferred_element_type=F32)
        m_old = m_ref[...]
        m_new = jnp.maximum(m_old, jnp.max(s, axis=0, keepdims=True))
        alpha = jnp.exp2(m_old - m_new)
        p = jnp.exp2(s - m_new)
        l_ref[...] = alpha * l_ref[...] + jnp.sum(p, axis=0, keepdims=True)
        acc_ref[...] = alpha * acc_ref[...] + jnp.dot(vt_slice(off, tk), p.astype(BF16),
                                                      preferred_element_type=F32)
        m_ref[...] = m_new
        return carry

    lax.fori_loop(0, n_kv, body, 0)


def _finish_sweep(k_slice, vt_slice, qst_ref, m_ref, l_ref, l8_ref, acc_ref, n_keys, tk):
    row_sum = jnp.sum(l8_ref[...], axis=0, keepdims=True)
    l_ref[...] = row_sum
    n_bad = jnp.sum(jnp.where(row_sum >= MIN_FAST_ROW_SUM, 0.0, 1.0))

    @pl.when(n_bad > 0.0)
    def _():
        tk_online = min(tk, 256)
        _online_sweep(k_slice, vt_slice, qst_ref, m_ref, l_ref, acc_ref, n_keys // tk_online, tk_online)


def _attn_scratch(tk, w, d, dv):
    return [pltpu.VMEM((d, w), BF16), pltpu.VMEM((1, w), F32), pltpu.VMEM((1, w), F32), pltpu.VMEM((8, w), F32),
            pltpu.VMEM((dv, w), F32), pltpu.VMEM((tk, w), BF16), pltpu.VMEM((tk, w), BF16),
            pltpu.VMEM((8, LANES), F32)]


def _max_row_norm(k, lane_mask):
    k2 = k.astype(F32)
    k2 = k2 * k2
    if lane_mask is not None:
        k2 = jnp.where(lane_mask, k2, 0.0)
    return jnp.sqrt(jnp.max(jnp.sum(k2, axis=1, keepdims=True), axis=0, keepdims=True))


def _mla_attn_kernel(q_ref, k_ref, vt_ref, o_ref, qst_ref, m_ref, l_ref, l8_ref, acc_ref, p0, p1, kn_ref,
                     *, tk):
    @pl.when(pl.program_id(2) == 0)
    def _():
        kn_ref[...] = jnp.broadcast_to(_max_row_norm(k_ref[0, 0], None), kn_ref.shape)

    qt = q_ref[0, 0].astype(F32).T
    qst_ref[...] = qt.astype(BF16)
    shift = jnp.sqrt(jnp.sum(qt * qt, axis=0, keepdims=True)) * kn_ref[0:1, 0:1]
    k_slice = lambda off, n: k_ref[0, 0, pl.ds(off, n), :]
    vt_slice = lambda off, n: vt_ref[0, 0, :, pl.ds(off, n)]
    n_keys = k_ref.shape[2]
    _fast_sweep(k_slice, vt_slice, qst_ref, shift, l8_ref, acc_ref, (p0, p1), n_keys // tk, tk)
    _finish_sweep(k_slice, vt_slice, qst_ref, m_ref, l_ref, l8_ref, acc_ref, n_keys, tk)
    out_t = acc_ref[...] / l_ref[...]
    o_ref[0] = out_t.T.astype(BF16)


def _mla_attn(q, k, vt, tq, tk):
    b, h, s, d = q.shape
    dv = vt.shape[2]
    return pl.pallas_call(
        functools.partial(_mla_attn_kernel, tk=tk),
        grid=(b, h, s // tq),
        in_specs=[
            pl.BlockSpec((1, 1, tq, d), lambda i, j, t: (i, j, t, 0)),
            pl.BlockSpec((1, 1, s, d), lambda i, j, t: (i, j, 0, 0)),
            pl.BlockSpec((1, 1, dv, s), lambda i, j, t: (i, j, 0, 0)),
        ],
        out_specs=pl.BlockSpec((1, tq, dv), lambda i, j, t: (i, t, j)),
        out_shape=jax.ShapeDtypeStruct((b, s, h * dv), BF16),
        scratch_shapes=_attn_scratch(tk, tq, d, dv),
        compiler_params=_params("parallel", "parallel", "arbitrary"),
        name="mla_attn",
    )(q, k, vt)


def _diff_attn_kernel(q_ref, k_ref, vt_ref, lq1_ref, lk1_ref, lq2_ref, lk2_ref, ng_ref, o_ref,
                      qst_ref, m_ref, l_ref, l8_ref, acc_ref, p0, p1, kn_ref, *, tk, lambda_init):
    tq = q_ref.shape[1]

    @pl.when(pl.program_id(2) == 0)
    def _():
        k_all = k_ref[0]
        first = lax.broadcasted_iota(jnp.int32, k_all.shape, 1) < DIFF_DIM
        kn_ref[0:1, :] = jnp.broadcast_to(_max_row_norm(k_all, first), (1, LANES))
        kn_ref[1:2, :] = jnp.broadcast_to(_max_row_norm(k_all, jnp.logical_not(first)), (1, LANES))

    q = q_ref[0].astype(F32)
    first = lax.broadcasted_iota(jnp.int32, q.shape, 1) < DIFF_DIM
    q1t = jnp.where(first, q, 0.0).T
    q2t = jnp.where(first, 0.0, q).T
    qst_ref[:, 0:tq] = q1t.astype(BF16)
    qst_ref[:, tq:2 * tq] = q2t.astype(BF16)
    shift = jnp.concatenate([jnp.sqrt(jnp.sum(q1t * q1t, axis=0, keepdims=True)) * kn_ref[0:1, 0:1],
                             jnp.sqrt(jnp.sum(q2t * q2t, axis=0, keepdims=True)) * kn_ref[1:2, 0:1]], axis=1)
    k_slice = lambda off, n: k_ref[0, pl.ds(off, n), :]
    vt_slice = lambda off, n: vt_ref[0, 0, :, pl.ds(off, n)]
    n_keys = k_ref.shape[1]
    _fast_sweep(k_slice, vt_slice, qst_ref, shift, l8_ref, acc_ref, (p0, p1), n_keys // tk, tk)
    _finish_sweep(k_slice, vt_slice, qst_ref, m_ref, l_ref, l8_ref, acc_ref, n_keys, tk)
    lam = (jnp.exp(jnp.sum(lq1_ref[...] * lk1_ref[...], axis=1, keepdims=True))
           - jnp.exp(jnp.sum(lq2_ref[...] * lk2_ref[...], axis=1, keepdims=True)) + lambda_init)
    soft = acc_ref[...] / l_ref[...]
    out_t = soft[:, 0:tq] - lam * soft[:, tq:2 * tq]
    y = _rms(out_t.T, ng_ref[...]) * (1.0 - lambda_init)
    o_ref[0] = y.astype(BF16)


def _diff_attn(q, k, vt, lq1, lk1, lq2, lk2, norm_gain, lambda_init, tq, tk):
    b, s, d = q.shape
    h = vt.shape[1]
    dv = vt.shape[2]
    small = _full(lq1.shape)
    return pl.pallas_call(
        functools.partial(_diff_attn_kernel, tk=tk, lambda_init=lambda_init),
        grid=(b, h, s // tq),
        in_specs=[
            pl.BlockSpec((1, tq, dv), lambda i, j, t: (i, t, j)),
            pl.BlockSpec((1, s, dv), lambda i, j, t: (i, 0, j)),
            pl.BlockSpec((1, 1, dv, s), lambda i, j, t: (i, j, 0, 0)),
            small, small, small, small, _full(norm_gain.shape),
        ],
        out_specs=pl.BlockSpec((1, tq, dv), lambda i, j, t: (i, t, j)),
        out_shape=jax.ShapeDtypeStruct((b, s, d), BF16),
        scratch_shapes=_attn_scratch(tk, 2 * tq, dv, dv),
        compiler_params=_params("parallel", "parallel", "arbitrary"),
        name="diff_attn",
    )(q, k, vt, lq1, lk1, lq2, lk2, norm_gain)


def _mix_ffn_kernel(*refs, n_mix, final, th):
    x_ref = refs[0]
    mix_refs = refs[1:1 + n_mix]
    wo_ref, g_ref, wg_ref, wu_ref, wd_ref, gf_ref, out_ref = refs[1 + n_mix:]
    mix = jnp.concatenate([m_ref[...] for m_ref in mix_refs], axis=1)
    x1 = x_ref[...] + jnp.dot(mix, wo_ref[...], preferred_element_type=F32)
    xn = _rms(x1, g_ref[...]).astype(BF16)
    y = x1
    for c in range(wg_ref.shape[1] // th):
        cs = slice(c * th, (c + 1) * th)
        gate = jnp.dot(xn, wg_ref[:, cs], preferred_element_type=F32)
        up = jnp.dot(xn, wu_ref[:, cs], preferred_element_type=F32)
        hid = (gate * jax.nn.sigmoid(gate) * up).astype(BF16)
        y = y + jnp.dot(hid, wd_ref[cs, :], preferred_element_type=F32)
    if final:
        y = _rms(y, gf_ref[...])
    out_ref[...] = y


def _mix_ffn(x, mixes, w_out, gain, wg, wu, wd, final_gain, final, tm, th):
    t, d = x.shape
    assert wg.shape[1] % th == 0 and sum(m.shape[1] for m in mixes) == w_out.shape[0]
    n_mix = len(mixes)
    row = lambda i: (i, 0)
    in_specs = [pl.BlockSpec((tm, d), row)]
    in_specs += [pl.BlockSpec((tm, m.shape[1]), row) for m in mixes]
    in_specs += [_resident(w_out.shape), _full(gain.shape), _resident(wg.shape), _resident(wu.shape),
                 _resident(wd.shape), _full(final_gain.shape)]
    return pl.pallas_call(
        functools.partial(_mix_ffn_kernel, n_mix=n_mix, final=final, th=th),
        grid=(t // tm,),
        in_specs=in_specs,
        out_specs=pl.BlockSpec((tm, d), row),
        out_shape=jax.ShapeDtypeStruct((t, d), F32),
        compiler_params=_params("parallel"),
        name="mix_ffn",
    )(x, *mixes, w_out, gain, wg, wu, wd, final_gain)


def _rope_tables(seq_len):
    dim = MLA_ROPE
    inv = 1.0 / (ROPE_THETA ** (jnp.arange(0, dim, 2, dtype=F32) / dim))
    ang = jnp.arange(seq_len, dtype=F32)[:, None] * inv[None, :]
    cos, sin = jnp.cos(ang), jnp.sin(ang)
    cos_t = jnp.concatenate([cos, cos, cos, cos], axis=1)
    sin_t = jnp.concatenate([-sin, sin, -sin, sin], axis=1)
    return cos_t, sin_t


def _swap_halves(w):
    half = w.shape[-1] // 2
    return jnp.concatenate([w[..., half:], w[..., :half]], axis=-1)


def _pad_heads(w):
    kdim, h, r = w.shape
    return jnp.concatenate([w, jnp.zeros_like(w)], axis=-1).reshape(kdim, h * 2 * r)


def _row(v):
    return v.reshape(1, -1).astype(F32)


def kernel(x, norm_attn, norm_ffn, ffn_w_gate, ffn_w_up, ffn_w_down, ab_w_in, hgrn_lower_bound, hgrn_out_norm,
           mla_q_norm, mla_w_uq, mla_kv_norm, mla_w_ukv, ab_w_out, c_w_in, diff_lambda_q1, diff_lambda_k1,
           diff_lambda_q2, diff_lambda_k2, diff_out_norm, c_w_out, final_norm):
    b, s, d = x.shape
    depth = norm_attn.shape[0]
    assert DIFF_DIM == MLA_ROPE and d == DIFF_HEADS * 2 * DIFF_DIM and d == 2 * HGRN_W
    assert s % 512 == 0
    tm_proj = 512
    tm_ffn = 1024 if (b * s) % 1024 == 0 else 512
    th_ffn = 256
    tq_mla, tq_diff, tk = min(1024, s), 512, min(4096, s)

    cos_t, sin_t = _rope_tables(s)
    lower_bounds = jnp.cumsum(jax.nn.softmax(hgrn_lower_bound.astype(F32), axis=1), axis=1)

    for layer in range(depth):
        j = layer // 2
        gain = _row(norm_attn[layer])
        if layer % 2 == 0:
            w_in = ab_w_in[j]
            c_h = 5 * HGRN_W
            c_q = c_h + MLA_Q_LORA
            c_kv = c_q + MLA_KV_LORA
            w_kr = w_in[:, c_kv:]
            w_krs = _swap_halves(w_kr)
            wh = w_in[:, :c_h].astype(BF16)
            wm = jnp.concatenate([w_in[:, c_h:c_kv], w_kr, w_kr, w_krs, w_krs], axis=1).astype(BF16)
            wuq = mla_w_uq[j].reshape(MLA_Q_LORA, MLA_HEADS, MLA_NOPE + MLA_ROPE)
            wqn = wuq[..., :MLA_NOPE].reshape(MLA_Q_LORA, MLA_HEADS * MLA_NOPE).astype(BF16)
            wqr = _pad_heads(wuq[..., MLA_NOPE:]).astype(BF16)
            wqs = _pad_heads(_swap_halves(wuq[..., MLA_NOPE:])).astype(BF16)
            wukv = mla_w_ukv[j].reshape(MLA_KV_LORA, MLA_HEADS, MLA_NOPE + MLA_V)
            wkv = jnp.concatenate([wukv[..., :MLA_NOPE].reshape(MLA_KV_LORA, -1),
                                   wukv[..., MLA_NOPE:].reshape(MLA_KV_LORA, -1)], axis=1).astype(BF16)
            zh, q, k, vt = _proj0(x, gain, wh, wm, _row(mla_q_norm[j]), wqn, wqr, wqs, _row(mla_kv_norm[j]),
                                  wkv, cos_t, sin_t, tm_proj)
            o_a = _hgrn(zh, _row(lower_bounds[0, j]), _row(lower_bounds[1, j]), _row(hgrn_out_norm[j]))
            o_b = _mla_attn(q, k, vt, tq_mla, tk)
            w_out = ab_w_out[j].astype(BF16)
            mixes = [o_a.reshape(b * s, HGRN_W), o_b.reshape(b * s, MLA_HEADS * MLA_V)]
        else:
            lambda_init = 0.8 - 0.6 * math.exp(-0.3 * layer)
            q, k, vt = _proj1(x, gain, c_w_in[j].astype(BF16), cos_t, sin_t, tm_proj)
            o_c = _diff_attn(q, k, vt, _row(diff_lambda_q1[j]), _row(diff_lambda_k1[j]),
                             _row(diff_lambda_q2[j]), _row(diff_lambda_k2[j]), _row(diff_out_norm[j]),
                             lambda_init, tq_diff, tk)
            mixes = [o_c.reshape(b * s, d)]
            w_out = c_w_out[j].astype(BF16)
        x = _mix_ffn(x.reshape(b * s, d), mixes, w_out, _row(norm_ffn[layer]),
                     ffn_w_gate[layer].astype(BF16), ffn_w_up[layer].astype(BF16),
                     ffn_w_down[layer].astype(BF16), _row(final_norm),
                     layer == depth - 1, tm_ffn, th_ffn).reshape(b, s, d)
    return x
```

```python
import functools
import math

import jax
import jax.numpy as jnp
import numpy as np
from jax import lax
from jax.experimental import pallas as pl
from jax.experimental.pallas import tpu as pltpu

F32 = jnp.float32
BF16 = jnp.bfloat16

NORM_EPS = 1e-6
ROPE_THETA = 10000.0
LANES = 128
V7X_VMEM_BYTES = 64 * 1024 * 1024
VMEM_LIMIT = V7X_VMEM_BYTES * 7 // 8

HGRN_HEADS = 4
HGRN_DIM = 128
HGRN_W = HGRN_HEADS * HGRN_DIM
HGRN_CHUNK = 64
HGRN_BLOCK = 256

MLA_HEADS = 4
MLA_NOPE = 128
MLA_ROPE = 64
MLA_V = 128
MLA_Q_LORA = 384
MLA_KV_LORA = 256
MLA_QK_PAD = 256
LOG2_E = math.log2(math.e)
MLA_SCALE = (MLA_NOPE + MLA_ROPE) ** -0.5 * LOG2_E

DIFF_HEADS = 8
DIFF_DIM = 64
DIFF_SCALE = DIFF_DIM ** -0.5 * LOG2_E


def _rms(x, gain):
    ms = jnp.mean(x * x, axis=-1, keepdims=True)
    return x * lax.rsqrt(ms + NORM_EPS) * gain


def _params(*sem):
    return pltpu.CompilerParams(dimension_semantics=sem, vmem_limit_bytes=VMEM_LIMIT)


def _full(shape):
    n = len(shape)
    return pl.BlockSpec(shape, lambda *_: (0,) * n)


def _resident(shape):
    n = len(shape)
    return pl.BlockSpec(shape, lambda *_: (0,) * n, pipeline_mode=pl.Buffered(1))


def _proj0_kernel(x_ref, g_ref, wh_ref, wm_ref, qg_ref, wqn_ref, wqr_ref, wqs_ref, kvg_ref, wkv_ref,
                  cos_ref, sin_ref, zh_ref, q_ref, k_ref, vt_ref):
    xn = _rms(x_ref[0], g_ref[...]).astype(BF16)
    zh_ref[0] = jnp.dot(xn, wh_ref[...], preferred_element_type=F32)
    zm = jnp.dot(xn, wm_ref[...], preferred_element_type=F32)
    cos = cos_ref[...]
    sin = sin_ref[...]
    c0 = MLA_Q_LORA
    c1 = c0 + MLA_KV_LORA
    cqn = _rms(zm[:, :c0], qg_ref[...]).astype(BF16)
    ckvn = _rms(zm[:, c0:c1], kvg_ref[...]).astype(BF16)
    k_rope = zm[:, c1:c1 + LANES] * cos + zm[:, c1 + LANES:c1 + 2 * LANES] * sin
    k_rope = jnp.where(lax.broadcasted_iota(jnp.int32, k_rope.shape, 1) < MLA_ROPE, k_rope, 0.0).astype(BF16)
    qn = jnp.dot(cqn, wqn_ref[...], preferred_element_type=F32)
    qr = jnp.dot(cqn, wqr_ref[...], preferred_element_type=F32)
    qs = jnp.dot(cqn, wqs_ref[...], preferred_element_type=F32)
    kv = jnp.dot(ckvn, wkv_ref[...], preferred_element_type=F32)
    hw = MLA_HEADS * MLA_NOPE
    for h in range(MLA_HEADS):
        hs = slice(h * LANES, (h + 1) * LANES)
        q_ref[0, h, :, 0:LANES] = (qn[:, hs] * MLA_SCALE).astype(BF16)
        q_ref[0, h, :, LANES:2 * LANES] = ((qr[:, hs] * cos + qs[:, hs] * sin) * MLA_SCALE).astype(BF16)
        k_ref[0, h, :, 0:LANES] = kv[:, hs].astype(BF16)
        k_ref[0, h, :, LANES:2 * LANES] = k_rope
        vt_ref[0, h] = kv[:, hw + h * LANES:hw + (h + 1) * LANES].T.astype(BF16)


def _proj0(x, gain, wh, wm, qg, wqn, wqr, wqs, kvg, wkv, cos_t, sin_t, tm):
    b, s, d = x.shape
    grid = (b, s // tm)
    return pl.pallas_call(
        _proj0_kernel,
        grid=grid,
        in_specs=[
            pl.BlockSpec((1, tm, d), lambda i, j: (i, j, 0)),
            _full(gain.shape), _resident(wh.shape), _resident(wm.shape), _full(qg.shape), _resident(wqn.shape),
            _resident(wqr.shape), _resident(wqs.shape), _full(kvg.shape), _resident(wkv.shape),
            pl.BlockSpec((tm, LANES), lambda i, j: (j, 0)),
            pl.BlockSpec((tm, LANES), lambda i, j: (j, 0)),
        ],
        out_specs=[
            pl.BlockSpec((1, tm, wh.shape[1]), lambda i, j: (i, j, 0)),
            pl.BlockSpec((1, MLA_HEADS, tm, MLA_QK_PAD), lambda i, j: (i, 0, j, 0)),
            pl.BlockSpec((1, MLA_HEADS, tm, MLA_QK_PAD), lambda i, j: (i, 0, j, 0)),
            pl.BlockSpec((1, MLA_HEADS, MLA_V, tm), lambda i, j: (i, 0, 0, j)),
        ],
        out_shape=[
            jax.ShapeDtypeStruct((b, s, wh.shape[1]), F32),
            jax.ShapeDtypeStruct((b, MLA_HEADS, s, MLA_QK_PAD), BF16),
            jax.ShapeDtypeStruct((b, MLA_HEADS, s, MLA_QK_PAD), BF16),
            jax.ShapeDtypeStruct((b, MLA_HEADS, MLA_V, s), BF16),
        ],
        compiler_params=_params("parallel", "parallel"),
        name="proj0",
    )(x, gain, wh, wm, qg, wqn, wqr, wqs, kvg, wkv, cos_t, sin_t)


def _proj1_kernel(x_ref, g_ref, w_ref, cos_ref, sin_ref, q_ref, k_ref, vt_ref):
    d = x_ref.shape[2]
    xn = _rms(x_ref[0], g_ref[...]).astype(BF16)
    z = jnp.dot(xn, w_ref[...], preferred_element_type=F32)
    cos = cos_ref[...]
    sin = sin_ref[...]
    lane = lax.broadcasted_iota(jnp.int32, cos.shape, 1)
    first_half = (lane % DIFF_DIM) < (DIFF_DIM // 2)
    half = DIFF_DIM // 2
    for h in range(DIFF_HEADS):
        hs = slice(h * LANES, (h + 1) * LANES)
        for base, ref, scale in ((0, q_ref, DIFF_SCALE), (d, k_ref, 1.0)):
            t = z[:, base + h * LANES:base + (h + 1) * LANES]
            partner = jnp.where(first_half, pltpu.roll(t, LANES - half, axis=1), pltpu.roll(t, half, axis=1))
            ref[0, :, hs] = ((t * cos + partner * sin) * scale).astype(BF16)
        vt_ref[0, h] = z[:, 2 * d + h * LANES:2 * d + (h + 1) * LANES].T.astype(BF16)


def _proj1(x, gain, w, cos_t, sin_t, tm):
    b, s, d = x.shape
    return pl.pallas_call(
        _proj1_kernel,
        grid=(b, s // tm),
        in_specs=[
            pl.BlockSpec((1, tm, d), lambda i, j: (i, j, 0)),
            _full(gain.shape), _resident(w.shape),
            pl.BlockSpec((tm, LANES), lambda i, j: (j, 0)),
            pl.BlockSpec((tm, LANES), lambda i, j: (j, 0)),
        ],
        out_specs=[
            pl.BlockSpec((1, tm, d), lambda i, j: (i, j, 0)),
            pl.BlockSpec((1, tm, d), lambda i, j: (i, j, 0)),
            pl.BlockSpec((1, DIFF_HEADS, 2 * DIFF_DIM, tm), lambda i, j: (i, 0, 0, j)),
        ],
        out_shape=[
            jax.ShapeDtypeStruct((b, s, d), BF16),
            jax.ShapeDtypeStruct((b, s, d), BF16),
            jax.ShapeDtypeStruct((b, DIFF_HEADS, 2 * DIFF_DIM, s), BF16),
        ],
        compiler_params=_params("parallel", "parallel"),
        name="proj1",
    )(x, gain, w, cos_t, sin_t)


def _hgrn_pair_masks(rev):
    n = HGRN_CHUNK
    t = np.arange(n)[:, None]
    s = np.arange(n)[None, :]
    masks = [t == s]
    m = 1
    while m < n:
        t_blk, s_blk = t // m, s // m
        if rev:
            masks.append((s_blk - t_blk == 1) & (t_blk % 2 == 0))
        else:
            masks.append((t_blk - s_blk == 1) & (t_blk % 2 == 1))
        m *= 2
    return jnp.asarray(np.stack(masks).astype(np.float32))


def _hgrn_chunk(q, k, v, g, pair_ref, st_ref, rev):
    n = HGRN_CHUNK
    row = lax.broadcasted_iota(jnp.int32, g.shape, 0)
    p_sum = g
    total = g
    levels = [(q.astype(BF16), k.astype(BF16))]
    m = 1
    while m < n:
        levels.append(((q * jnp.exp2(p_sum)).astype(BF16), (k * jnp.exp2(total - p_sum)).astype(BF16)))
        in_right = (row % (2 * m)) >= m
        other = jnp.where(in_right, pltpu.roll(total, m, axis=0), pltpu.roll(total, n - m, axis=0))
        p_sum = p_sum + (jnp.where(in_right, 0.0, other) if rev else jnp.where(in_right, other, 0.0))
        total = total + other
        m *= 2
    q_in = (q * jnp.exp2(p_sum)).astype(BF16)
    k_out = (k * jnp.exp2(total - p_sum)).astype(BF16)
    chunk_decay = jnp.exp2(total[0:1])
    vb = v.astype(BF16)
    nt = (((1,), (1,)), ((), ()))
    tn = (((0,), (0,)), ((), ()))
    outs = []
    for h in range(HGRN_HEADS):
        hs = slice(h * HGRN_DIM, (h + 1) * HGRN_DIM)
        a = None
        for lvl, (qt, kt) in enumerate(levels):
            part = lax.dot_general(qt[:, hs], kt[:, hs], nt, preferred_element_type=F32) * pair_ref[lvl]
            a = part if a is None else a + part
        st = st_ref[h]
        o = jnp.dot(a.astype(BF16), vb[:, hs], preferred_element_type=F32)
        o = o + lax.dot_general(q_in[:, hs], st.astype(BF16), nt, preferred_element_type=F32)
        st_ref[h] = st * chunk_decay[:, hs] + lax.dot_general(vb[:, hs], k_out[:, hs], tn,
                                                               preferred_element_type=F32)
        outs.append(o)
    return outs


def _hgrn_gates(qh, fp, lb):
    q = qh * jax.nn.sigmoid(qh) * (HGRN_DIM ** -0.5)
    g = jnp.log2(lb + (1.0 - lb) * jax.nn.sigmoid(fp))
    k = (1.0 - lb) * jax.nn.sigmoid(-fp)
    return q, k, g


def _hgrn_fwd_kernel(q_ref, f_ref, v_ref, lb_ref, pair_ref, o_ref, st_ref):
    @pl.when(pl.program_id(1) == 0)
    def _():
        st_ref[...] = jnp.zeros_like(st_ref)

    lb = lb_ref[...]
    for c in range(HGRN_BLOCK // HGRN_CHUNK):
        rs = slice(c * HGRN_CHUNK, (c + 1) * HGRN_CHUNK)
        for b in range(q_ref.shape[0]):
            q, k, g = _hgrn_gates(q_ref[b, rs, :], f_ref[b, rs, :], lb)
            outs = _hgrn_chunk(q, k, v_ref[b, rs, :], g, pair_ref, st_ref.at[b], False)
            for h, o in enumerate(outs):
                o_ref[b, rs, h * HGRN_DIM:(h + 1) * HGRN_DIM] = o


def _hgrn_bwd_kernel(q_ref, f_ref, v_ref, gate_ref, of_ref, lb_ref, ng_ref, pair_ref, o_ref, st_ref):
    @pl.when(pl.program_id(1) == 0)
    def _():
        st_ref[...] = jnp.zeros_like(st_ref)

    lb = lb_ref[...]
    for c in reversed(range(HGRN_BLOCK // HGRN_CHUNK)):
        rs = slice(c * HGRN_CHUNK, (c + 1) * HGRN_CHUNK)
        for b in range(q_ref.shape[0]):
            q, k, g = _hgrn_gates(q_ref[b, rs, :], f_ref[b, rs, :], lb)
            outs = _hgrn_chunk(q, k, v_ref[b, rs, :], g, pair_ref, st_ref.at[b], True)
            gate = gate_ref[b, rs, :]
            gate = gate * jax.nn.sigmoid(gate)
            for h, o in enumerate(outs):
                hs = slice(h * HGRN_DIM, (h + 1) * HGRN_DIM)
                y = _rms(o + of_ref[b, rs, hs], ng_ref[:, hs])
                o_ref[b, rs, hs] = (y * gate[:, hs]).astype(BF16)


def _hgrn(zh, lb_fwd, lb_bwd, norm_gain):
    b, s, _ = zh.shape
    nb = s // HGRN_BLOCK
    group = 2 if b % 2 == 0 else 1
    state = pltpu.VMEM((group, HGRN_HEADS, HGRN_DIM, HGRN_DIM), F32)

    def col(c, rev):
        if rev:
            return pl.BlockSpec((group, HGRN_BLOCK, HGRN_W), lambda i, j: (i, nb - 1 - j, c))
        return pl.BlockSpec((group, HGRN_BLOCK, HGRN_W), lambda i, j: (i, j, c))

    pair_fwd, pair_bwd = _hgrn_pair_masks(False), _hgrn_pair_masks(True)
    o_fwd = pl.pallas_call(
        _hgrn_fwd_kernel,
        grid=(b // group, nb),
        in_specs=[col(0, False), col(1, False), col(3, False), _full(lb_fwd.shape), _full(pair_fwd.shape)],
        out_specs=col(0, False),
        out_shape=jax.ShapeDtypeStruct((b, s, HGRN_W), F32),
        scratch_shapes=[state],
        compiler_params=_params("parallel", "arbitrary"),
        name="hgrn_fwd",
    )(zh, zh, zh, lb_fwd, pair_fwd)
    return pl.pallas_call(
        _hgrn_bwd_kernel,
        grid=(b // group, nb),
        in_specs=[col(0, True), col(2, True), col(3, True), col(4, True), col(0, True),
                  _full(lb_bwd.shape), _full(norm_gain.shape), _full(pair_bwd.shape)],
        out_specs=col(0, True),
        out_shape=jax.ShapeDtypeStruct((b, s, HGRN_W), BF16),
        scratch_shapes=[state],
        compiler_params=_params("parallel", "arbitrary"),
        name="hgrn_bwd",
    )(zh, zh, zh, zh, o_fwd, lb_bwd, norm_gain, pair_bwd)


MIN_FAST_ROW_SUM = 2.0 ** -60
ONLINE_KEY_CHUNK = 256


def _fast_sweep(k_slice, vt_slice, qst_ref, shift, l8_ref, acc_ref, p_bufs, n_kv, tk):
    w = qst_ref.shape[1]
    l8_ref[...] = jnp.zeros_like(l8_ref)
    acc_ref[...] = jnp.zeros_like(acc_ref)

    def probs(j, slot):
        s = jnp.dot(k_slice(j * tk, tk), qst_ref[...], preferred_element_type=F32)
        p = jnp.exp2(s - shift)
        l8_ref[...] += jnp.sum(p.reshape(tk // 8, 8, w), axis=0)
        p_bufs[slot][...] = p.astype(BF16)

    def weighted(j, slot):
        acc_ref[...] += jnp.dot(vt_slice(j * tk, tk), p_bufs[slot][...], preferred_element_type=F32)

    probs(0, 0)
    for j in range(1, n_kv):
        weighted(j - 1, (j - 1) % 2)
        probs(j, j % 2)
    weighted(n_kv - 1, (n_kv - 1) % 2)


def _online_sweep(k_slice, vt_slice, qst_ref, m_ref, l_ref, acc_ref, n_kv, tk):
    m_ref[...] = jnp.full(m_ref.shape, -jnp.inf, F32)
    l_ref[...] = jnp.zeros_like(l_ref)
    acc_ref[...] = jnp.zeros_like(acc_ref)

    def body(j, carry):
        off = pl.multiple_of(j * tk, tk)
        s = jnp.dot(k_slice(off, tk), qst_ref[...], preferred_element_type=F32)
        m_old = m_ref[...]
        m_new = jnp.maximum(m_old, jnp.max(s, axis=0, keepdims=True))
        alpha = jnp.exp2(m_old - m_new)
        p = jnp.exp2(s - m_new)
        l_ref[...] = alpha * l_ref[...] + jnp.sum(p, axis=0, keepdims=True)
        acc_ref[...] = alpha * acc_ref[...] + jnp.dot(vt_slice(off, tk), p.astype(BF16),
                                                      preferred_element_type=F32)
        m_ref[...] = m_new
        return carry

    lax.fori_loop(0, n_kv, body, 0)


def _finish_sweep(k_slice, vt_slice, qst_ref, m_ref, l_ref, l8_ref, acc_ref, n_keys, tk):
    row_sum = jnp.sum(l8_ref[...], axis=0, keepdims=True)
    l_ref[...] = row_sum
    n_bad = jnp.sum(jnp.where(row_sum >= MIN_FAST_ROW_SUM, 0.0, 1.0))

    @pl.when(n_bad > 0.0)
    def _():
        tk_online = min(tk, ONLINE_KEY_CHUNK)
        _online_sweep(k_slice, vt_slice, qst_ref, m_ref, l_ref, acc_ref, n_keys // tk_online, tk_online)


def _attn_scratch(tk, w, d, dv):
    return [pltpu.VMEM((d, w), BF16), pltpu.VMEM((1, w), F32), pltpu.VMEM((1, w), F32), pltpu.VMEM((8, w), F32),
            pltpu.VMEM((dv, w), F32), pltpu.VMEM((tk, w), BF16), pltpu.VMEM((tk, w), BF16),
            pltpu.VMEM((8, LANES), F32)]


def _max_row_norm(k, lane_mask):
    k2 = k.astype(F32)
    k2 = k2 * k2
    if lane_mask is not None:
        k2 = jnp.where(lane_mask, k2, 0.0)
    return jnp.sqrt(jnp.max(jnp.sum(k2, axis=1, keepdims=True), axis=0, keepdims=True))


def _mla_attn_kernel(q_ref, k_ref, vt_ref, o_ref, qst_ref, m_ref, l_ref, l8_ref, acc_ref, p0, p1, kn_ref,
                     *, tk):
    @pl.when(pl.program_id(2) == 0)
    def _():
        kn_ref[...] = jnp.broadcast_to(_max_row_norm(k_ref[0, 0], None), kn_ref.shape)

    qt = q_ref[0, 0].astype(F32).T
    qst_ref[...] = qt.astype(BF16)
    shift = jnp.sqrt(jnp.sum(qt * qt, axis=0, keepdims=True)) * kn_ref[0:1, 0:1]
    k_slice = lambda off, n: k_ref[0, 0, pl.ds(off, n), :]
    vt_slice = lambda off, n: vt_ref[0, 0, :, pl.ds(off, n)]
    n_keys = k_ref.shape[2]
    _fast_sweep(k_slice, vt_slice, qst_ref, shift, l8_ref, acc_ref, (p0, p1), n_keys // tk, tk)
    _finish_sweep(k_slice, vt_slice, qst_ref, m_ref, l_ref, l8_ref, acc_ref, n_keys, tk)
    out_t = acc_ref[...] / l_ref[...]
    o_ref[0] = out_t.T.astype(BF16)


def _mla_attn(q, k, vt, tq, tk):
    b, h, s, d = q.shape
    dv = vt.shape[2]
    return pl.pallas_call(
        functools.partial(_mla_attn_kernel, tk=tk),
        grid=(b, h, s // tq),
        in_specs=[
            pl.BlockSpec((1, 1, tq, d), lambda i, j, t: (i, j, t, 0)),
            pl.BlockSpec((1, 1, s, d), lambda i, j, t: (i, j, 0, 0)),
            pl.BlockSpec((1, 1, dv, s), lambda i, j, t: (i, j, 0, 0)),
        ],
        out_specs=pl.BlockSpec((1, tq, dv), lambda i, j, t: (i, t, j)),
        out_shape=jax.ShapeDtypeStruct((b, s, h * dv), BF16),
        scratch_shapes=_attn_scratch(tk, tq, d, dv),
        compiler_params=_params("parallel", "parallel", "arbitrary"),
        name="mla_attn",
    )(q, k, vt)


def _diff_attn_kernel(q_ref, k_ref, vt_ref, lq1_ref, lk1_ref, lq2_ref, lk2_ref, ng_ref, o_ref,
                      qst_ref, m_ref, l_ref, l8_ref, acc_ref, p0, p1, kn_ref, *, tk, lambda_init):
    tq = q_ref.shape[1]

    @pl.when(pl.program_id(2) == 0)
    def _():
        k_all = k_ref[0]
        first = lax.broadcasted_iota(jnp.int32, k_all.shape, 1) < DIFF_DIM
        kn_ref[0:1, :] = jnp.broadcast_to(_max_row_norm(k_all, first), (1, LANES))
        kn_ref[1:2, :] = jnp.broadcast_to(_max_row_norm(k_all, jnp.logical_not(first)), (1, LANES))

    q = q_ref[0].astype(F32)
    first = lax.broadcasted_iota(jnp.int32, q.shape, 1) < DIFF_DIM
    q1t = jnp.where(first, q, 0.0).T
    q2t = jnp.where(first, 0.0, q).T
    qst_ref[:, 0:tq] = q1t.astype(BF16)
    qst_ref[:, tq:2 * tq] = q2t.astype(BF16)
    shift = jnp.concatenate([jnp.sqrt(jnp.sum(q1t * q1t, axis=0, keepdims=True)) * kn_ref[0:1, 0:1],
                             jnp.sqrt(jnp.sum(q2t * q2t, axis=0, keepdims=True)) * kn_ref[1:2, 0:1]], axis=1)
    k_slice = lambda off, n: k_ref[0, pl.ds(off, n), :]
    vt_slice = lambda off, n: vt_ref[0, 0, :, pl.ds(off, n)]
    n_keys = k_ref.shape[1]
    _fast_sweep(k_slice, vt_slice, qst_ref, shift, l8_ref, acc_ref, (p0, p1), n_keys // tk, tk)
    _finish_sweep(k_slice, vt_slice, qst_ref, m_ref, l_ref, l8_ref, acc_ref, n_keys, tk)
    lam = (jnp.exp(jnp.sum(lq1_ref[...] * lk1_ref[...], axis=1, keepdims=True))
           - jnp.exp(jnp.sum(lq2_ref[...] * lk2_ref[...], axis=1, keepdims=True)) + lambda_init)
    soft = acc_ref[...] / l_ref[...]
    out_t = soft[:, 0:tq] - lam * soft[:, tq:2 * tq]
    y = _rms(out_t.T, ng_ref[...]) * (1.0 - lambda_init)
    o_ref[0] = y.astype(BF16)


def _diff_attn(q, k, vt, lq1, lk1, lq2, lk2, norm_gain, lambda_init, tq, tk):
    b, s, d = q.shape
    h = vt.shape[1]
    dv = vt.shape[2]
    small = _full(lq1.shape)
    return pl.pallas_call(
        functools.partial(_diff_attn_kernel, tk=tk, lambda_init=lambda_init),
        grid=(b, h, s // tq),
        in_specs=[
            pl.BlockSpec((1, tq, dv), lambda i, j, t: (i, t, j)),
            pl.BlockSpec((1, s, dv), lambda i, j, t: (i, 0, j)),
            pl.BlockSpec((1, 1, dv, s), lambda i, j, t: (i, j, 0, 0)),
            small, small, small, small, _full(norm_gain.shape),
        ],
        out_specs=pl.BlockSpec((1, tq, dv), lambda i, j, t: (i, t, j)),
        out_shape=jax.ShapeDtypeStruct((b, s, d), BF16),
        scratch_shapes=_attn_scratch(tk, 2 * tq, dv, dv),
        compiler_params=_params("parallel", "parallel", "arbitrary"),
        name="diff_attn",
    )(q, k, vt, lq1, lk1, lq2, lk2, norm_gain)


def _mix_ffn_kernel(*refs, n_mix, final, th):
    x_ref = refs[0]
    mix_refs = refs[1:1 + n_mix]
    wo_ref, g_ref, wg_ref, wu_ref, wd_ref, gf_ref, out_ref = refs[1 + n_mix:]
    mix = jnp.concatenate([m_ref[...] for m_ref in mix_refs], axis=1)
    x1 = x_ref[...] + jnp.dot(mix, wo_ref[...], preferred_element_type=F32)
    xn = _rms(x1, g_ref[...]).astype(BF16)
    y = x1
    for c in range(wg_ref.shape[1] // th):
        cs = slice(c * th, (c + 1) * th)
        gate = jnp.dot(xn, wg_ref[:, cs], preferred_element_type=F32)
        up = jnp.dot(xn, wu_ref[:, cs], preferred_element_type=F32)
        hid = (gate * jax.nn.sigmoid(gate) * up).astype(BF16)
        y = y + jnp.dot(hid, wd_ref[cs, :], preferred_element_type=F32)
    if final:
        y = _rms(y, gf_ref[...])
    out_ref[...] = y


def _mix_ffn(x, mixes, w_out, gain, wg, wu, wd, final_gain, final, tm, th):
    t, d = x.shape
    assert wg.shape[1] % th == 0 and sum(m.shape[1] for m in mixes) == w_out.shape[0]
    n_mix = len(mixes)
    row = lambda i: (i, 0)
    in_specs = [pl.BlockSpec((tm, d), row)]
    in_specs += [pl.BlockSpec((tm, m.shape[1]), row) for m in mixes]
    in_specs += [_resident(w_out.shape), _full(gain.shape), _resident(wg.shape), _resident(wu.shape),
                 _resident(wd.shape), _full(final_gain.shape)]
    return pl.pallas_call(
        functools.partial(_mix_ffn_kernel, n_mix=n_mix, final=final, th=th),
        grid=(t // tm,),
        in_specs=in_specs,
        out_specs=pl.BlockSpec((tm, d), row),
        out_shape=jax.ShapeDtypeStruct((t, d), F32),
        compiler_params=_params("parallel"),
        name="mix_ffn",
    )(x, *mixes, w_out, gain, wg, wu, wd, final_gain)


def _rope_tables(seq_len):
    dim = MLA_ROPE
    inv = 1.0 / (ROPE_THETA ** (jnp.arange(0, dim, 2, dtype=F32) / dim))
    ang = jnp.arange(seq_len, dtype=F32)[:, None] * inv[None, :]
    cos, sin = jnp.cos(ang), jnp.sin(ang)
    cos_t = jnp.concatenate([cos, cos, cos, cos], axis=1)
    sin_t = jnp.concatenate([-sin, sin, -sin, sin], axis=1)
    return cos_t, sin_t


def _swap_halves(w):
    half = w.shape[-1] // 2
    return jnp.concatenate([w[..., half:], w[..., :half]], axis=-1)


def _pad_heads(w):
    kdim, h, r = w.shape
    return jnp.concatenate([w, jnp.zeros_like(w)], axis=-1).reshape(kdim, h * 2 * r)


def _row(v):
    return v.reshape(1, -1).astype(F32)


def _tiles(b, s):
    assert s % 512 == 0
    tm_proj = 512
    tm_ffn = 1024 if (b * s) % 1024 == 0 else 512
    th_ffn = 256
    return tm_proj, tm_ffn, th_ffn, min(1024, s), 512, min(4096, s)


def kernel(x, norm_attn, norm_ffn, ffn_w_gate, ffn_w_up, ffn_w_down, ab_w_in, hgrn_lower_bound, hgrn_out_norm,
           mla_q_norm, mla_w_uq, mla_kv_norm, mla_w_ukv, ab_w_out, c_w_in, diff_lambda_q1, diff_lambda_k1,
           diff_lambda_q2, diff_lambda_k2, diff_out_norm, c_w_out, final_norm):
    b, s, d = x.shape
    depth = norm_attn.shape[0]
    assert DIFF_DIM == MLA_ROPE and d == DIFF_HEADS * 2 * DIFF_DIM and d == 2 * HGRN_W
    tm_proj, tm_ffn, th_ffn, tq_mla, tq_diff, tk = _tiles(b, s)

    cos_t, sin_t = _rope_tables(s)
    lower_bounds = jnp.cumsum(jax.nn.softmax(hgrn_lower_bound.astype(F32), axis=1), axis=1)

    for layer in range(depth):
        j = layer // 2
        gain = _row(norm_attn[layer])
        if layer % 2 == 0:
            w_in = ab_w_in[j]
            c_h = 5 * HGRN_W
            c_q = c_h + MLA_Q_LORA
            c_kv = c_q + MLA_KV_LORA
            w_kr = w_in[:, c_kv:]
            w_krs = _swap_halves(w_kr)
            wh = w_in[:, :c_h].astype(BF16)
            wm = jnp.concatenate([w_in[:, c_h:c_kv], w_kr, w_kr, w_krs, w_krs], axis=1).astype(BF16)
            wuq = mla_w_uq[j].reshape(MLA_Q_LORA, MLA_HEADS, MLA_NOPE + MLA_ROPE)
            wqn = wuq[..., :MLA_NOPE].reshape(MLA_Q_LORA, MLA_HEADS * MLA_NOPE).astype(BF16)
            wqr = _pad_heads(wuq[..., MLA_NOPE:]).astype(BF16)
            wqs = _pad_heads(_swap_halves(wuq[..., MLA_NOPE:])).astype(BF16)
            wukv = mla_w_ukv[j].reshape(MLA_KV_LORA, MLA_HEADS, MLA_NOPE + MLA_V)
            wkv = jnp.concatenate([wukv[..., :MLA_NOPE].reshape(MLA_KV_LORA, -1),
                                   wukv[..., MLA_NOPE:].reshape(MLA_KV_LORA, -1)], axis=1).astype(BF16)
            zh, q, k, vt = _proj0(x, gain, wh, wm, _row(mla_q_norm[j]), wqn, wqr, wqs, _row(mla_kv_norm[j]),
                                  wkv, cos_t, sin_t, tm_proj)
            o_a = _hgrn(zh, _row(lower_bounds[0, j]), _row(lower_bounds[1, j]), _row(hgrn_out_norm[j]))
            o_b = _mla_attn(q, k, vt, tq_mla, tk)
            w_out = ab_w_out[j].astype(BF16)
            mixes = [o_a.reshape(b * s, HGRN_W), o_b.reshape(b * s, MLA_HEADS * MLA_V)]
        else:
            lambda_init = 0.8 - 0.6 * math.exp(-0.3 * layer)
            q, k, vt = _proj1(x, gain, c_w_in[j].astype(BF16), cos_t, sin_t, tm_proj)
            o_c = _diff_attn(q, k, vt, _row(diff_lambda_q1[j]), _row(diff_lambda_k1[j]),
                             _row(diff_lambda_q2[j]), _row(diff_lambda_k2[j]), _row(diff_out_norm[j]),
                             lambda_init, tq_diff, tk)
            mixes = [o_c.reshape(b * s, d)]
            w_out = c_w_out[j].astype(BF16)
        x = _mix_ffn(x.reshape(b * s, d), mixes, w_out, _row(norm_ffn[layer]),
                     ffn_w_gate[layer].astype(BF16), ffn_w_up[layer].astype(BF16),
                     ffn_w_down[layer].astype(BF16), _row(final_norm),
                     layer == depth - 1, tm_ffn, th_ffn).reshape(b, s, d)
    return x
```

```python
import functools
import math

import jax
import jax.numpy as jnp
import numpy as np
from jax import lax
from jax.experimental import pallas as pl
from jax.experimental.pallas import tpu as pltpu

F32 = jnp.float32
BF16 = jnp.bfloat16

NORM_EPS = 1e-6
ROPE_THETA = 10000.0
LANES = 128
NORM_SLAB = (8, LANES)
V7X_VMEM_BYTES = 64 * 1024 * 1024
VMEM_LIMIT = V7X_VMEM_BYTES * 7 // 8

HGRN_HEADS = 4
HGRN_DIM = 128
HGRN_W = HGRN_HEADS * HGRN_DIM
HGRN_CHUNK = 64
HGRN_BLOCK = 256

MLA_HEADS = 4
MLA_NOPE = 128
MLA_ROPE = 64
MLA_V = 128
MLA_Q_LORA = 384
MLA_KV_LORA = 256
MLA_QK_PAD = 256
LOG2_E = math.log2(math.e)
MLA_SCALE = (MLA_NOPE + MLA_ROPE) ** -0.5 * LOG2_E

DIFF_HEADS = 8
DIFF_DIM = 64
DIFF_SCALE = DIFF_DIM ** -0.5 * LOG2_E


def _rms(x, gain):
    ms = jnp.mean(x * x, axis=-1, keepdims=True)
    return x * lax.rsqrt(ms + NORM_EPS) * gain


def _params(*sem):
    return pltpu.CompilerParams(dimension_semantics=sem, vmem_limit_bytes=VMEM_LIMIT)


def _full(shape):
    n = len(shape)
    return pl.BlockSpec(shape, lambda *_: (0,) * n)


def _resident(shape):
    n = len(shape)
    return pl.BlockSpec(shape, lambda *_: (0,) * n, pipeline_mode=pl.Buffered(1))


def _row_sq_norm(k, lane_mask=None):
    k2 = k.astype(F32)
    k2 = k2 * k2
    if lane_mask is not None:
        k2 = jnp.where(lane_mask, k2, 0.0)
    return jnp.sum(k2, axis=1, keepdims=True)


def _tile_max(row_values, shape):
    return jnp.broadcast_to(jnp.max(row_values, axis=0, keepdims=True), shape)


def _proj0_kernel(x_ref, g_ref, wh_ref, wm_ref, qg_ref, wqn_ref, wqr_ref, wqs_ref, kvg_ref, wkv_ref,
                  cos_ref, sin_ref, zh_ref, q_ref, k_ref, vt_ref, kn_ref):
    xn = _rms(x_ref[0], g_ref[...]).astype(BF16)
    zh_ref[0] = jnp.dot(xn, wh_ref[...], preferred_element_type=F32)
    zm = jnp.dot(xn, wm_ref[...], preferred_element_type=F32)
    cos = cos_ref[...]
    sin = sin_ref[...]
    c0 = MLA_Q_LORA
    c1 = c0 + MLA_KV_LORA
    cqn = _rms(zm[:, :c0], qg_ref[...]).astype(BF16)
    ckvn = _rms(zm[:, c0:c1], kvg_ref[...]).astype(BF16)
    k_rope = zm[:, c1:c1 + LANES] * cos + zm[:, c1 + LANES:c1 + 2 * LANES] * sin
    k_rope = jnp.where(lax.broadcasted_iota(jnp.int32, k_rope.shape, 1) < MLA_ROPE, k_rope, 0.0).astype(BF16)
    qn = jnp.dot(cqn, wqn_ref[...], preferred_element_type=F32)
    qr = jnp.dot(cqn, wqr_ref[...], preferred_element_type=F32)
    qs = jnp.dot(cqn, wqs_ref[...], preferred_element_type=F32)
    kv = jnp.dot(ckvn, wkv_ref[...], preferred_element_type=F32)
    hw = MLA_HEADS * MLA_NOPE
    rope_sq = _row_sq_norm(k_rope)
    for h in range(MLA_HEADS):
        hs = slice(h * LANES, (h + 1) * LANES)
        q_ref[0, h, :, 0:LANES] = (qn[:, hs] * MLA_SCALE).astype(BF16)
        q_ref[0, h, :, LANES:2 * LANES] = ((qr[:, hs] * cos + qs[:, hs] * sin) * MLA_SCALE).astype(BF16)
        k_nope = kv[:, hs].astype(BF16)
        k_ref[0, h, :, 0:LANES] = k_nope
        k_ref[0, h, :, LANES:2 * LANES] = k_rope
        kn_ref[0, h, 0] = _tile_max(_row_sq_norm(k_nope) + rope_sq, kn_ref.shape[3:])
        vt_ref[0, h] = kv[:, hw + h * LANES:hw + (h + 1) * LANES].T.astype(BF16)


def _proj0(x, gain, wh, wm, qg, wqn, wqr, wqs, kvg, wkv, cos_t, sin_t, tm):
    b, s, d = x.shape
    grid = (b, s // tm)
    return pl.pallas_call(
        _proj0_kernel,
        grid=grid,
        in_specs=[
            pl.BlockSpec((1, tm, d), lambda i, j: (i, j, 0)),
            _full(gain.shape), _resident(wh.shape), _resident(wm.shape), _full(qg.shape), _resident(wqn.shape),
            _resident(wqr.shape), _resident(wqs.shape), _full(kvg.shape), _resident(wkv.shape),
            pl.BlockSpec((tm, LANES), lambda i, j: (j, 0)),
            pl.BlockSpec((tm, LANES), lambda i, j: (j, 0)),
        ],
        out_specs=[
            pl.BlockSpec((1, tm, wh.shape[1]), lambda i, j: (i, j, 0)),
            pl.BlockSpec((1, MLA_HEADS, tm, MLA_QK_PAD), lambda i, j: (i, 0, j, 0)),
            pl.BlockSpec((1, MLA_HEADS, tm, MLA_QK_PAD), lambda i, j: (i, 0, j, 0)),
            pl.BlockSpec((1, MLA_HEADS, MLA_V, tm), lambda i, j: (i, 0, 0, j)),
            pl.BlockSpec((1, MLA_HEADS, 1) + NORM_SLAB, lambda i, j: (i, 0, j, 0, 0)),
        ],
        out_shape=[
            jax.ShapeDtypeStruct((b, s, wh.shape[1]), F32),
            jax.ShapeDtypeStruct((b, MLA_HEADS, s, MLA_QK_PAD), BF16),
            jax.ShapeDtypeStruct((b, MLA_HEADS, s, MLA_QK_PAD), BF16),
            jax.ShapeDtypeStruct((b, MLA_HEADS, MLA_V, s), BF16),
            jax.ShapeDtypeStruct((b, MLA_HEADS, s // tm) + NORM_SLAB, F32),
        ],
        compiler_params=_params("parallel", "parallel"),
        name="proj0",
    )(x, gain, wh, wm, qg, wqn, wqr, wqs, kvg, wkv, cos_t, sin_t)


def _proj1_kernel(x_ref, g_ref, w_ref, cos_ref, sin_ref, sel_ref, q_ref, k_ref, vt_ref, kn_ref):
    d = x_ref.shape[2]
    xn = _rms(x_ref[0], g_ref[...]).astype(BF16)
    z = jnp.dot(xn, w_ref[...], preferred_element_type=F32)
    cos = cos_ref[...]
    sin = sin_ref[...]
    lane = lax.broadcasted_iota(jnp.int32, cos.shape, 1)
    first_half = (lane % DIFF_DIM) < (DIFF_DIM // 2)
    half = DIFF_DIM // 2

    def rope(t):
        partner = jnp.where(first_half, pltpu.roll(t, LANES - half, axis=1), pltpu.roll(t, half, axis=1))
        return t * cos + partner * sin

    k_sq = []
    for h in range(DIFF_HEADS):
        hs = slice(h * LANES, (h + 1) * LANES)
        q_ref[0, :, hs] = (rope(z[:, hs]) * DIFF_SCALE).astype(BF16)
        k = rope(z[:, d + h * LANES:d + (h + 1) * LANES]).astype(BF16)
        k_ref[0, :, hs] = k
        k_sq.append(k * k)
        vt_ref[0, h] = z[:, 2 * d + h * LANES:2 * d + (h + 1) * LANES].T.astype(BF16)
    row_sq = jnp.dot(jnp.concatenate(k_sq, axis=1), sel_ref[...], preferred_element_type=F32)
    kn_ref[0, 0] = _tile_max(row_sq, NORM_SLAB)


def _proj1(x, gain, w, cos_t, sin_t, tm):
    b, s, d = x.shape
    sel = jnp.asarray(np.arange(d)[:, None] // DIFF_DIM == np.arange(LANES)[None, :], BF16)
    return pl.pallas_call(
        _proj1_kernel,
        grid=(b, s // tm),
        in_specs=[
            pl.BlockSpec((1, tm, d), lambda i, j: (i, j, 0)),
            _full(gain.shape), _resident(w.shape),
            pl.BlockSpec((tm, LANES), lambda i, j: (j, 0)),
            pl.BlockSpec((tm, LANES), lambda i, j: (j, 0)),
            _resident(sel.shape),
        ],
        out_specs=[
            pl.BlockSpec((1, tm, d), lambda i, j: (i, j, 0)),
            pl.BlockSpec((1, tm, d), lambda i, j: (i, j, 0)),
            pl.BlockSpec((1, DIFF_HEADS, 2 * DIFF_DIM, tm), lambda i, j: (i, 0, 0, j)),
            pl.BlockSpec((1, 1) + NORM_SLAB, lambda i, j: (i, j, 0, 0)),
        ],
        out_shape=[
            jax.ShapeDtypeStruct((b, s, d), BF16),
            jax.ShapeDtypeStruct((b, s, d), BF16),
            jax.ShapeDtypeStruct((b, DIFF_HEADS, 2 * DIFF_DIM, s), BF16),
            jax.ShapeDtypeStruct((b, s // tm) + NORM_SLAB, F32),
        ],
        compiler_params=_params("parallel", "parallel"),
        name="proj1",
    )(x, gain, w, cos_t, sin_t, sel)


def _hgrn_pair_masks(rev):
    n = HGRN_CHUNK
    t = np.arange(n)[:, None]
    s = np.arange(n)[None, :]
    masks = [t == s]
    m = 1
    while m < n:
        t_blk, s_blk = t // m, s // m
        if rev:
            masks.append((s_blk - t_blk == 1) & (t_blk % 2 == 0))
        else:
            masks.append((t_blk - s_blk == 1) & (t_blk % 2 == 1))
        m *= 2
    return jnp.asarray(np.stack(masks).astype(np.float32))


def _hgrn_chunk(q, k, v, g, pair_ref, st_ref, rev):
    n = HGRN_CHUNK
    row = lax.broadcasted_iota(jnp.int32, g.shape, 0)
    p_sum = g
    total = g
    levels = [(q.astype(BF16), k.astype(BF16))]
    m = 1
    while m < n:
        levels.append(((q * jnp.exp2(p_sum)).astype(BF16), (k * jnp.exp2(total - p_sum)).astype(BF16)))
        in_right = (row % (2 * m)) >= m
        other = jnp.where(in_right, pltpu.roll(total, m, axis=0), pltpu.roll(total, n - m, axis=0))
        p_sum = p_sum + (jnp.where(in_right, 0.0, other) if rev else jnp.where(in_right, other, 0.0))
        total = total + other
        m *= 2
    q_in = (q * jnp.exp2(p_sum)).astype(BF16)
    k_out = (k * jnp.exp2(total - p_sum)).astype(BF16)
    chunk_decay = jnp.exp2(total[0:1])
    vb = v.astype(BF16)
    nt = (((1,), (1,)), ((), ()))
    tn = (((0,), (0,)), ((), ()))
    outs = []
    for h in range(HGRN_HEADS):
        hs = slice(h * HGRN_DIM, (h + 1) * HGRN_DIM)
        a = None
        for lvl, (qt, kt) in enumerate(levels):
            part = lax.dot_general(qt[:, hs], kt[:, hs], nt, preferred_element_type=F32) * pair_ref[lvl]
            a = part if a is None else a + part
        st = st_ref[h]
        o = jnp.dot(a.astype(BF16), vb[:, hs], preferred_element_type=F32)
        o = o + lax.dot_general(q_in[:, hs], st.astype(BF16), nt, preferred_element_type=F32)
        st_ref[h] = st * chunk_decay[:, hs] + lax.dot_general(vb[:, hs], k_out[:, hs], tn,
                                                               preferred_element_type=F32)
        outs.append(o)
    return outs


def _hgrn_gates(qh, fp, lb):
    q = qh * jax.nn.sigmoid(qh) * (HGRN_DIM ** -0.5)
    g = jnp.log2(lb + (1.0 - lb) * jax.nn.sigmoid(fp))
    k = (1.0 - lb) * jax.nn.sigmoid(-fp)
    return q, k, g


def _hgrn_fwd_kernel(q_ref, f_ref, v_ref, lb_ref, pair_ref, o_ref, st_ref):
    @pl.when(pl.program_id(1) == 0)
    def _():
        st_ref[...] = jnp.zeros_like(st_ref)

    lb = lb_ref[...]
    for c in range(HGRN_BLOCK // HGRN_CHUNK):
        rs = slice(c * HGRN_CHUNK, (c + 1) * HGRN_CHUNK)
        for b in range(q_ref.shape[0]):
            q, k, g = _hgrn_gates(q_ref[b, rs, :], f_ref[b, rs, :], lb)
            outs = _hgrn_chunk(q, k, v_ref[b, rs, :], g, pair_ref, st_ref.at[b], False)
            for h, o in enumerate(outs):
                o_ref[b, rs, h * HGRN_DIM:(h + 1) * HGRN_DIM] = o


def _hgrn_bwd_kernel(q_ref, f_ref, v_ref, gate_ref, of_ref, lb_ref, ng_ref, pair_ref, o_ref, st_ref):
    @pl.when(pl.program_id(1) == 0)
    def _():
        st_ref[...] = jnp.zeros_like(st_ref)

    lb = lb_ref[...]
    for c in reversed(range(HGRN_BLOCK // HGRN_CHUNK)):
        rs = slice(c * HGRN_CHUNK, (c + 1) * HGRN_CHUNK)
        for b in range(q_ref.shape[0]):
            q, k, g = _hgrn_gates(q_ref[b, rs, :], f_ref[b, rs, :], lb)
            outs = _hgrn_chunk(q, k, v_ref[b, rs, :], g, pair_ref, st_ref.at[b], True)
            gate = gate_ref[b, rs, :]
            gate = gate * jax.nn.sigmoid(gate)
            for h, o in enumerate(outs):
                hs = slice(h * HGRN_DIM, (h + 1) * HGRN_DIM)
                y = _rms(o + of_ref[b, rs, hs], ng_ref[:, hs])
                o_ref[b, rs, hs] = (y * gate[:, hs]).astype(BF16)


def _hgrn(zh, lb_fwd, lb_bwd, norm_gain):
    b, s, _ = zh.shape
    nb = s // HGRN_BLOCK
    group = 2 if b % 2 == 0 else 1
    state = pltpu.VMEM((group, HGRN_HEADS, HGRN_DIM, HGRN_DIM), F32)

    def col(c, rev):
        if rev:
            return pl.BlockSpec((group, HGRN_BLOCK, HGRN_W), lambda i, j: (i, nb - 1 - j, c))
        return pl.BlockSpec((group, HGRN_BLOCK, HGRN_W), lambda i, j: (i, j, c))

    pair_fwd, pair_bwd = _hgrn_pair_masks(False), _hgrn_pair_masks(True)
    o_fwd = pl.pallas_call(
        _hgrn_fwd_kernel,
        grid=(b // group, nb),
        in_specs=[col(0, False), col(1, False), col(3, False), _full(lb_fwd.shape), _full(pair_fwd.shape)],
        out_specs=col(0, False),
        out_shape=jax.ShapeDtypeStruct((b, s, HGRN_W), F32),
        scratch_shapes=[state],
        compiler_params=_params("parallel", "arbitrary"),
        name="hgrn_fwd",
    )(zh, zh, zh, lb_fwd, pair_fwd)
    return pl.pallas_call(
        _hgrn_bwd_kernel,
        grid=(b // group, nb),
        in_specs=[col(0, True), col(2, True), col(3, True), col(4, True), col(0, True),
                  _full(lb_bwd.shape), _full(norm_gain.shape), _full(pair_bwd.shape)],
        out_specs=col(0, True),
        out_shape=jax.ShapeDtypeStruct((b, s, HGRN_W), BF16),
        scratch_shapes=[state],
        compiler_params=_params("parallel", "arbitrary"),
        name="hgrn_bwd",
    )(zh, zh, zh, zh, o_fwd, lb_bwd, norm_gain, pair_bwd)


MIN_FAST_ROW_SUM = 2.0 ** -60
ONLINE_KEY_CHUNK = 256


def _fast_sweep(k_slice, vt_slice, qst_ref, shift, l8_ref, acc_ref, p_bufs, n_kv, tk):
    w = qst_ref.shape[1]
    l8_ref[...] = jnp.zeros_like(l8_ref)
    acc_ref[...] = jnp.zeros_like(acc_ref)

    def probs(j, slot):
        s = jnp.dot(k_slice(j * tk, tk), qst_ref[...], preferred_element_type=F32)
        p = jnp.exp2(s - shift)
        l8_ref[...] += jnp.sum(p.reshape(tk // 8, 8, w), axis=0)
        p_bufs[slot][...] = p.astype(BF16)

    def weighted(j, slot):
        acc_ref[...] += jnp.dot(vt_slice(j * tk, tk), p_bufs[slot][...], preferred_element_type=F32)

    probs(0, 0)
    for j in range(1, n_kv):
        weighted(j - 1, (j - 1) % 2)
        probs(j, j % 2)
    weighted(n_kv - 1, (n_kv - 1) % 2)


def _online_sweep(k_slice, vt_slice, qst_ref, m_ref, l_ref, acc_ref, n_kv, tk):
    m_ref[...] = jnp.full(m_ref.shape, -jnp.inf, F32)
    l_ref[...] = jnp.zeros_like(l_ref)
    acc_ref[...] = jnp.zeros_like(acc_ref)

    def body(j, carry):
        off = pl.multiple_of(j * tk, tk)
        s = jnp.dot(k_slice(off, tk), qst_ref[...], preferred_element_type=F32)
        m_old = m_ref[...]
        m_new = jnp.maximum(m_old, jnp.max(s, axis=0, keepdims=True))
        alpha = jnp.exp2(m_old - m_new)
        p = jnp.exp2(s - m_new)
        l_ref[...] = alpha * l_ref[...] + jnp.sum(p, axis=0, keepdims=True)
        acc_ref[...] = alpha * acc_ref[...] + jnp.dot(vt_slice(off, tk), p.astype(BF16),
                                                      preferred_element_type=F32)
        m_ref[...] = m_new
        return carry

    lax.fori_loop(0, n_kv, body, 0)


def _finish_sweep(k_slice, vt_slice, qst_ref, m_ref, l_ref, l8_ref, acc_ref, n_keys, tk):
    row_sum = jnp.sum(l8_ref[...], axis=0, keepdims=True)
    l_ref[...] = row_sum
    n_bad = jnp.sum(jnp.where(row_sum >= MIN_FAST_ROW_SUM, 0.0, 1.0))

    @pl.when(n_bad > 0.0)
    def _():
        tk_online = min(tk, ONLINE_KEY_CHUNK)
        _online_sweep(k_slice, vt_slice, qst_ref, m_ref, l_ref, acc_ref, n_keys // tk_online, tk_online)


def _attn_scratch(tk, w, d, dv):
    return [pltpu.VMEM((d, w), BF16), pltpu.VMEM((1, w), F32), pltpu.VMEM((1, w), F32), pltpu.VMEM((8, w), F32),
            pltpu.VMEM((dv, w), F32), pltpu.VMEM((tk, w), BF16), pltpu.VMEM((tk, w), BF16)]


def _mla_attn_kernel(q_ref, k_ref, vt_ref, kn_ref, o_ref, qst_ref, m_ref, l_ref, l8_ref, acc_ref, p0, p1,
                     *, tk):
    max_key_norm = jnp.sqrt(jnp.max(kn_ref[0, 0], axis=0))
    qt = q_ref[0, 0].astype(F32).T
    qst_ref[...] = qt.astype(BF16)
    shift = jnp.sqrt(jnp.sum(qt * qt, axis=0, keepdims=True)) * max_key_norm[0:1, 0:1]
    k_slice = lambda off, n: k_ref[0, 0, pl.ds(off, n), :]
    vt_slice = lambda off, n: vt_ref[0, 0, :, pl.ds(off, n)]
    n_keys = k_ref.shape[2]
    _fast_sweep(k_slice, vt_slice, qst_ref, shift, l8_ref, acc_ref, (p0, p1), n_keys // tk, tk)
    _finish_sweep(k_slice, vt_slice, qst_ref, m_ref, l_ref, l8_ref, acc_ref, n_keys, tk)
    out_t = acc_ref[...] / l_ref[...]
    o_ref[0] = out_t.T.astype(BF16)


def _mla_attn(q, k, vt, kn, tq, tk):
    b, h, s, d = q.shape
    dv = vt.shape[2]
    return pl.pallas_call(
        functools.partial(_mla_attn_kernel, tk=tk),
        grid=(b, h, s // tq),
        in_specs=[
            pl.BlockSpec((1, 1, tq, d), lambda i, j, t: (i, j, t, 0)),
            pl.BlockSpec((1, 1, s, d), lambda i, j, t: (i, j, 0, 0)),
            pl.BlockSpec((1, 1, dv, s), lambda i, j, t: (i, j, 0, 0)),
            pl.BlockSpec((1, 1) + kn.shape[2:], lambda i, j, t: (i, j, 0, 0, 0)),
        ],
        out_specs=pl.BlockSpec((1, tq, dv), lambda i, j, t: (i, t, j)),
        out_shape=jax.ShapeDtypeStruct((b, s, h * dv), BF16),
        scratch_shapes=_attn_scratch(tk, tq, d, dv),
        compiler_params=_params("parallel", "parallel", "parallel"),
        name="mla_attn",
    )(q, k, vt, kn)


def _diff_attn_kernel(q_ref, k_ref, vt_ref, kn_ref, lq1_ref, lk1_ref, lq2_ref, lk2_ref, ng_ref, o_ref,
                      qst_ref, m_ref, l_ref, l8_ref, acc_ref, p0, p1, *, tk, lambda_init):
    tq = q_ref.shape[1]
    key_sq = jnp.max(kn_ref[0], axis=0)
    chunk = lax.broadcasted_iota(jnp.int32, key_sq.shape, 1) - 2 * pl.program_id(1)
    max_key_norm = [jnp.sqrt(jnp.max(jnp.where(chunk == c, key_sq, 0.0), axis=1, keepdims=True))[0:1]
                    for c in range(2)]

    q = q_ref[0].astype(F32)
    first = lax.broadcasted_iota(jnp.int32, q.shape, 1) < DIFF_DIM
    q1t = jnp.where(first, q, 0.0).T
    q2t = jnp.where(first, 0.0, q).T
    qst_ref[:, 0:tq] = q1t.astype(BF16)
    qst_ref[:, tq:2 * tq] = q2t.astype(BF16)
    shift = jnp.concatenate([jnp.sqrt(jnp.sum(q1t * q1t, axis=0, keepdims=True)) * max_key_norm[0],
                             jnp.sqrt(jnp.sum(q2t * q2t, axis=0, keepdims=True)) * max_key_norm[1]], axis=1)
    k_slice = lambda off, n: k_ref[0, pl.ds(off, n), :]
    vt_slice = lambda off, n: vt_ref[0, 0, :, pl.ds(off, n)]
    n_keys = k_ref.shape[1]
    _fast_sweep(k_slice, vt_slice, qst_ref, shift, l8_ref, acc_ref, (p0, p1), n_keys // tk, tk)
    _finish_sweep(k_slice, vt_slice, qst_ref, m_ref, l_ref, l8_ref, acc_ref, n_keys, tk)
    lam = (jnp.exp(jnp.sum(lq1_ref[...] * lk1_ref[...], axis=1, keepdims=True))
           - jnp.exp(jnp.sum(lq2_ref[...] * lk2_ref[...], axis=1, keepdims=True)) + lambda_init)
    soft = acc_ref[...] / l_ref[...]
    out_t = soft[:, 0:tq] - lam * soft[:, tq:2 * tq]
    y = _rms(out_t.T, ng_ref[...]) * (1.0 - lambda_init)
    o_ref[0] = y.astype(BF16)


def _diff_attn(q, k, vt, kn, lq1, lk1, lq2, lk2, norm_gain, lambda_init, tq, tk):
    b, s, d = q.shape
    h = vt.shape[1]
    dv = vt.shape[2]
    small = _full(lq1.shape)
    return pl.pallas_call(
        functools.partial(_diff_attn_kernel, tk=tk, lambda_init=lambda_init),
        grid=(b, h, s // tq),
        in_specs=[
            pl.BlockSpec((1, tq, dv), lambda i, j, t: (i, t, j)),
            pl.BlockSpec((1, s, dv), lambda i, j, t: (i, 0, j)),
            pl.BlockSpec((1, 1, dv, s), lambda i, j, t: (i, j, 0, 0)),
            pl.BlockSpec((1,) + kn.shape[1:], lambda i, j, t: (i, 0, 0, 0)),
            small, small, small, small, _full(norm_gain.shape),
        ],
        out_specs=pl.BlockSpec((1, tq, dv), lambda i, j, t: (i, t, j)),
        out_shape=jax.ShapeDtypeStruct((b, s, d), BF16),
        scratch_shapes=_attn_scratch(tk, 2 * tq, dv, dv),
        compiler_params=_params("parallel", "parallel", "parallel"),
        name="diff_attn",
    )(q, k, vt, kn, lq1, lk1, lq2, lk2, norm_gain)


def _mix_ffn_kernel(*refs, n_mix, final, th):
    x_ref = refs[0]
    mix_refs = refs[1:1 + n_mix]
    wo_ref, g_ref, wg_ref, wu_ref, wd_ref, gf_ref, out_ref = refs[1 + n_mix:]
    mix = jnp.concatenate([m_ref[...] for m_ref in mix_refs], axis=1)
    x1 = x_ref[...] + jnp.dot(mix, wo_ref[...], preferred_element_type=F32)
    xn = _rms(x1, g_ref[...]).astype(BF16)
    y = x1
    for c in range(wg_ref.shape[1] // th):
        cs = slice(c * th, (c + 1) * th)
        gate = jnp.dot(xn, wg_ref[:, cs], preferred_element_type=F32)
        up = jnp.dot(xn, wu_ref[:, cs], preferred_element_type=F32)
        hid = (gate * jax.nn.sigmoid(gate) * up).astype(BF16)
        y = y + jnp.dot(hid, wd_ref[cs, :], preferred_element_type=F32)
    if final:
        y = _rms(y, gf_ref[...])
    out_ref[...] = y


def _mix_ffn(x, mixes, w_out, gain, wg, wu, wd, final_gain, final, tm, th):
    t, d = x.shape
    assert wg.shape[1] % th == 0 and sum(m.shape[1] for m in mixes) == w_out.shape[0]
    n_mix = len(mixes)
    row = lambda i: (i, 0)
    in_specs = [pl.BlockSpec((tm, d), row)]
    in_specs += [pl.BlockSpec((tm, m.shape[1]), row) for m in mixes]
    in_specs += [_resident(w_out.shape), _full(gain.shape), _resident(wg.shape), _resident(wu.shape),
                 _resident(wd.shape), _full(final_gain.shape)]
    return pl.pallas_call(
        functools.partial(_mix_ffn_kernel, n_mix=n_mix, final=final, th=th),
        grid=(t // tm,),
        in_specs=in_specs,
        out_specs=pl.BlockSpec((tm, d), row),
        out_shape=jax.ShapeDtypeStruct((t, d), F32),
        compiler_params=_params("parallel"),
        name="mix_ffn",
    )(x, *mixes, w_out, gain, wg, wu, wd, final_gain)


def _rope_tables(seq_len):
    dim = MLA_ROPE
    inv = 1.0 / (ROPE_THETA ** (jnp.arange(0, dim, 2, dtype=F32) / dim))
    ang = jnp.arange(seq_len, dtype=F32)[:, None] * inv[None, :]
    cos, sin = jnp.cos(ang), jnp.sin(ang)
    cos_t = jnp.concatenate([cos, cos, cos, cos], axis=1)
    sin_t = jnp.concatenate([-sin, sin, -sin, sin], axis=1)
    return cos_t, sin_t


def _swap_halves(w):
    half = w.shape[-1] // 2
    return jnp.concatenate([w[..., half:], w[..., :half]], axis=-1)


def _pad_heads(w):
    kdim, h, r = w.shape
    return jnp.concatenate([w, jnp.zeros_like(w)], axis=-1).reshape(kdim, h * 2 * r)


def _row(v):
    return v.reshape(1, -1).astype(F32)


def _tiles(b, s):
    assert s % 512 == 0
    tm_proj = 512
    tm_ffn = 1024 if (b * s) % 1024 == 0 else 512
    th_ffn = 256
    return tm_proj, tm_ffn, th_ffn, min(1024, s), 512, min(4096, s)


def kernel(x, norm_attn, norm_ffn, ffn_w_gate, ffn_w_up, ffn_w_down, ab_w_in, hgrn_lower_bound, hgrn_out_norm,
           mla_q_norm, mla_w_uq, mla_kv_norm, mla_w_ukv, ab_w_out, c_w_in, diff_lambda_q1, diff_lambda_k1,
           diff_lambda_q2, diff_lambda_k2, diff_out_norm, c_w_out, final_norm):
    b, s, d = x.shape
    depth = norm_attn.shape[0]
    assert DIFF_DIM == MLA_ROPE and d == DIFF_HEADS * 2 * DIFF_DIM and d == 2 * HGRN_W
    tm_proj, tm_ffn, th_ffn, tq_mla, tq_diff, tk = _tiles(b, s)

    cos_t, sin_t = _rope_tables(s)
    lower_bounds = jnp.cumsum(jax.nn.softmax(hgrn_lower_bound.astype(F32), axis=1), axis=1)

    for layer in range(depth):
        j = layer // 2
        gain = _row(norm_attn[layer])
        if layer % 2 == 0:
            w_in = ab_w_in[j]
            c_h = 5 * HGRN_W
            c_q = c_h + MLA_Q_LORA
            c_kv = c_q + MLA_KV_LORA
            w_kr = w_in[:, c_kv:]
            w_krs = _swap_halves(w_kr)
            wh = w_in[:, :c_h].astype(BF16)
            wm = jnp.concatenate([w_in[:, c_h:c_kv], w_kr, w_kr, w_krs, w_krs], axis=1).astype(BF16)
            wuq = mla_w_uq[j].reshape(MLA_Q_LORA, MLA_HEADS, MLA_NOPE + MLA_ROPE)
            wqn = wuq[..., :MLA_NOPE].reshape(MLA_Q_LORA, MLA_HEADS * MLA_NOPE).astype(BF16)
            wqr = _pad_heads(wuq[..., MLA_NOPE:]).astype(BF16)
            wqs = _pad_heads(_swap_halves(wuq[..., MLA_NOPE:])).astype(BF16)
            wukv = mla_w_ukv[j].reshape(MLA_KV_LORA, MLA_HEADS, MLA_NOPE + MLA_V)
            wkv = jnp.concatenate([wukv[..., :MLA_NOPE].reshape(MLA_KV_LORA, -1),
                                   wukv[..., MLA_NOPE:].reshape(MLA_KV_LORA, -1)], axis=1).astype(BF16)
            zh, q, k, vt, kn = _proj0(x, gain, wh, wm, _row(mla_q_norm[j]), wqn, wqr, wqs,
                                      _row(mla_kv_norm[j]), wkv, cos_t, sin_t, tm_proj)
            o_a = _hgrn(zh, _row(lower_bounds[0, j]), _row(lower_bounds[1, j]), _row(hgrn_out_norm[j]))
            o_b = _mla_attn(q, k, vt, kn, tq_mla, tk)
            w_out = ab_w_out[j].astype(BF16)
            mixes = [o_a.reshape(b * s, HGRN_W), o_b.reshape(b * s, MLA_HEADS * MLA_V)]
        else:
            lambda_init = 0.8 - 0.6 * math.exp(-0.3 * layer)
            q, k, vt, kn = _proj1(x, gain, c_w_in[j].astype(BF16), cos_t, sin_t, tm_proj)
            o_c = _diff_attn(q, k, vt, kn, _row(diff_lambda_q1[j]), _row(diff_lambda_k1[j]),
                             _row(diff_lambda_q2[j]), _row(diff_lambda_k2[j]), _row(diff_out_norm[j]),
                             lambda_init, tq_diff, tk)
            mixes = [o_c.reshape(b * s, d)]
            w_out = c_w_out[j].astype(BF16)
        x = _mix_ffn(x.reshape(b * s, d), mixes, w_out, _row(norm_ffn[layer]),
                     ffn_w_gate[layer].astype(BF16), ffn_w_up[layer].astype(BF16),
                     ffn_w_down[layer].astype(BF16), _row(final_norm),
                     layer == depth - 1, tm_ffn, th_ffn).reshape(b, s, d)
    return x
```

```python
import functools
import math

import jax
import jax.numpy as jnp
import numpy as np
from jax import lax
from jax.experimental import pallas as pl
from jax.experimental.pallas import tpu as pltpu

F32 = jnp.float32
BF16 = jnp.bfloat16

NORM_EPS = 1e-6
ROPE_THETA = 10000.0
LANES = 128
NORM_SLAB = (8, LANES)
V7X_VMEM_BYTES = 64 * 1024 * 1024
VMEM_LIMIT = V7X_VMEM_BYTES * 7 // 8

HGRN_HEADS = 4
HGRN_DIM = 128
HGRN_W = HGRN_HEADS * HGRN_DIM
HGRN_CHUNK = 64
HGRN_BLOCK = 256

MLA_HEADS = 4
MLA_NOPE = 128
MLA_ROPE = 64
MLA_V = 128
MLA_Q_LORA = 384
MLA_KV_LORA = 256
MLA_QK_PAD = 256
LOG2_E = math.log2(math.e)
MLA_SCALE = (MLA_NOPE + MLA_ROPE) ** -0.5 * LOG2_E

DIFF_HEADS = 8
DIFF_DIM = 64
DIFF_SCALE = DIFF_DIM ** -0.5 * LOG2_E


def _rms(x, gain):
    ms = jnp.mean(x * x, axis=-1, keepdims=True)
    return x * lax.rsqrt(ms + NORM_EPS) * gain


def _params(*sem):
    return pltpu.CompilerParams(dimension_semantics=sem, vmem_limit_bytes=VMEM_LIMIT)


def _full(shape):
    n = len(shape)
    return pl.BlockSpec(shape, lambda *_: (0,) * n)


def _resident(shape):
    n = len(shape)
    return pl.BlockSpec(shape, lambda *_: (0,) * n, pipeline_mode=pl.Buffered(1))


def _row_sq_norm(k, lane_mask=None):
    k2 = k.astype(F32)
    k2 = k2 * k2
    if lane_mask is not None:
        k2 = jnp.where(lane_mask, k2, 0.0)
    return jnp.sum(k2, axis=1, keepdims=True)


def _tile_max(row_values, shape):
    return jnp.broadcast_to(jnp.max(row_values, axis=0, keepdims=True), shape)


def _proj0_kernel(x_ref, g_ref, wh_ref, wm_ref, qg_ref, wqn_ref, wqr_ref, wqs_ref, kvg_ref, wkv_ref,
                  cos_ref, sin_ref, zh_ref, q_ref, k_ref, vt_ref, kn_ref):
    xn = _rms(x_ref[0], g_ref[...]).astype(BF16)
    zh_ref[0] = jnp.dot(xn, wh_ref[...], preferred_element_type=F32)
    zm = jnp.dot(xn, wm_ref[...], preferred_element_type=F32)
    cos = cos_ref[...]
    sin = sin_ref[...]
    c0 = MLA_Q_LORA
    c1 = c0 + MLA_KV_LORA
    cqn = _rms(zm[:, :c0], qg_ref[...]).astype(BF16)
    ckvn = _rms(zm[:, c0:c1], kvg_ref[...]).astype(BF16)
    k_rope = zm[:, c1:c1 + LANES] * cos + zm[:, c1 + LANES:c1 + 2 * LANES] * sin
    k_rope = jnp.where(lax.broadcasted_iota(jnp.int32, k_rope.shape, 1) < MLA_ROPE, k_rope, 0.0).astype(BF16)
    qn = jnp.dot(cqn, wqn_ref[...], preferred_element_type=F32)
    qr = jnp.dot(cqn, wqr_ref[...], preferred_element_type=F32)
    qs = jnp.dot(cqn, wqs_ref[...], preferred_element_type=F32)
    kv = jnp.dot(ckvn, wkv_ref[...], preferred_element_type=F32)
    hw = MLA_HEADS * MLA_NOPE
    rope_sq = _row_sq_norm(k_rope)
    for h in range(MLA_HEADS):
        hs = slice(h * LANES, (h + 1) * LANES)
        q_ref[0, h, :, 0:LANES] = (qn[:, hs] * MLA_SCALE).astype(BF16)
        q_ref[0, h, :, LANES:2 * LANES] = ((qr[:, hs] * cos + qs[:, hs] * sin) * MLA_SCALE).astype(BF16)
        k_nope = kv[:, hs].astype(BF16)
        k_ref[0, h, :, 0:LANES] = k_nope
        k_ref[0, h, :, LANES:2 * LANES] = k_rope
        kn_ref[0, h, 0] = _tile_max(_row_sq_norm(k_nope) + rope_sq, kn_ref.shape[3:])
        vt_ref[0, h] = kv[:, hw + h * LANES:hw + (h + 1) * LANES].T.astype(BF16)


def _proj0(x, gain, wh, wm, qg, wqn, wqr, wqs, kvg, wkv, cos_t, sin_t, tm):
    b, s, d = x.shape
    grid = (b, s // tm)
    return pl.pallas_call(
        _proj0_kernel,
        grid=grid,
        in_specs=[
            pl.BlockSpec((1, tm, d), lambda i, j: (i, j, 0)),
            _full(gain.shape), _resident(wh.shape), _resident(wm.shape), _full(qg.shape), _resident(wqn.shape),
            _resident(wqr.shape), _resident(wqs.shape), _full(kvg.shape), _resident(wkv.shape),
            pl.BlockSpec((tm, LANES), lambda i, j: (j, 0)),
            pl.BlockSpec((tm, LANES), lambda i, j: (j, 0)),
        ],
        out_specs=[
            pl.BlockSpec((1, tm, wh.shape[1]), lambda i, j: (i, j, 0)),
            pl.BlockSpec((1, MLA_HEADS, tm, MLA_QK_PAD), lambda i, j: (i, 0, j, 0)),
            pl.BlockSpec((1, MLA_HEADS, tm, MLA_QK_PAD), lambda i, j: (i, 0, j, 0)),
            pl.BlockSpec((1, MLA_HEADS, MLA_V, tm), lambda i, j: (i, 0, 0, j)),
            pl.BlockSpec((1, MLA_HEADS, 1) + NORM_SLAB, lambda i, j: (i, 0, j, 0, 0)),
        ],
        out_shape=[
            jax.ShapeDtypeStruct((b, s, wh.shape[1]), F32),
            jax.ShapeDtypeStruct((b, MLA_HEADS, s, MLA_QK_PAD), BF16),
            jax.ShapeDtypeStruct((b, MLA_HEADS, s, MLA_QK_PAD), BF16),
            jax.ShapeDtypeStruct((b, MLA_HEADS, MLA_V, s), BF16),
            jax.ShapeDtypeStruct((b, MLA_HEADS, s // tm) + NORM_SLAB, F32),
        ],
        compiler_params=_params("parallel", "parallel"),
        name="proj0",
    )(x, gain, wh, wm, qg, wqn, wqr, wqs, kvg, wkv, cos_t, sin_t)


def _proj1_kernel(x_ref, g_ref, w_ref, cos_ref, sin_ref, sel_ref, q_ref, k_ref, vt_ref, kn_ref):
    d = x_ref.shape[2]
    xn = _rms(x_ref[0], g_ref[...]).astype(BF16)
    z = jnp.dot(xn, w_ref[...], preferred_element_type=F32)
    cos = cos_ref[...]
    sin = sin_ref[...]
    lane = lax.broadcasted_iota(jnp.int32, cos.shape, 1)
    first_half = (lane % DIFF_DIM) < (DIFF_DIM // 2)
    half = DIFF_DIM // 2

    def rope(t):
        partner = jnp.where(first_half, pltpu.roll(t, LANES - half, axis=1), pltpu.roll(t, half, axis=1))
        return t * cos + partner * sin

    k_sq = []
    for h in range(DIFF_HEADS):
        hs = slice(h * LANES, (h + 1) * LANES)
        q_ref[0, :, hs] = (rope(z[:, hs]) * DIFF_SCALE).astype(BF16)
        k = rope(z[:, d + h * LANES:d + (h + 1) * LANES]).astype(BF16)
        k_ref[0, :, hs] = k
        k_sq.append(k * k)
        vt_ref[0, h] = z[:, 2 * d + h * LANES:2 * d + (h + 1) * LANES].T.astype(BF16)
    row_sq = jnp.dot(jnp.concatenate(k_sq, axis=1), sel_ref[...], preferred_element_type=F32)
    kn_ref[0, 0] = _tile_max(row_sq, NORM_SLAB)


def _proj1(x, gain, w, cos_t, sin_t, tm):
    b, s, d = x.shape
    sel = jnp.asarray(np.arange(d)[:, None] // DIFF_DIM == np.arange(LANES)[None, :], BF16)
    return pl.pallas_call(
        _proj1_kernel,
        grid=(b, s // tm),
        in_specs=[
            pl.BlockSpec((1, tm, d), lambda i, j: (i, j, 0)),
            _full(gain.shape), _resident(w.shape),
            pl.BlockSpec((tm, LANES), lambda i, j: (j, 0)),
            pl.BlockSpec((tm, LANES), lambda i, j: (j, 0)),
            _resident(sel.shape),
        ],
        out_specs=[
            pl.BlockSpec((1, tm, d), lambda i, j: (i, j, 0)),
            pl.BlockSpec((1, tm, d), lambda i, j: (i, j, 0)),
            pl.BlockSpec((1, DIFF_HEADS, 2 * DIFF_DIM, tm), lambda i, j: (i, 0, 0, j)),
            pl.BlockSpec((1, 1) + NORM_SLAB, lambda i, j: (i, j, 0, 0)),
        ],
        out_shape=[
            jax.ShapeDtypeStruct((b, s, d), BF16),
            jax.ShapeDtypeStruct((b, s, d), BF16),
            jax.ShapeDtypeStruct((b, DIFF_HEADS, 2 * DIFF_DIM, s), BF16),
            jax.ShapeDtypeStruct((b, s // tm) + NORM_SLAB, F32),
        ],
        compiler_params=_params("parallel", "parallel"),
        name="proj1",
    )(x, gain, w, cos_t, sin_t, sel)


def _hgrn_pair_masks(rev):
    n = HGRN_CHUNK
    t = np.arange(n)[:, None]
    s = np.arange(n)[None, :]
    masks = [t == s]
    m = 1
    while m < n:
        t_blk, s_blk = t // m, s // m
        if rev:
            masks.append((s_blk - t_blk == 1) & (t_blk % 2 == 0))
        else:
            masks.append((t_blk - s_blk == 1) & (t_blk % 2 == 1))
        m *= 2
    return jnp.asarray(np.stack(masks).astype(np.float32))


def _hgrn_chunk(q, k, v, g, pair_ref, st_ref, rev):
    n = HGRN_CHUNK
    row = lax.broadcasted_iota(jnp.int32, g.shape, 0)
    p_sum = g
    total = g
    levels = [(q.astype(BF16), k.astype(BF16))]
    m = 1
    while m < n:
        levels.append(((q * jnp.exp2(p_sum)).astype(BF16), (k * jnp.exp2(total - p_sum)).astype(BF16)))
        in_right = (row % (2 * m)) >= m
        other = jnp.where(in_right, pltpu.roll(total, m, axis=0), pltpu.roll(total, n - m, axis=0))
        p_sum = p_sum + (jnp.where(in_right, 0.0, other) if rev else jnp.where(in_right, other, 0.0))
        total = total + other
        m *= 2
    q_in = (q * jnp.exp2(p_sum)).astype(BF16)
    k_out = (k * jnp.exp2(total - p_sum)).astype(BF16)
    chunk_decay = jnp.exp2(total[0:1])
    vb = v.astype(BF16)
    nt = (((1,), (1,)), ((), ()))
    tn = (((0,), (0,)), ((), ()))
    outs = []
    for h in range(HGRN_HEADS):
        hs = slice(h * HGRN_DIM, (h + 1) * HGRN_DIM)
        a = None
        for lvl, (qt, kt) in enumerate(levels):
            part = lax.dot_general(qt[:, hs], kt[:, hs], nt, preferred_element_type=F32) * pair_ref[lvl]
            a = part if a is None else a + part
        st = st_ref[h]
        o = jnp.dot(a.astype(BF16), vb[:, hs], preferred_element_type=F32)
        o = o + lax.dot_general(q_in[:, hs], st.astype(BF16), nt, preferred_element_type=F32)
        st_ref[h] = st * chunk_decay[:, hs] + lax.dot_general(vb[:, hs], k_out[:, hs], tn,
                                                               preferred_element_type=F32)
        outs.append(o)
    return outs


def _hgrn_gates(qh, fp, lb):
    q = qh * jax.nn.sigmoid(qh) * (HGRN_DIM ** -0.5)
    g = jnp.log2(lb + (1.0 - lb) * jax.nn.sigmoid(fp))
    k = (1.0 - lb) * jax.nn.sigmoid(-fp)
    return q, k, g


def _hgrn_fwd_kernel(q_ref, f_ref, v_ref, lb_ref, pair_ref, o_ref, st_ref):
    @pl.when(pl.program_id(1) == 0)
    def _():
        st_ref[...] = jnp.zeros_like(st_ref)

    lb = lb_ref[...]
    for c in range(HGRN_BLOCK // HGRN_CHUNK):
        rs = slice(c * HGRN_CHUNK, (c + 1) * HGRN_CHUNK)
        for b in range(q_ref.shape[0]):
            q, k, g = _hgrn_gates(q_ref[b, rs, :], f_ref[b, rs, :], lb)
            outs = _hgrn_chunk(q, k, v_ref[b, rs, :], g, pair_ref, st_ref.at[b], False)
            for h, o in enumerate(outs):
                o_ref[b, rs, h * HGRN_DIM:(h + 1) * HGRN_DIM] = o


def _hgrn_bwd_kernel(q_ref, f_ref, v_ref, gate_ref, of_ref, lb_ref, ng_ref, pair_ref, o_ref, st_ref):
    @pl.when(pl.program_id(1) == 0)
    def _():
        st_ref[...] = jnp.zeros_like(st_ref)

    lb = lb_ref[...]
    for c in reversed(range(HGRN_BLOCK // HGRN_CHUNK)):
        rs = slice(c * HGRN_CHUNK, (c + 1) * HGRN_CHUNK)
        for b in range(q_ref.shape[0]):
            q, k, g = _hgrn_gates(q_ref[b, rs, :], f_ref[b, rs, :], lb)
            outs = _hgrn_chunk(q, k, v_ref[b, rs, :], g, pair_ref, st_ref.at[b], True)
            gate = gate_ref[b, rs, :]
            gate = gate * jax.nn.sigmoid(gate)
            for h, o in enumerate(outs):
                hs = slice(h * HGRN_DIM, (h + 1) * HGRN_DIM)
                y = _rms(o + of_ref[b, rs, hs], ng_ref[:, hs])
                o_ref[b, rs, hs] = (y * gate[:, hs]).astype(BF16)


def _hgrn(zh, lb_fwd, lb_bwd, norm_gain):
    b, s, _ = zh.shape
    nb = s // HGRN_BLOCK
    group = 2 if b % 2 == 0 else 1
    state = pltpu.VMEM((group, HGRN_HEADS, HGRN_DIM, HGRN_DIM), F32)

    def col(c, rev):
        if rev:
            return pl.BlockSpec((group, HGRN_BLOCK, HGRN_W), lambda i, j: (i, nb - 1 - j, c))
        return pl.BlockSpec((group, HGRN_BLOCK, HGRN_W), lambda i, j: (i, j, c))

    pair_fwd, pair_bwd = _hgrn_pair_masks(False), _hgrn_pair_masks(True)
    o_fwd = pl.pallas_call(
        _hgrn_fwd_kernel,
        grid=(b // group, nb),
        in_specs=[col(0, False), col(1, False), col(3, False), _full(lb_fwd.shape), _full(pair_fwd.shape)],
        out_specs=col(0, False),
        out_shape=jax.ShapeDtypeStruct((b, s, HGRN_W), F32),
        scratch_shapes=[state],
        compiler_params=_params("parallel", "arbitrary"),
        name="hgrn_fwd",
    )(zh, zh, zh, lb_fwd, pair_fwd)
    return pl.pallas_call(
        _hgrn_bwd_kernel,
        grid=(b // group, nb),
        in_specs=[col(0, True), col(2, True), col(3, True), col(4, True), col(0, True),
                  _full(lb_bwd.shape), _full(norm_gain.shape), _full(pair_bwd.shape)],
        out_specs=col(0, True),
        out_shape=jax.ShapeDtypeStruct((b, s, HGRN_W), BF16),
        scratch_shapes=[state],
        compiler_params=_params("parallel", "arbitrary"),
        name="hgrn_bwd",
    )(zh, zh, zh, zh, o_fwd, lb_bwd, norm_gain, pair_bwd)


MIN_FAST_ROW_SUM = 2.0 ** -60
ONLINE_KEY_CHUNK = 256


def _fast_sweep(k_slice, vt_slice, qst_ref, shift, l8_ref, acc_ref, p_bufs, n_kv, tk):
    w = qst_ref.shape[1]
    l8_ref[...] = jnp.zeros_like(l8_ref)
    acc_ref[...] = jnp.zeros_like(acc_ref)

    def probs(j, slot):
        s = jnp.dot(k_slice(j * tk, tk), qst_ref[...], preferred_element_type=F32)
        p = jnp.exp2(s - shift)
        l8_ref[...] += jnp.sum(p.reshape(tk // 8, 8, w), axis=0)
        p_bufs[slot][...] = p.astype(BF16)

    def weighted(j, slot):
        acc_ref[...] += jnp.dot(vt_slice(j * tk, tk), p_bufs[slot][...], preferred_element_type=F32)

    probs(0, 0)
    for j in range(1, n_kv):
        weighted(j - 1, (j - 1) % 2)
        probs(j, j % 2)
    weighted(n_kv - 1, (n_kv - 1) % 2)


def _online_sweep(k_slice, vt_slice, qst_ref, m_ref, l_ref, acc_ref, n_kv, tk):
    m_ref[...] = jnp.full(m_ref.shape, -jnp.inf, F32)
    l_ref[...] = jnp.zeros_like(l_ref)
    acc_ref[...] = jnp.zeros_like(acc_ref)

    def body(j, carry):
        off = pl.multiple_of(j * tk, tk)
        s = jnp.dot(k_slice(off, tk), qst_ref[...], preferred_element_type=F32)
        m_old = m_ref[...]
        m_new = jnp.maximum(m_old, jnp.max(s, axis=0, keepdims=True))
        alpha = jnp.exp2(m_old - m_new)
        p = jnp.exp2(s - m_new)
        l_ref[...] = alpha * l_ref[...] + jnp.sum(p, axis=0, keepdims=True)
        acc_ref[...] = alpha * acc_ref[...] + jnp.dot(vt_slice(off, tk), p.astype(BF16),
                                                      preferred_element_type=F32)
        m_ref[...] = m_new
        return carry

    lax.fori_loop(0, n_kv, body, 0)


def _finish_sweep(k_slice, vt_slice, qst_ref, m_ref, l_ref, l8_ref, acc_ref, n_keys, tk):
    row_sum = jnp.sum(l8_ref[...], axis=0, keepdims=True)
    l_ref[...] = row_sum
    n_bad = jnp.sum(jnp.where(row_sum >= MIN_FAST_ROW_SUM, 0.0, 1.0))

    @pl.when(n_bad > 0.0)
    def _():
        tk_online = min(tk, ONLINE_KEY_CHUNK)
        _online_sweep(k_slice, vt_slice, qst_ref, m_ref, l_ref, acc_ref, n_keys // tk_online, tk_online)


SUB_TILES = 2


def _attn_scratch(tk, w, d, dv):
    n = SUB_TILES
    return [pltpu.VMEM((n, d, w), BF16), pltpu.VMEM((1, w), F32), pltpu.VMEM((n, 1, w), F32),
            pltpu.VMEM((n, 8, w), F32), pltpu.VMEM((n, dv, w), F32), pltpu.VMEM((tk, w), BF16),
            pltpu.VMEM((tk, w), BF16)]


def _mla_attn_kernel(q_ref, k_ref, vt_ref, kn_ref, o_ref, qst_ref, m_ref, l_ref, l8_ref, acc_ref, p0, p1,
                     *, tq, tk):
    max_key_norm = jnp.sqrt(jnp.max(kn_ref[0, 0], axis=0))
    k_slice = lambda off, n: k_ref[0, 0, pl.ds(off, n), :]
    vt_slice = lambda off, n: vt_ref[0, 0, :, pl.ds(off, n)]
    n_keys = k_ref.shape[2]
    for i in range(SUB_TILES):
        rows = slice(i * tq, (i + 1) * tq)
        qst, l, l8, acc = qst_ref.at[i], l_ref.at[i], l8_ref.at[i], acc_ref.at[i]
        qt = q_ref[0, 0, rows, :].astype(F32).T
        qst[...] = qt.astype(BF16)
        shift = jnp.sqrt(jnp.sum(qt * qt, axis=0, keepdims=True)) * max_key_norm[0:1, 0:1]
        _fast_sweep(k_slice, vt_slice, qst, shift, l8, acc, (p0, p1), n_keys // tk, tk)
        _finish_sweep(k_slice, vt_slice, qst, m_ref, l, l8, acc, n_keys, tk)
        out_t = acc[...] / l[...]
        o_ref[0, rows, :] = out_t.T.astype(BF16)


def _mla_attn(q, k, vt, kn, tq, tk):
    b, h, s, d = q.shape
    dv = vt.shape[2]
    rows = SUB_TILES * tq
    return pl.pallas_call(
        functools.partial(_mla_attn_kernel, tq=tq, tk=tk),
        grid=(b, h, s // rows),
        in_specs=[
            pl.BlockSpec((1, 1, rows, d), lambda i, j, t: (i, j, t, 0)),
            pl.BlockSpec((1, 1, s, d), lambda i, j, t: (i, j, 0, 0)),
            pl.BlockSpec((1, 1, dv, s), lambda i, j, t: (i, j, 0, 0)),
            pl.BlockSpec((1, 1) + kn.shape[2:], lambda i, j, t: (i, j, 0, 0, 0)),
        ],
        out_specs=pl.BlockSpec((1, rows, dv), lambda i, j, t: (i, t, j)),
        out_shape=jax.ShapeDtypeStruct((b, s, h * dv), BF16),
        scratch_shapes=_attn_scratch(tk, tq, d, dv),
        compiler_params=_params("parallel", "parallel", "parallel"),
        name="mla_attn",
    )(q, k, vt, kn)


def _diff_attn_kernel(q_ref, k_ref, vt_ref, kn_ref, lq1_ref, lk1_ref, lq2_ref, lk2_ref, ng_ref, o_ref,
                      qst_ref, m_ref, l_ref, l8_ref, acc_ref, p0, p1, *, tq, tk, lambda_init):
    key_sq = jnp.max(kn_ref[0], axis=0)
    chunk = lax.broadcasted_iota(jnp.int32, key_sq.shape, 1) - 2 * pl.program_id(1)
    max_key_norm = [jnp.sqrt(jnp.max(jnp.where(chunk == c, key_sq, 0.0), axis=1, keepdims=True))[0:1]
                    for c in range(2)]

    lam = (jnp.exp(jnp.sum(lq1_ref[...] * lk1_ref[...], axis=1, keepdims=True))
           - jnp.exp(jnp.sum(lq2_ref[...] * lk2_ref[...], axis=1, keepdims=True)) + lambda_init)
    k_slice = lambda off, n: k_ref[0, pl.ds(off, n), :]
    vt_slice = lambda off, n: vt_ref[0, 0, :, pl.ds(off, n)]
    n_keys = k_ref.shape[1]
    for i in range(SUB_TILES):
        rows = slice(i * tq, (i + 1) * tq)
        qst, l, l8, acc = qst_ref.at[i], l_ref.at[i], l8_ref.at[i], acc_ref.at[i]
        q = q_ref[0, rows, :].astype(F32)
        first = lax.broadcasted_iota(jnp.int32, q.shape, 1) < DIFF_DIM
        q1t = jnp.where(first, q, 0.0).T
        q2t = jnp.where(first, 0.0, q).T
        qst[:, 0:tq] = q1t.astype(BF16)
        qst[:, tq:2 * tq] = q2t.astype(BF16)
        shift = jnp.concatenate([jnp.sqrt(jnp.sum(q1t * q1t, axis=0, keepdims=True)) * max_key_norm[0],
                                 jnp.sqrt(jnp.sum(q2t * q2t, axis=0, keepdims=True)) * max_key_norm[1]],
                                axis=1)
        _fast_sweep(k_slice, vt_slice, qst, shift, l8, acc, (p0, p1), n_keys // tk, tk)
        _finish_sweep(k_slice, vt_slice, qst, m_ref, l, l8, acc, n_keys, tk)
        soft = acc[...] / l[...]
        out_t = soft[:, 0:tq] - lam * soft[:, tq:2 * tq]
        y = _rms(out_t.T, ng_ref[...]) * (1.0 - lambda_init)
        o_ref[0, rows, :] = y.astype(BF16)


def _diff_attn(q, k, vt, kn, lq1, lk1, lq2, lk2, norm_gain, lambda_init, tq, tk):
    b, s, d = q.shape
    h = vt.shape[1]
    dv = vt.shape[2]
    small = _full(lq1.shape)
    rows = SUB_TILES * tq
    return pl.pallas_call(
        functools.partial(_diff_attn_kernel, tq=tq, tk=tk, lambda_init=lambda_init),
        grid=(b, h, s // rows),
        in_specs=[
            pl.BlockSpec((1, rows, dv), lambda i, j, t: (i, t, j)),
            pl.BlockSpec((1, s, dv), lambda i, j, t: (i, 0, j)),
            pl.BlockSpec((1, 1, dv, s), lambda i, j, t: (i, j, 0, 0)),
            pl.BlockSpec((1,) + kn.shape[1:], lambda i, j, t: (i, 0, 0, 0)),
            small, small, small, small, _full(norm_gain.shape),
        ],
        out_specs=pl.BlockSpec((1, rows, dv), lambda i, j, t: (i, t, j)),
        out_shape=jax.ShapeDtypeStruct((b, s, d), BF16),
        scratch_shapes=_attn_scratch(tk, 2 * tq, dv, dv),
        compiler_params=_params("parallel", "parallel", "parallel"),
        name="diff_attn",
    )(q, k, vt, kn, lq1, lk1, lq2, lk2, norm_gain)


def _mix_ffn_kernel(*refs, n_mix, final, th):
    x_ref = refs[0]
    mix_refs = refs[1:1 + n_mix]
    wo_ref, g_ref, wg_ref, wu_ref, wd_ref, gf_ref, out_ref = refs[1 + n_mix:]
    mix = jnp.concatenate([m_ref[...] for m_ref in mix_refs], axis=1)
    x1 = x_ref[...] + jnp.dot(mix, wo_ref[...], preferred_element_type=F32)
    xn = _rms(x1, g_ref[...]).astype(BF16)
    y = x1
    for c in range(wg_ref.shape[1] // th):
        cs = slice(c * th, (c + 1) * th)
        gate = jnp.dot(xn, wg_ref[:, cs], preferred_element_type=F32)
        up = jnp.dot(xn, wu_ref[:, cs], preferred_element_type=F32)
        hid = (gate * jax.nn.sigmoid(gate) * up).astype(BF16)
        y = y + jnp.dot(hid, wd_ref[cs, :], preferred_element_type=F32)
    if final:
        y = _rms(y, gf_ref[...])
    out_ref[...] = y


def _mix_ffn(x, mixes, w_out, gain, wg, wu, wd, final_gain, final, tm, th):
    t, d = x.shape
    assert wg.shape[1] % th == 0 and sum(m.shape[1] for m in mixes) == w_out.shape[0]
    n_mix = len(mixes)
    row = lambda i: (i, 0)
    in_specs = [pl.BlockSpec((tm, d), row)]
    in_specs += [pl.BlockSpec((tm, m.shape[1]), row) for m in mixes]
    in_specs += [_resident(w_out.shape), _full(gain.shape), _resident(wg.shape), _resident(wu.shape),
                 _resident(wd.shape), _full(final_gain.shape)]
    return pl.pallas_call(
        functools.partial(_mix_ffn_kernel, n_mix=n_mix, final=final, th=th),
        grid=(t // tm,),
        in_specs=in_specs,
        out_specs=pl.BlockSpec((tm, d), row),
        out_shape=jax.ShapeDtypeStruct((t, d), F32),
        compiler_params=_params("parallel"),
        name="mix_ffn",
    )(x, *mixes, w_out, gain, wg, wu, wd, final_gain)


def _rope_tables(seq_len):
    dim = MLA_ROPE
    inv = 1.0 / (ROPE_THETA ** (jnp.arange(0, dim, 2, dtype=F32) / dim))
    ang = jnp.arange(seq_len, dtype=F32)[:, None] * inv[None, :]
    cos, sin = jnp.cos(ang), jnp.sin(ang)
    cos_t = jnp.concatenate([cos, cos, cos, cos], axis=1)
    sin_t = jnp.concatenate([-sin, sin, -sin, sin], axis=1)
    return cos_t, sin_t


def _swap_halves(w):
    half = w.shape[-1] // 2
    return jnp.concatenate([w[..., half:], w[..., :half]], axis=-1)


def _pad_heads(w):
    kdim, h, r = w.shape
    return jnp.concatenate([w, jnp.zeros_like(w)], axis=-1).reshape(kdim, h * 2 * r)


def _row(v):
    return v.reshape(1, -1).astype(F32)


def _tiles(b, s):
    assert s % 512 == 0
    tm_proj = 512
    tm_ffn = 1024 if (b * s) % 1024 == 0 else 512
    th_ffn = 256
    return tm_proj, tm_ffn, th_ffn, min(1024, s // SUB_TILES), min(512, s // SUB_TILES), min(4096, s)


def kernel(x, norm_attn, norm_ffn, ffn_w_gate, ffn_w_up, ffn_w_down, ab_w_in, hgrn_lower_bound, hgrn_out_norm,
           mla_q_norm, mla_w_uq, mla_kv_norm, mla_w_ukv, ab_w_out, c_w_in, diff_lambda_q1, diff_lambda_k1,
           diff_lambda_q2, diff_lambda_k2, diff_out_norm, c_w_out, final_norm):
    b, s, d = x.shape
    depth = norm_attn.shape[0]
    assert DIFF_DIM == MLA_ROPE and d == DIFF_HEADS * 2 * DIFF_DIM and d == 2 * HGRN_W
    tm_proj, tm_ffn, th_ffn, tq_mla, tq_diff, tk = _tiles(b, s)

    cos_t, sin_t = _rope_tables(s)
    lower_bounds = jnp.cumsum(jax.nn.softmax(hgrn_lower_bound.astype(F32), axis=1), axis=1)

    for layer in range(depth):
        j = layer // 2
        gain = _row(norm_attn[layer])
        if layer % 2 == 0:
            w_in = ab_w_in[j]
            c_h = 5 * HGRN_W
            c_q = c_h + MLA_Q_LORA
            c_kv = c_q + MLA_KV_LORA
            w_kr = w_in[:, c_kv:]
            w_krs = _swap_halves(w_kr)
            wh = w_in[:, :c_h].astype(BF16)
            wm = jnp.concatenate([w_in[:, c_h:c_kv], w_kr, w_kr, w_krs, w_krs], axis=1).astype(BF16)
            wuq = mla_w_uq[j].reshape(MLA_Q_LORA, MLA_HEADS, MLA_NOPE + MLA_ROPE)
            wqn = wuq[..., :MLA_NOPE].reshape(MLA_Q_LORA, MLA_HEADS * MLA_NOPE).astype(BF16)
            wqr = _pad_heads(wuq[..., MLA_NOPE:]).astype(BF16)
            wqs = _pad_heads(_swap_halves(wuq[..., MLA_NOPE:])).astype(BF16)
            wukv = mla_w_ukv[j].reshape(MLA_KV_LORA, MLA_HEADS, MLA_NOPE + MLA_V)
            wkv = jnp.concatenate([wukv[..., :MLA_NOPE].reshape(MLA_KV_LORA, -1),
                                   wukv[..., MLA_NOPE:].reshape(MLA_KV_LORA, -1)], axis=1).astype(BF16)
            zh, q, k, vt, kn = _proj0(x, gain, wh, wm, _row(mla_q_norm[j]), wqn, wqr, wqs,
                                      _row(mla_kv_norm[j]), wkv, cos_t, sin_t, tm_proj)
            o_a = _hgrn(zh, _row(lower_bounds[0, j]), _row(lower_bounds[1, j]), _row(hgrn_out_norm[j]))
            o_b = _mla_attn(q, k, vt, kn, tq_mla, tk)
            w_out = ab_w_out[j].astype(BF16)
            mixes = [o_a.reshape(b * s, HGRN_W), o_b.reshape(b * s, MLA_HEADS * MLA_V)]
        else:
            lambda_init = 0.8 - 0.6 * math.exp(-0.3 * layer)
            q, k, vt, kn = _proj1(x, gain, c_w_in[j].astype(BF16), cos_t, sin_t, tm_proj)
            o_c = _diff_attn(q, k, vt, kn, _row(diff_lambda_q1[j]), _row(diff_lambda_k1[j]),
                             _row(diff_lambda_q2[j]), _row(diff_lambda_k2[j]), _row(diff_out_norm[j]),
                             lambda_init, tq_diff, tk)
            mixes = [o_c.reshape(b * s, d)]
            w_out = c_w_out[j].astype(BF16)
        x = _mix_ffn(x.reshape(b * s, d), mixes, w_out, _row(norm_ffn[layer]),
                     ffn_w_gate[layer].astype(BF16), ffn_w_up[layer].astype(BF16),
                     ffn_w_down[layer].astype(BF16), _row(final_norm),
                     layer == depth - 1, tm_ffn, th_ffn).reshape(b, s, d)
    return x
```

```python
import functools
import math

import jax
import jax.numpy as jnp
import numpy as np
from jax import lax
from jax.experimental import pallas as pl
from jax.experimental.pallas import tpu as pltpu

F32 = jnp.float32
BF16 = jnp.bfloat16

NORM_EPS = 1e-6
ROPE_THETA = 10000.0
LANES = 128
NORM_SLAB = (8, LANES)
V7X_VMEM_BYTES = 64 * 1024 * 1024
VMEM_LIMIT = V7X_VMEM_BYTES * 7 // 8

HGRN_HEADS = 4
HGRN_DIM = 128
HGRN_W = HGRN_HEADS * HGRN_DIM
HGRN_CHUNK = 64
HGRN_BLOCK = 256

MLA_HEADS = 4
MLA_NOPE = 128
MLA_ROPE = 64
MLA_V = 128
MLA_Q_LORA = 384
MLA_KV_LORA = 256
MLA_QK_PAD = 256
LOG2_E = math.log2(math.e)
MLA_SCALE = (MLA_NOPE + MLA_ROPE) ** -0.5 * LOG2_E

DIFF_HEADS = 8
DIFF_DIM = 64
DIFF_SCALE = DIFF_DIM ** -0.5 * LOG2_E


def _rms(x, gain):
    ms = jnp.mean(x * x, axis=-1, keepdims=True)
    return x * lax.rsqrt(ms + NORM_EPS) * gain


def _params(*sem):
    return pltpu.CompilerParams(dimension_semantics=sem, vmem_limit_bytes=VMEM_LIMIT)


def _full(shape):
    n = len(shape)
    return pl.BlockSpec(shape, lambda *_: (0,) * n)


def _resident(shape):
    n = len(shape)
    return pl.BlockSpec(shape, lambda *_: (0,) * n, pipeline_mode=pl.Buffered(1))


def _row_sq_norm(k, lane_mask=None):
    k2 = k.astype(F32)
    k2 = k2 * k2
    if lane_mask is not None:
        k2 = jnp.where(lane_mask, k2, 0.0)
    return jnp.sum(k2, axis=1, keepdims=True)


def _tile_max(row_values, shape):
    return jnp.broadcast_to(jnp.max(row_values, axis=0, keepdims=True), shape)


def _proj0_kernel(x_ref, g_ref, wh_ref, wm_ref, qg_ref, wqn_ref, wqr_ref, wqs_ref, kvg_ref, wkv_ref,
                  cos_ref, sin_ref, zh_ref, q_ref, k_ref, vt_ref, kn_ref):
    xn = _rms(x_ref[0], g_ref[...]).astype(BF16)
    zh_ref[0] = jnp.dot(xn, wh_ref[...], preferred_element_type=F32)
    zm = jnp.dot(xn, wm_ref[...], preferred_element_type=F32)
    cos = cos_ref[...]
    sin = sin_ref[...]
    c0 = MLA_Q_LORA
    c1 = c0 + MLA_KV_LORA
    cqn = _rms(zm[:, :c0], qg_ref[...]).astype(BF16)
    ckvn = _rms(zm[:, c0:c1], kvg_ref[...]).astype(BF16)
    k_rope = zm[:, c1:c1 + LANES] * cos + zm[:, c1 + LANES:c1 + 2 * LANES] * sin
    k_rope = jnp.where(lax.broadcasted_iota(jnp.int32, k_rope.shape, 1) < MLA_ROPE, k_rope, 0.0).astype(BF16)
    qn = jnp.dot(cqn, wqn_ref[...], preferred_element_type=F32)
    qr = jnp.dot(cqn, wqr_ref[...], preferred_element_type=F32)
    qs = jnp.dot(cqn, wqs_ref[...], preferred_element_type=F32)
    kv = jnp.dot(ckvn, wkv_ref[...], preferred_element_type=F32)
    hw = MLA_HEADS * MLA_NOPE
    rope_sq = _row_sq_norm(k_rope)
    for h in range(MLA_HEADS):
        hs = slice(h * LANES, (h + 1) * LANES)
        q_ref[0, h, :, 0:LANES] = (qn[:, hs] * MLA_SCALE).astype(BF16)
        q_ref[0, h, :, LANES:2 * LANES] = ((qr[:, hs] * cos + qs[:, hs] * sin) * MLA_SCALE).astype(BF16)
        k_nope = kv[:, hs].astype(BF16)
        k_ref[0, h, :, 0:LANES] = k_nope
        k_ref[0, h, :, LANES:2 * LANES] = k_rope
        kn_ref[0, h, 0] = _tile_max(_row_sq_norm(k_nope) + rope_sq, kn_ref.shape[3:])
        vt_ref[0, h] = kv[:, hw + h * LANES:hw + (h + 1) * LANES].T.astype(BF16)


def _proj0(x, gain, wh, wm, qg, wqn, wqr, wqs, kvg, wkv, cos_t, sin_t, tm):
    b, s, d = x.shape
    grid = (b, s // tm)
    return pl.pallas_call(
        _proj0_kernel,
        grid=grid,
        in_specs=[
            pl.BlockSpec((1, tm, d), lambda i, j: (i, j, 0)),
            _full(gain.shape), _resident(wh.shape), _resident(wm.shape), _full(qg.shape), _resident(wqn.shape),
            _resident(wqr.shape), _resident(wqs.shape), _full(kvg.shape), _resident(wkv.shape),
            pl.BlockSpec((tm, LANES), lambda i, j: (j, 0)),
            pl.BlockSpec((tm, LANES), lambda i, j: (j, 0)),
        ],
        out_specs=[
            pl.BlockSpec((1, tm, wh.shape[1]), lambda i, j: (i, j, 0)),
            pl.BlockSpec((1, MLA_HEADS, tm, MLA_QK_PAD), lambda i, j: (i, 0, j, 0)),
            pl.BlockSpec((1, MLA_HEADS, tm, MLA_QK_PAD), lambda i, j: (i, 0, j, 0)),
            pl.BlockSpec((1, MLA_HEADS, MLA_V, tm), lambda i, j: (i, 0, 0, j)),
            pl.BlockSpec((1, MLA_HEADS, 1) + NORM_SLAB, lambda i, j: (i, 0, j, 0, 0)),
        ],
        out_shape=[
            jax.ShapeDtypeStruct((b, s, wh.shape[1]), F32),
            jax.ShapeDtypeStruct((b, MLA_HEADS, s, MLA_QK_PAD), BF16),
            jax.ShapeDtypeStruct((b, MLA_HEADS, s, MLA_QK_PAD), BF16),
            jax.ShapeDtypeStruct((b, MLA_HEADS, MLA_V, s), BF16),
            jax.ShapeDtypeStruct((b, MLA_HEADS, s // tm) + NORM_SLAB, F32),
        ],
        compiler_params=_params("parallel", "parallel"),
        name="proj0",
    )(x, gain, wh, wm, qg, wqn, wqr, wqs, kvg, wkv, cos_t, sin_t)


def _proj1_kernel(x_ref, g_ref, w_ref, cos_ref, sin_ref, sel_ref, q_ref, k_ref, vt_ref, kn_ref):
    d = x_ref.shape[2]
    xn = _rms(x_ref[0], g_ref[...]).astype(BF16)
    z = jnp.dot(xn, w_ref[...], preferred_element_type=F32)
    cos = cos_ref[...]
    sin = sin_ref[...]
    lane = lax.broadcasted_iota(jnp.int32, cos.shape, 1)
    first_half = (lane % DIFF_DIM) < (DIFF_DIM // 2)
    half = DIFF_DIM // 2

    def rope(t):
        partner = jnp.where(first_half, pltpu.roll(t, LANES - half, axis=1), pltpu.roll(t, half, axis=1))
        return t * cos + partner * sin

    k_sq = []
    for h in range(DIFF_HEADS):
        hs = slice(h * LANES, (h + 1) * LANES)
        q_ref[0, :, hs] = (rope(z[:, hs]) * DIFF_SCALE).astype(BF16)
        k = rope(z[:, d + h * LANES:d + (h + 1) * LANES]).astype(BF16)
        k_ref[0, :, hs] = k
        k_sq.append(k * k)
        vt_ref[0, h] = z[:, 2 * d + h * LANES:2 * d + (h + 1) * LANES].T.astype(BF16)
    row_sq = jnp.dot(jnp.concatenate(k_sq, axis=1), sel_ref[...], preferred_element_type=F32)
    kn_ref[0, 0] = _tile_max(row_sq, NORM_SLAB)


def _proj1(x, gain, w, cos_t, sin_t, tm):
    b, s, d = x.shape
    sel = jnp.asarray(np.arange(d)[:, None] // DIFF_DIM == np.arange(LANES)[None, :], BF16)
    return pl.pallas_call(
        _proj1_kernel,
        grid=(b, s // tm),
        in_specs=[
            pl.BlockSpec((1, tm, d), lambda i, j: (i, j, 0)),
            _full(gain.shape), _resident(w.shape),
            pl.BlockSpec((tm, LANES), lambda i, j: (j, 0)),
            pl.BlockSpec((tm, LANES), lambda i, j: (j, 0)),
            _resident(sel.shape),
        ],
        out_specs=[
            pl.BlockSpec((1, tm, d), lambda i, j: (i, j, 0)),
            pl.BlockSpec((1, tm, d), lambda i, j: (i, j, 0)),
            pl.BlockSpec((1, DIFF_HEADS, 2 * DIFF_DIM, tm), lambda i, j: (i, 0, 0, j)),
            pl.BlockSpec((1, 1) + NORM_SLAB, lambda i, j: (i, j, 0, 0)),
        ],
        out_shape=[
            jax.ShapeDtypeStruct((b, s, d), BF16),
            jax.ShapeDtypeStruct((b, s, d), BF16),
            jax.ShapeDtypeStruct((b, DIFF_HEADS, 2 * DIFF_DIM, s), BF16),
            jax.ShapeDtypeStruct((b, s // tm) + NORM_SLAB, F32),
        ],
        compiler_params=_params("parallel", "parallel"),
        name="proj1",
    )(x, gain, w, cos_t, sin_t, sel)


def _hgrn_pair_masks(rev):
    n = HGRN_CHUNK
    t = np.arange(n)[:, None]
    s = np.arange(n)[None, :]
    masks = [t == s]
    m = 1
    while m < n:
        t_blk, s_blk = t // m, s // m
        if rev:
            masks.append((s_blk - t_blk == 1) & (t_blk % 2 == 0))
        else:
            masks.append((t_blk - s_blk == 1) & (t_blk % 2 == 1))
        m *= 2
    return jnp.asarray(np.stack(masks).astype(np.float32))


def _hgrn_chunk(q, k, v, g, pair_ref, st_ref, rev):
    n = HGRN_CHUNK
    row = lax.broadcasted_iota(jnp.int32, g.shape, 0)
    p_sum = g
    total = g
    levels = [(q.astype(BF16), k.astype(BF16))]
    m = 1
    while m < n:
        levels.append(((q * jnp.exp2(p_sum)).astype(BF16), (k * jnp.exp2(total - p_sum)).astype(BF16)))
        in_right = (row % (2 * m)) >= m
        other = jnp.where(in_right, pltpu.roll(total, m, axis=0), pltpu.roll(total, n - m, axis=0))
        p_sum = p_sum + (jnp.where(in_right, 0.0, other) if rev else jnp.where(in_right, other, 0.0))
        total = total + other
        m *= 2
    q_in = (q * jnp.exp2(p_sum)).astype(BF16)
    k_out = (k * jnp.exp2(total - p_sum)).astype(BF16)
    chunk_decay = jnp.exp2(total[0:1])
    vb = v.astype(BF16)
    nt = (((1,), (1,)), ((), ()))
    tn = (((0,), (0,)), ((), ()))
    outs = []
    for h in range(HGRN_HEADS):
        hs = slice(h * HGRN_DIM, (h + 1) * HGRN_DIM)
        a = None
        for lvl, (qt, kt) in enumerate(levels):
            part = lax.dot_general(qt[:, hs], kt[:, hs], nt, preferred_element_type=F32) * pair_ref[lvl]
            a = part if a is None else a + part
        st = st_ref[h]
        o = jnp.dot(a.astype(BF16), vb[:, hs], preferred_element_type=F32)
        o = o + lax.dot_general(q_in[:, hs], st.astype(BF16), nt, preferred_element_type=F32)
        st_ref[h] = st * chunk_decay[:, hs] + lax.dot_general(vb[:, hs], k_out[:, hs], tn,
                                                               preferred_element_type=F32)
        outs.append(o)
    return outs


def _hgrn_gates(qh, fp, lb):
    q = qh * jax.nn.sigmoid(qh) * (HGRN_DIM ** -0.5)
    g = jnp.log2(lb + (1.0 - lb) * jax.nn.sigmoid(fp))
    k = (1.0 - lb) * jax.nn.sigmoid(-fp)
    return q, k, g


def _hgrn_fwd_kernel(q_ref, f_ref, v_ref, lb_ref, pair_ref, o_ref, st_ref):
    @pl.when(pl.program_id(1) == 0)
    def _():
        st_ref[...] = jnp.zeros_like(st_ref)

    lb = lb_ref[...]
    for c in range(HGRN_BLOCK // HGRN_CHUNK):
        rs = slice(c * HGRN_CHUNK, (c + 1) * HGRN_CHUNK)
        for b in range(q_ref.shape[0]):
            q, k, g = _hgrn_gates(q_ref[b, rs, :], f_ref[b, rs, :], lb)
            outs = _hgrn_chunk(q, k, v_ref[b, rs, :], g, pair_ref, st_ref.at[b], False)
            for h, o in enumerate(outs):
                o_ref[b, rs, h * HGRN_DIM:(h + 1) * HGRN_DIM] = o


def _hgrn_bwd_kernel(q_ref, f_ref, v_ref, gate_ref, of_ref, lb_ref, ng_ref, pair_ref, o_ref, st_ref):
    @pl.when(pl.program_id(1) == 0)
    def _():
        st_ref[...] = jnp.zeros_like(st_ref)

    lb = lb_ref[...]
    for c in reversed(range(HGRN_BLOCK // HGRN_CHUNK)):
        rs = slice(c * HGRN_CHUNK, (c + 1) * HGRN_CHUNK)
        for b in range(q_ref.shape[0]):
            q, k, g = _hgrn_gates(q_ref[b, rs, :], f_ref[b, rs, :], lb)
            outs = _hgrn_chunk(q, k, v_ref[b, rs, :], g, pair_ref, st_ref.at[b], True)
            gate = gate_ref[b, rs, :]
            gate = gate * jax.nn.sigmoid(gate)
            for h, o in enumerate(outs):
                hs = slice(h * HGRN_DIM, (h + 1) * HGRN_DIM)
                y = _rms(o + of_ref[b, rs, hs], ng_ref[:, hs])
                o_ref[b, rs, hs] = (y * gate[:, hs]).astype(BF16)


def _hgrn(zh, lb_fwd, lb_bwd, norm_gain):
    b, s, _ = zh.shape
    nb = s // HGRN_BLOCK
    group = 2 if b % 2 == 0 else 1
    state = pltpu.VMEM((group, HGRN_HEADS, HGRN_DIM, HGRN_DIM), F32)

    def col(c, rev):
        if rev:
            return pl.BlockSpec((group, HGRN_BLOCK, HGRN_W), lambda i, j: (i, nb - 1 - j, c))
        return pl.BlockSpec((group, HGRN_BLOCK, HGRN_W), lambda i, j: (i, j, c))

    pair_fwd, pair_bwd = _hgrn_pair_masks(False), _hgrn_pair_masks(True)
    o_fwd = pl.pallas_call(
        _hgrn_fwd_kernel,
        grid=(b // group, nb),
        in_specs=[col(0, False), col(1, False), col(3, False), _full(lb_fwd.shape), _full(pair_fwd.shape)],
        out_specs=col(0, False),
        out_shape=jax.ShapeDtypeStruct((b, s, HGRN_W), F32),
        scratch_shapes=[state],
        compiler_params=_params("parallel", "arbitrary"),
        name="hgrn_fwd",
    )(zh, zh, zh, lb_fwd, pair_fwd)
    return pl.pallas_call(
        _hgrn_bwd_kernel,
        grid=(b // group, nb),
        in_specs=[col(0, True), col(2, True), col(3, True), col(4, True), col(0, True),
                  _full(lb_bwd.shape), _full(norm_gain.shape), _full(pair_bwd.shape)],
        out_specs=col(0, True),
        out_shape=jax.ShapeDtypeStruct((b, s, HGRN_W), BF16),
        scratch_shapes=[state],
        compiler_params=_params("parallel", "arbitrary"),
        name="hgrn_bwd",
    )(zh, zh, zh, zh, o_fwd, lb_bwd, norm_gain, pair_bwd)


MIN_FAST_ROW_SUM = 2.0 ** -60
ONLINE_KEY_CHUNK = 256


def _fast_sweep(k_slice, vt_slice, qst_ref, shift, l8_ref, acc_ref, p_bufs, n_kv, tk):
    w = qst_ref.shape[1]
    l8_ref[...] = jnp.zeros_like(l8_ref)
    acc_ref[...] = jnp.zeros_like(acc_ref)

    def probs(j, slot):
        s = jnp.dot(k_slice(j * tk, tk), qst_ref[...], preferred_element_type=F32)
        p = jnp.exp2(s - shift)
        l8_ref[...] += jnp.sum(p.reshape(tk // 8, 8, w), axis=0)
        p_bufs[slot][...] = p.astype(BF16)

    def weighted(j, slot):
        acc_ref[...] += jnp.dot(vt_slice(j * tk, tk), p_bufs[slot][...], preferred_element_type=F32)

    probs(0, 0)
    for j in range(1, n_kv):
        weighted(j - 1, (j - 1) % 2)
        probs(j, j % 2)
    weighted(n_kv - 1, (n_kv - 1) % 2)


def _online_sweep(k_slice, vt_slice, qst_ref, m_ref, l_ref, acc_ref, n_kv, tk):
    m_ref[...] = jnp.full(m_ref.shape, -jnp.inf, F32)
    l_ref[...] = jnp.zeros_like(l_ref)
    acc_ref[...] = jnp.zeros_like(acc_ref)

    def body(j, carry):
        off = pl.multiple_of(j * tk, tk)
        s = jnp.dot(k_slice(off, tk), qst_ref[...], preferred_element_type=F32)
        m_old = m_ref[...]
        m_new = jnp.maximum(m_old, jnp.max(s, axis=0, keepdims=True))
        alpha = jnp.exp2(m_old - m_new)
        p = jnp.exp2(s - m_new)
        l_ref[...] = alpha * l_ref[...] + jnp.sum(p, axis=0, keepdims=True)
        acc_ref[...] = alpha * acc_ref[...] + jnp.dot(vt_slice(off, tk), p.astype(BF16),
                                                      preferred_element_type=F32)
        m_ref[...] = m_new
        return carry

    lax.fori_loop(0, n_kv, body, 0)


def _finish_sweep(k_slice, vt_slice, qst_ref, m_ref, l_ref, l8_ref, acc_ref, n_keys, tk):
    row_sum = jnp.sum(l8_ref[...], axis=0, keepdims=True)
    l_ref[...] = row_sum
    n_bad = jnp.sum(jnp.where(row_sum >= MIN_FAST_ROW_SUM, 0.0, 1.0))

    @pl.when(n_bad > 0.0)
    def _():
        tk_online = min(tk, ONLINE_KEY_CHUNK)
        _online_sweep(k_slice, vt_slice, qst_ref, m_ref, l_ref, acc_ref, n_keys // tk_online, tk_online)


SUB_TILES = 4


def _attn_scratch(tk, w, d, dv):
    n = SUB_TILES
    return [pltpu.VMEM((n, d, w), BF16), pltpu.VMEM((1, w), F32), pltpu.VMEM((n, 1, w), F32),
            pltpu.VMEM((n, 8, w), F32), pltpu.VMEM((n, dv, w), F32), pltpu.VMEM((tk, w), BF16),
            pltpu.VMEM((tk, w), BF16)]


def _mla_attn_kernel(q_ref, k_ref, vt_ref, kn_ref, o_ref, qst_ref, m_ref, l_ref, l8_ref, acc_ref, p0, p1,
                     *, tq, tk):
    max_key_norm = jnp.sqrt(jnp.max(kn_ref[0, 0], axis=0))
    k_slice = lambda off, n: k_ref[0, 0, pl.ds(off, n), :]
    vt_slice = lambda off, n: vt_ref[0, 0, :, pl.ds(off, n)]
    n_keys = k_ref.shape[2]
    for i in range(SUB_TILES):
        rows = slice(i * tq, (i + 1) * tq)
        qst, l, l8, acc = qst_ref.at[i], l_ref.at[i], l8_ref.at[i], acc_ref.at[i]
        qt = q_ref[0, 0, rows, :].astype(F32).T
        qst[...] = qt.astype(BF16)
        shift = jnp.sqrt(jnp.sum(qt * qt, axis=0, keepdims=True)) * max_key_norm[0:1, 0:1]
        _fast_sweep(k_slice, vt_slice, qst, shift, l8, acc, (p0, p1), n_keys // tk, tk)
        _finish_sweep(k_slice, vt_slice, qst, m_ref, l, l8, acc, n_keys, tk)
        out_t = acc[...] / l[...]
        o_ref[0, rows, :] = out_t.T.astype(BF16)


def _mla_attn(q, k, vt, kn, tq, tk):
    b, h, s, d = q.shape
    dv = vt.shape[2]
    rows = SUB_TILES * tq
    return pl.pallas_call(
        functools.partial(_mla_attn_kernel, tq=tq, tk=tk),
        grid=(b, h, s // rows),
        in_specs=[
            pl.BlockSpec((1, 1, rows, d), lambda i, j, t: (i, j, t, 0)),
            pl.BlockSpec((1, 1, s, d), lambda i, j, t: (i, j, 0, 0)),
            pl.BlockSpec((1, 1, dv, s), lambda i, j, t: (i, j, 0, 0)),
            pl.BlockSpec((1, 1) + kn.shape[2:], lambda i, j, t: (i, j, 0, 0, 0)),
        ],
        out_specs=pl.BlockSpec((1, rows, dv), lambda i, j, t: (i, t, j)),
        out_shape=jax.ShapeDtypeStruct((b, s, h * dv), BF16),
        scratch_shapes=_attn_scratch(tk, tq, d, dv),
        compiler_params=_params("parallel", "parallel", "parallel"),
        name="mla_attn",
    )(q, k, vt, kn)


def _diff_attn_kernel(q_ref, k_ref, vt_ref, kn_ref, lq1_ref, lk1_ref, lq2_ref, lk2_ref, ng_ref, o_ref,
                      qst_ref, m_ref, l_ref, l8_ref, acc_ref, p0, p1, *, tq, tk, lambda_init):
    key_sq = jnp.max(kn_ref[0], axis=0)
    chunk = lax.broadcasted_iota(jnp.int32, key_sq.shape, 1) - 2 * pl.program_id(1)
    max_key_norm = [jnp.sqrt(jnp.max(jnp.where(chunk == c, key_sq, 0.0), axis=1, keepdims=True))[0:1]
                    for c in range(2)]

    lam = (jnp.exp(jnp.sum(lq1_ref[...] * lk1_ref[...], axis=1, keepdims=True))
           - jnp.exp(jnp.sum(lq2_ref[...] * lk2_ref[...], axis=1, keepdims=True)) + lambda_init)
    k_slice = lambda off, n: k_ref[0, pl.ds(off, n), :]
    vt_slice = lambda off, n: vt_ref[0, 0, :, pl.ds(off, n)]
    n_keys = k_ref.shape[1]
    for i in range(SUB_TILES):
        rows = slice(i * tq, (i + 1) * tq)
        qst, l, l8, acc = qst_ref.at[i], l_ref.at[i], l8_ref.at[i], acc_ref.at[i]
        q = q_ref[0, rows, :].astype(F32)
        first = lax.broadcasted_iota(jnp.int32, q.shape, 1) < DIFF_DIM
        q1t = jnp.where(first, q, 0.0).T
        q2t = jnp.where(first, 0.0, q).T
        qst[:, 0:tq] = q1t.astype(BF16)
        qst[:, tq:2 * tq] = q2t.astype(BF16)
        shift = jnp.concatenate([jnp.sqrt(jnp.sum(q1t * q1t, axis=0, keepdims=True)) * max_key_norm[0],
                                 jnp.sqrt(jnp.sum(q2t * q2t, axis=0, keepdims=True)) * max_key_norm[1]],
                                axis=1)
        _fast_sweep(k_slice, vt_slice, qst, shift, l8, acc, (p0, p1), n_keys // tk, tk)
        _finish_sweep(k_slice, vt_slice, qst, m_ref, l, l8, acc, n_keys, tk)
        soft = acc[...] / l[...]
        out_t = soft[:, 0:tq] - lam * soft[:, tq:2 * tq]
        y = _rms(out_t.T, ng_ref[...]) * (1.0 - lambda_init)
        o_ref[0, rows, :] = y.astype(BF16)


def _diff_attn(q, k, vt, kn, lq1, lk1, lq2, lk2, norm_gain, lambda_init, tq, tk):
    b, s, d = q.shape
    h = vt.shape[1]
    dv = vt.shape[2]
    small = _full(lq1.shape)
    rows = SUB_TILES * tq
    return pl.pallas_call(
        functools.partial(_diff_attn_kernel, tq=tq, tk=tk, lambda_init=lambda_init),
        grid=(b, h, s // rows),
        in_specs=[
            pl.BlockSpec((1, rows, dv), lambda i, j, t: (i, t, j)),
            pl.BlockSpec((1, s, dv), lambda i, j, t: (i, 0, j)),
            pl.BlockSpec((1, 1, dv, s), lambda i, j, t: (i, j, 0, 0)),
            pl.BlockSpec((1,) + kn.shape[1:], lambda i, j, t: (i, 0, 0, 0)),
            small, small, small, small, _full(norm_gain.shape),
        ],
        out_specs=pl.BlockSpec((1, rows, dv), lambda i, j, t: (i, t, j)),
        out_shape=jax.ShapeDtypeStruct((b, s, d), BF16),
        scratch_shapes=_attn_scratch(tk, 2 * tq, dv, dv),
        compiler_params=_params("parallel", "parallel", "parallel"),
        name="diff_attn",
    )(q, k, vt, kn, lq1, lk1, lq2, lk2, norm_gain)


def _mix_ffn_kernel(*refs, n_mix, final, th):
    x_ref = refs[0]
    mix_refs = refs[1:1 + n_mix]
    wo_ref, g_ref, wg_ref, wu_ref, wd_ref, gf_ref, out_ref = refs[1 + n_mix:]
    mix = jnp.concatenate([m_ref[...] for m_ref in mix_refs], axis=1)
    x1 = x_ref[...] + jnp.dot(mix, wo_ref[...], preferred_element_type=F32)
    xn = _rms(x1, g_ref[...]).astype(BF16)
    y = x1
    for c in range(wg_ref.shape[1] // th):
        cs = slice(c * th, (c + 1) * th)
        gate = jnp.dot(xn, wg_ref[:, cs], preferred_element_type=F32)
        up = jnp.dot(xn, wu_ref[:, cs], preferred_element_type=F32)
        hid = (gate * jax.nn.sigmoid(gate) * up).astype(BF16)
        y = y + jnp.dot(hid, wd_ref[cs, :], preferred_element_type=F32)
    if final:
        y = _rms(y, gf_ref[...])
    out_ref[...] = y


def _mix_ffn(x, mixes, w_out, gain, wg, wu, wd, final_gain, final, tm, th):
    t, d = x.shape
    assert wg.shape[1] % th == 0 and sum(m.shape[1] for m in mixes) == w_out.shape[0]
    n_mix = len(mixes)
    row = lambda i: (i, 0)
    in_specs = [pl.BlockSpec((tm, d), row)]
    in_specs += [pl.BlockSpec((tm, m.shape[1]), row) for m in mixes]
    in_specs += [_resident(w_out.shape), _full(gain.shape), _resident(wg.shape), _resident(wu.shape),
                 _resident(wd.shape), _full(final_gain.shape)]
    return pl.pallas_call(
        functools.partial(_mix_ffn_kernel, n_mix=n_mix, final=final, th=th),
        grid=(t // tm,),
        in_specs=in_specs,
        out_specs=pl.BlockSpec((tm, d), row),
        out_shape=jax.ShapeDtypeStruct((t, d), F32),
        compiler_params=_params("parallel"),
        name="mix_ffn",
    )(x, *mixes, w_out, gain, wg, wu, wd, final_gain)


def _rope_tables(seq_len):
    dim = MLA_ROPE
    inv = 1.0 / (ROPE_THETA ** (jnp.arange(0, dim, 2, dtype=F32) / dim))
    ang = jnp.arange(seq_len, dtype=F32)[:, None] * inv[None, :]
    cos, sin = jnp.cos(ang), jnp.sin(ang)
    cos_t = jnp.concatenate([cos, cos, cos, cos], axis=1)
    sin_t = jnp.concatenate([-sin, sin, -sin, sin], axis=1)
    return cos_t, sin_t


def _swap_halves(w):
    half = w.shape[-1] // 2
    return jnp.concatenate([w[..., half:], w[..., :half]], axis=-1)


def _pad_heads(w):
    kdim, h, r = w.shape
    return jnp.concatenate([w, jnp.zeros_like(w)], axis=-1).reshape(kdim, h * 2 * r)


def _row(v):
    return v.reshape(1, -1).astype(F32)


def _tiles(b, s):
    assert s % 512 == 0
    tm_proj = 512
    tm_ffn = 1024 if (b * s) % 1024 == 0 else 512
    th_ffn = 256
    return tm_proj, tm_ffn, th_ffn, min(1024, s // SUB_TILES), min(512, s // SUB_TILES), min(4096, s)


def kernel(x, norm_attn, norm_ffn, ffn_w_gate, ffn_w_up, ffn_w_down, ab_w_in, hgrn_lower_bound, hgrn_out_norm,
           mla_q_norm, mla_w_uq, mla_kv_norm, mla_w_ukv, ab_w_out, c_w_in, diff_lambda_q1, diff_lambda_k1,
           diff_lambda_q2, diff_lambda_k2, diff_out_norm, c_w_out, final_norm):
    b, s, d = x.shape
    depth = norm_attn.shape[0]
    assert DIFF_DIM == MLA_ROPE and d == DIFF_HEADS * 2 * DIFF_DIM and d == 2 * HGRN_W
    tm_proj, tm_ffn, th_ffn, tq_mla, tq_diff, tk = _tiles(b, s)

    cos_t, sin_t = _rope_tables(s)
    lower_bounds = jnp.cumsum(jax.nn.softmax(hgrn_lower_bound.astype(F32), axis=1), axis=1)

    for layer in range(depth):
        j = layer // 2
        gain = _row(norm_attn[layer])
        if layer % 2 == 0:
            w_in = ab_w_in[j]
            c_h = 5 * HGRN_W
            c_q = c_h + MLA_Q_LORA
            c_kv = c_q + MLA_KV_LORA
            w_kr = w_in[:, c_kv:]
            w_krs = _swap_halves(w_kr)
            wh = w_in[:, :c_h].astype(BF16)
            wm = jnp.concatenate([w_in[:, c_h:c_kv], w_kr, w_kr, w_krs, w_krs], axis=1).astype(BF16)
            wuq = mla_w_uq[j].reshape(MLA_Q_LORA, MLA_HEADS, MLA_NOPE + MLA_ROPE)
            wqn = wuq[..., :MLA_NOPE].reshape(MLA_Q_LORA, MLA_HEADS * MLA_NOPE).astype(BF16)
            wqr = _pad_heads(wuq[..., MLA_NOPE:]).astype(BF16)
            wqs = _pad_heads(_swap_halves(wuq[..., MLA_NOPE:])).astype(BF16)
            wukv = mla_w_ukv[j].reshape(MLA_KV_LORA, MLA_HEADS, MLA_NOPE + MLA_V)
            wkv = jnp.concatenate([wukv[..., :MLA_NOPE].reshape(MLA_KV_LORA, -1),
                                   wukv[..., MLA_NOPE:].reshape(MLA_KV_LORA, -1)], axis=1).astype(BF16)
            zh, q, k, vt, kn = _proj0(x, gain, wh, wm, _row(mla_q_norm[j]), wqn, wqr, wqs,
                                      _row(mla_kv_norm[j]), wkv, cos_t, sin_t, tm_proj)
            o_a = _hgrn(zh, _row(lower_bounds[0, j]), _row(lower_bounds[1, j]), _row(hgrn_out_norm[j]))
            o_b = _mla_attn(q, k, vt, kn, tq_mla, tk)
            w_out = ab_w_out[j].astype(BF16)
            mixes = [o_a.reshape(b * s, HGRN_W), o_b.reshape(b * s, MLA_HEADS * MLA_V)]
        else:
            lambda_init = 0.8 - 0.6 * math.exp(-0.3 * layer)
            q, k, vt, kn = _proj1(x, gain, c_w_in[j].astype(BF16), cos_t, sin_t, tm_proj)
            o_c = _diff_attn(q, k, vt, kn, _row(diff_lambda_q1[j]), _row(diff_lambda_k1[j]),
                             _row(diff_lambda_q2[j]), _row(diff_lambda_k2[j]), _row(diff_out_norm[j]),
                             lambda_init, tq_diff, tk)
            mixes = [o_c.reshape(b * s, d)]
            w_out = c_w_out[j].astype(BF16)
        x = _mix_ffn(x.reshape(b * s, d), mixes, w_out, _row(norm_ffn[layer]),
                     ffn_w_gate[layer].astype(BF16), ffn_w_up[layer].astype(BF16),
                     ffn_w_down[layer].astype(BF16), _row(final_norm),
                     layer == depth - 1, tm_ffn, th_ffn).reshape(b, s, d)
    return x
```

```python
import functools
import math

import jax
import jax.numpy as jnp
import numpy as np
from jax import lax
from jax.experimental import pallas as pl
from jax.experimental.pallas import tpu as pltpu

F32 = jnp.float32
BF16 = jnp.bfloat16

NORM_EPS = 1e-6
ROPE_THETA = 10000.0
LANES = 128
NORM_SLAB = (8, LANES)
V7X_VMEM_BYTES = 64 * 1024 * 1024
VMEM_LIMIT = V7X_VMEM_BYTES * 7 // 8

HGRN_HEADS = 4
HGRN_DIM = 128
HGRN_W = HGRN_HEADS * HGRN_DIM
HGRN_CHUNK = 64
HGRN_BLOCK = 256

MLA_HEADS = 4
MLA_NOPE = 128
MLA_ROPE = 64
MLA_V = 128
MLA_Q_LORA = 384
MLA_KV_LORA = 256
MLA_QK_PAD = 256
LOG2_E = math.log2(math.e)
MLA_SCALE = (MLA_NOPE + MLA_ROPE) ** -0.5 * LOG2_E

DIFF_HEADS = 8
DIFF_DIM = 64
DIFF_SCALE = DIFF_DIM ** -0.5 * LOG2_E


def _rms(x, gain):
    ms = jnp.mean(x * x, axis=-1, keepdims=True)
    return x * lax.rsqrt(ms + NORM_EPS) * gain


def _params(*sem):
    return pltpu.CompilerParams(dimension_semantics=sem, vmem_limit_bytes=VMEM_LIMIT)


def _full(shape):
    n = len(shape)
    return pl.BlockSpec(shape, lambda *_: (0,) * n)


def _resident(shape):
    n = len(shape)
    return pl.BlockSpec(shape, lambda *_: (0,) * n, pipeline_mode=pl.Buffered(1))


def _row_sq_norm(k, lane_mask=None):
    k2 = k.astype(F32)
    k2 = k2 * k2
    if lane_mask is not None:
        k2 = jnp.where(lane_mask, k2, 0.0)
    return jnp.sum(k2, axis=1, keepdims=True)


def _tile_max(row_values, shape):
    return jnp.broadcast_to(jnp.max(row_values, axis=0, keepdims=True), shape)


def _proj0_kernel(x_ref, g_ref, wh_ref, wm_ref, qg_ref, wqn_ref, wqr_ref, wqs_ref, kvg_ref, wkv_ref,
                  cos_ref, sin_ref, zh_ref, q_ref, k_ref, vt_ref, kn_ref):
    xn = _rms(x_ref[0], g_ref[...]).astype(BF16)
    zh_ref[0] = jnp.dot(xn, wh_ref[...], preferred_element_type=F32)
    zm = jnp.dot(xn, wm_ref[...], preferred_element_type=F32)
    cos = cos_ref[...]
    sin = sin_ref[...]
    c0 = MLA_Q_LORA
    c1 = c0 + MLA_KV_LORA
    cqn = _rms(zm[:, :c0], qg_ref[...]).astype(BF16)
    ckvn = _rms(zm[:, c0:c1], kvg_ref[...]).astype(BF16)
    k_rope = zm[:, c1:c1 + LANES] * cos + zm[:, c1 + LANES:c1 + 2 * LANES] * sin
    k_rope = jnp.where(lax.broadcasted_iota(jnp.int32, k_rope.shape, 1) < MLA_ROPE, k_rope, 0.0).astype(BF16)
    qn = jnp.dot(cqn, wqn_ref[...], preferred_element_type=F32)
    qr = jnp.dot(cqn, wqr_ref[...], preferred_element_type=F32)
    qs = jnp.dot(cqn, wqs_ref[...], preferred_element_type=F32)
    kv = jnp.dot(ckvn, wkv_ref[...], preferred_element_type=F32)
    hw = MLA_HEADS * MLA_NOPE
    rope_sq = _row_sq_norm(k_rope)
    for h in range(MLA_HEADS):
        hs = slice(h * LANES, (h + 1) * LANES)
        q_ref[0, h, :, 0:LANES] = (qn[:, hs] * MLA_SCALE).astype(BF16)
        q_ref[0, h, :, LANES:2 * LANES] = ((qr[:, hs] * cos + qs[:, hs] * sin) * MLA_SCALE).astype(BF16)
        k_nope = kv[:, hs].astype(BF16)
        k_ref[0, h, :, 0:LANES] = k_nope
        k_ref[0, h, :, LANES:2 * LANES] = k_rope
        kn_ref[0, h, 0] = _tile_max(_row_sq_norm(k_nope) + rope_sq, kn_ref.shape[3:])
        vt_ref[0, h] = kv[:, hw + h * LANES:hw + (h + 1) * LANES].T.astype(BF16)


def _proj0(x, gain, wh, wm, qg, wqn, wqr, wqs, kvg, wkv, cos_t, sin_t, tm):
    b, s, d = x.shape
    grid = (b, s // tm)
    return pl.pallas_call(
        _proj0_kernel,
        grid=grid,
        in_specs=[
            pl.BlockSpec((1, tm, d), lambda i, j: (i, j, 0)),
            _full(gain.shape), _resident(wh.shape), _resident(wm.shape), _full(qg.shape), _resident(wqn.shape),
            _resident(wqr.shape), _resident(wqs.shape), _full(kvg.shape), _resident(wkv.shape),
            pl.BlockSpec((tm, LANES), lambda i, j: (j, 0)),
            pl.BlockSpec((tm, LANES), lambda i, j: (j, 0)),
        ],
        out_specs=[
            pl.BlockSpec((1, tm, wh.shape[1]), lambda i, j: (i, j, 0)),
            pl.BlockSpec((1, MLA_HEADS, tm, MLA_QK_PAD), lambda i, j: (i, 0, j, 0)),
            pl.BlockSpec((1, MLA_HEADS, tm, MLA_QK_PAD), lambda i, j: (i, 0, j, 0)),
            pl.BlockSpec((1, MLA_HEADS, MLA_V, tm), lambda i, j: (i, 0, 0, j)),
            pl.BlockSpec((1, MLA_HEADS, 1) + NORM_SLAB, lambda i, j: (i, 0, j, 0, 0)),
        ],
        out_shape=[
            jax.ShapeDtypeStruct((b, s, wh.shape[1]), F32),
            jax.ShapeDtypeStruct((b, MLA_HEADS, s, MLA_QK_PAD), BF16),
            jax.ShapeDtypeStruct((b, MLA_HEADS, s, MLA_QK_PAD), BF16),
            jax.ShapeDtypeStruct((b, MLA_HEADS, MLA_V, s), BF16),
            jax.ShapeDtypeStruct((b, MLA_HEADS, s // tm) + NORM_SLAB, F32),
        ],
        compiler_params=_params("parallel", "parallel"),
        name="proj0",
    )(x, gain, wh, wm, qg, wqn, wqr, wqs, kvg, wkv, cos_t, sin_t)


def _proj1_kernel(x_ref, g_ref, w_ref, cos_ref, sin_ref, sel_ref, q_ref, k_ref, vt_ref, kn_ref):
    d = x_ref.shape[2]
    xn = _rms(x_ref[0], g_ref[...]).astype(BF16)
    z = jnp.dot(xn, w_ref[...], preferred_element_type=F32)
    cos = cos_ref[...]
    sin = sin_ref[...]
    lane = lax.broadcasted_iota(jnp.int32, cos.shape, 1)
    first_half = (lane % DIFF_DIM) < (DIFF_DIM // 2)
    half = DIFF_DIM // 2

    def rope(t):
        partner = jnp.where(first_half, pltpu.roll(t, LANES - half, axis=1), pltpu.roll(t, half, axis=1))
        return t * cos + partner * sin

    k_sq = []
    for h in range(DIFF_HEADS):
        hs = slice(h * LANES, (h + 1) * LANES)
        q_ref[0, :, hs] = (rope(z[:, hs]) * DIFF_SCALE).astype(BF16)
        k = rope(z[:, d + h * LANES:d + (h + 1) * LANES]).astype(BF16)
        k_ref[0, :, hs] = k
        k_sq.append(k * k)
        vt_ref[0, h] = z[:, 2 * d + h * LANES:2 * d + (h + 1) * LANES].T.astype(BF16)
    row_sq = jnp.dot(jnp.concatenate(k_sq, axis=1), sel_ref[...], preferred_element_type=F32)
    kn_ref[0, 0] = _tile_max(row_sq, NORM_SLAB)


def _proj1(x, gain, w, cos_t, sin_t, tm):
    b, s, d = x.shape
    sel = jnp.asarray(np.arange(d)[:, None] // DIFF_DIM == np.arange(LANES)[None, :], BF16)
    return pl.pallas_call(
        _proj1_kernel,
        grid=(b, s // tm),
        in_specs=[
            pl.BlockSpec((1, tm, d), lambda i, j: (i, j, 0)),
            _full(gain.shape), _resident(w.shape),
            pl.BlockSpec((tm, LANES), lambda i, j: (j, 0)),
            pl.BlockSpec((tm, LANES), lambda i, j: (j, 0)),
            _resident(sel.shape),
        ],
        out_specs=[
            pl.BlockSpec((1, tm, d), lambda i, j: (i, j, 0)),
            pl.BlockSpec((1, tm, d), lambda i, j: (i, j, 0)),
            pl.BlockSpec((1, DIFF_HEADS, 2 * DIFF_DIM, tm), lambda i, j: (i, 0, 0, j)),
            pl.BlockSpec((1, 1) + NORM_SLAB, lambda i, j: (i, j, 0, 0)),
        ],
        out_shape=[
            jax.ShapeDtypeStruct((b, s, d), BF16),
            jax.ShapeDtypeStruct((b, s, d), BF16),
            jax.ShapeDtypeStruct((b, DIFF_HEADS, 2 * DIFF_DIM, s), BF16),
            jax.ShapeDtypeStruct((b, s // tm) + NORM_SLAB, F32),
        ],
        compiler_params=_params("parallel", "parallel"),
        name="proj1",
    )(x, gain, w, cos_t, sin_t, sel)


def _hgrn_pair_masks(rev):
    n = HGRN_CHUNK
    t = np.arange(n)[:, None]
    s = np.arange(n)[None, :]
    masks = [t == s]
    m = 1
    while m < n:
        t_blk, s_blk = t // m, s // m
        if rev:
            masks.append((s_blk - t_blk == 1) & (t_blk % 2 == 0))
        else:
            masks.append((t_blk - s_blk == 1) & (t_blk % 2 == 1))
        m *= 2
    return jnp.asarray(np.stack(masks).astype(np.float32))


def _hgrn_chunk(q, k, v, g, pair_ref, st_ref, rev):
    n = HGRN_CHUNK
    row = lax.broadcasted_iota(jnp.int32, g.shape, 0)
    p_sum = g
    total = g
    levels = [(q.astype(BF16), k.astype(BF16))]
    m = 1
    while m < n:
        levels.append(((q * jnp.exp2(p_sum)).astype(BF16), (k * jnp.exp2(total - p_sum)).astype(BF16)))
        in_right = (row % (2 * m)) >= m
        other = jnp.where(in_right, pltpu.roll(total, m, axis=0), pltpu.roll(total, n - m, axis=0))
        p_sum = p_sum + (jnp.where(in_right, 0.0, other) if rev else jnp.where(in_right, other, 0.0))
        total = total + other
        m *= 2
    q_in = (q * jnp.exp2(p_sum)).astype(BF16)
    k_out = (k * jnp.exp2(total - p_sum)).astype(BF16)
    chunk_decay = jnp.exp2(total[0:1])
    vb = v.astype(BF16)
    nt = (((1,), (1,)), ((), ()))
    tn = (((0,), (0,)), ((), ()))
    outs = []
    for h in range(HGRN_HEADS):
        hs = slice(h * HGRN_DIM, (h + 1) * HGRN_DIM)
        a = None
        for lvl, (qt, kt) in enumerate(levels):
            part = lax.dot_general(qt[:, hs], kt[:, hs], nt, preferred_element_type=F32) * pair_ref[lvl]
            a = part if a is None else a + part
        st = st_ref[h]
        o = jnp.dot(a.astype(BF16), vb[:, hs], preferred_element_type=F32)
        o = o + lax.dot_general(q_in[:, hs], st.astype(BF16), nt, preferred_element_type=F32)
        st_ref[h] = st * chunk_decay[:, hs] + lax.dot_general(vb[:, hs], k_out[:, hs], tn,
                                                               preferred_element_type=F32)
        outs.append(o)
    return outs


def _hgrn_gates(qh, fp, lb):
    q = qh * jax.nn.sigmoid(qh) * (HGRN_DIM ** -0.5)
    g = jnp.log2(lb + (1.0 - lb) * jax.nn.sigmoid(fp))
    k = (1.0 - lb) * jax.nn.sigmoid(-fp)
    return q, k, g


def _hgrn_fwd_kernel(q_ref, f_ref, v_ref, lb_ref, pair_ref, o_ref, st_ref):
    @pl.when(pl.program_id(1) == 0)
    def _():
        st_ref[...] = jnp.zeros_like(st_ref)

    lb = lb_ref[...]
    for c in range(HGRN_BLOCK // HGRN_CHUNK):
        rs = slice(c * HGRN_CHUNK, (c + 1) * HGRN_CHUNK)
        for b in range(q_ref.shape[0]):
            q, k, g = _hgrn_gates(q_ref[b, rs, :], f_ref[b, rs, :], lb)
            outs = _hgrn_chunk(q, k, v_ref[b, rs, :], g, pair_ref, st_ref.at[b], False)
            for h, o in enumerate(outs):
                o_ref[b, rs, h * HGRN_DIM:(h + 1) * HGRN_DIM] = o


def _hgrn_bwd_kernel(q_ref, f_ref, v_ref, gate_ref, of_ref, lb_ref, ng_ref, pair_ref, o_ref, st_ref):
    @pl.when(pl.program_id(1) == 0)
    def _():
        st_ref[...] = jnp.zeros_like(st_ref)

    lb = lb_ref[...]
    for c in reversed(range(HGRN_BLOCK // HGRN_CHUNK)):
        rs = slice(c * HGRN_CHUNK, (c + 1) * HGRN_CHUNK)
        for b in range(q_ref.shape[0]):
            q, k, g = _hgrn_gates(q_ref[b, rs, :], f_ref[b, rs, :], lb)
            outs = _hgrn_chunk(q, k, v_ref[b, rs, :], g, pair_ref, st_ref.at[b], True)
            gate = gate_ref[b, rs, :]
            gate = gate * jax.nn.sigmoid(gate)
            for h, o in enumerate(outs):
                hs = slice(h * HGRN_DIM, (h + 1) * HGRN_DIM)
                y = _rms(o + of_ref[b, rs, hs], ng_ref[:, hs])
                o_ref[b, rs, hs] = (y * gate[:, hs]).astype(BF16)


def _hgrn(zh, lb_fwd, lb_bwd, norm_gain):
    b, s, _ = zh.shape
    nb = s // HGRN_BLOCK
    group = next(g for g in (4, 2, 1) if b % g == 0)
    state = pltpu.VMEM((group, HGRN_HEADS, HGRN_DIM, HGRN_DIM), F32)

    def col(c, rev):
        if rev:
            return pl.BlockSpec((group, HGRN_BLOCK, HGRN_W), lambda i, j: (i, nb - 1 - j, c))
        return pl.BlockSpec((group, HGRN_BLOCK, HGRN_W), lambda i, j: (i, j, c))

    pair_fwd, pair_bwd = _hgrn_pair_masks(False), _hgrn_pair_masks(True)
    o_fwd = pl.pallas_call(
        _hgrn_fwd_kernel,
        grid=(b // group, nb),
        in_specs=[col(0, False), col(1, False), col(3, False), _full(lb_fwd.shape), _full(pair_fwd.shape)],
        out_specs=col(0, False),
        out_shape=jax.ShapeDtypeStruct((b, s, HGRN_W), F32),
        scratch_shapes=[state],
        compiler_params=_params("parallel", "arbitrary"),
        name="hgrn_fwd",
    )(zh, zh, zh, lb_fwd, pair_fwd)
    return pl.pallas_call(
        _hgrn_bwd_kernel,
        grid=(b // group, nb),
        in_specs=[col(0, True), col(2, True), col(3, True), col(4, True), col(0, True),
                  _full(lb_bwd.shape), _full(norm_gain.shape), _full(pair_bwd.shape)],
        out_specs=col(0, True),
        out_shape=jax.ShapeDtypeStruct((b, s, HGRN_W), BF16),
        scratch_shapes=[state],
        compiler_params=_params("parallel", "arbitrary"),
        name="hgrn_bwd",
    )(zh, zh, zh, zh, o_fwd, lb_bwd, norm_gain, pair_bwd)


MIN_FAST_ROW_SUM = 2.0 ** -60
ONLINE_KEY_CHUNK = 256


def _fast_sweep(k_slice, vt_slice, qst_ref, shift, l8_ref, acc_ref, p_bufs, n_kv, tk):
    w = qst_ref.shape[1]
    l8_ref[...] = jnp.zeros_like(l8_ref)
    acc_ref[...] = jnp.zeros_like(acc_ref)

    def probs(j, slot):
        s = jnp.dot(k_slice(j * tk, tk), qst_ref[...], preferred_element_type=F32)
        p = jnp.exp2(s - shift)
        l8_ref[...] += jnp.sum(p.reshape(tk // 8, 8, w), axis=0)
        p_bufs[slot][...] = p.astype(BF16)

    def weighted(j, slot):
        acc_ref[...] += jnp.dot(vt_slice(j * tk, tk), p_bufs[slot][...], preferred_element_type=F32)

    probs(0, 0)
    for j in range(1, n_kv):
        weighted(j - 1, (j - 1) % 2)
        probs(j, j % 2)
    weighted(n_kv - 1, (n_kv - 1) % 2)


def _online_sweep(k_slice, vt_slice, qst_ref, m_ref, l_ref, acc_ref, n_kv, tk):
    m_ref[...] = jnp.full(m_ref.shape, -jnp.inf, F32)
    l_ref[...] = jnp.zeros_like(l_ref)
    acc_ref[...] = jnp.zeros_like(acc_ref)

    def body(j, carry):
        off = pl.multiple_of(j * tk, tk)
        s = jnp.dot(k_slice(off, tk), qst_ref[...], preferred_element_type=F32)
        m_old = m_ref[...]
        m_new = jnp.maximum(m_old, jnp.max(s, axis=0, keepdims=True))
        alpha = jnp.exp2(m_old - m_new)
        p = jnp.exp2(s - m_new)
        l_ref[...] = alpha * l_ref[...] + jnp.sum(p, axis=0, keepdims=True)
        acc_ref[...] = alpha * acc_ref[...] + jnp.dot(vt_slice(off, tk), p.astype(BF16),
                                                      preferred_element_type=F32)
        m_ref[...] = m_new
        return carry

    lax.fori_loop(0, n_kv, body, 0)


def _finish_sweep(k_slice, vt_slice, qst_ref, m_ref, l_ref, l8_ref, acc_ref, n_keys, tk):
    row_sum = jnp.sum(l8_ref[...], axis=0, keepdims=True)
    l_ref[...] = row_sum
    n_bad = jnp.sum(jnp.where(row_sum >= MIN_FAST_ROW_SUM, 0.0, 1.0))

    @pl.when(n_bad > 0.0)
    def _():
        tk_online = min(tk, ONLINE_KEY_CHUNK)
        _online_sweep(k_slice, vt_slice, qst_ref, m_ref, l_ref, acc_ref, n_keys // tk_online, tk_online)


MLA_SUB_TILES = 2
DIFF_SUB_TILES = 4


def _attn_scratch(tk, w, d, dv, n):
    return [pltpu.VMEM((n, d, w), BF16), pltpu.VMEM((1, w), F32), pltpu.VMEM((n, 1, w), F32),
            pltpu.VMEM((n, 8, w), F32), pltpu.VMEM((n, dv, w), F32), pltpu.VMEM((tk, w), BF16),
            pltpu.VMEM((tk, w), BF16)]


def _mla_attn_kernel(q_ref, k_ref, vt_ref, kn_ref, o_ref, qst_ref, m_ref, l_ref, l8_ref, acc_ref, p0, p1,
                     *, tq, tk):
    max_key_norm = jnp.sqrt(jnp.max(kn_ref[0, 0], axis=0))
    k_slice = lambda off, n: k_ref[0, 0, pl.ds(off, n), :]
    vt_slice = lambda off, n: vt_ref[0, 0, :, pl.ds(off, n)]
    n_keys = k_ref.shape[2]
    for i in range(qst_ref.shape[0]):
        rows = slice(i * tq, (i + 1) * tq)
        qst, l, l8, acc = qst_ref.at[i], l_ref.at[i], l8_ref.at[i], acc_ref.at[i]
        qt = q_ref[0, 0, rows, :].astype(F32).T
        qst[...] = qt.astype(BF16)
        shift = jnp.sqrt(jnp.sum(qt * qt, axis=0, keepdims=True)) * max_key_norm[0:1, 0:1]
        _fast_sweep(k_slice, vt_slice, qst, shift, l8, acc, (p0, p1), n_keys // tk, tk)
        _finish_sweep(k_slice, vt_slice, qst, m_ref, l, l8, acc, n_keys, tk)
        out_t = acc[...] / l[...]
        o_ref[0, rows, :] = out_t.T.astype(BF16)


def _mla_attn(q, k, vt, kn, tq, tk):
    b, h, s, d = q.shape
    dv = vt.shape[2]
    rows = MLA_SUB_TILES * tq
    return pl.pallas_call(
        functools.partial(_mla_attn_kernel, tq=tq, tk=tk),
        grid=(b, h, s // rows),
        in_specs=[
            pl.BlockSpec((1, 1, rows, d), lambda i, j, t: (i, j, t, 0)),
            pl.BlockSpec((1, 1, s, d), lambda i, j, t: (i, j, 0, 0)),
            pl.BlockSpec((1, 1, dv, s), lambda i, j, t: (i, j, 0, 0)),
            pl.BlockSpec((1, 1) + kn.shape[2:], lambda i, j, t: (i, j, 0, 0, 0)),
        ],
        out_specs=pl.BlockSpec((1, rows, dv), lambda i, j, t: (i, t, j)),
        out_shape=jax.ShapeDtypeStruct((b, s, h * dv), BF16),
        scratch_shapes=_attn_scratch(tk, tq, d, dv, MLA_SUB_TILES),
        compiler_params=_params("parallel", "parallel", "parallel"),
        name="mla_attn",
    )(q, k, vt, kn)


def _diff_attn_kernel(q_ref, k_ref, vt_ref, kn_ref, lq1_ref, lk1_ref, lq2_ref, lk2_ref, ng_ref, o_ref,
                      qst_ref, m_ref, l_ref, l8_ref, acc_ref, p0, p1, *, tq, tk, lambda_init):
    key_sq = jnp.max(kn_ref[0], axis=0)
    chunk = lax.broadcasted_iota(jnp.int32, key_sq.shape, 1) - 2 * pl.program_id(1)
    max_key_norm = [jnp.sqrt(jnp.max(jnp.where(chunk == c, key_sq, 0.0), axis=1, keepdims=True))[0:1]
                    for c in range(2)]

    lam = (jnp.exp(jnp.sum(lq1_ref[...] * lk1_ref[...], axis=1, keepdims=True))
           - jnp.exp(jnp.sum(lq2_ref[...] * lk2_ref[...], axis=1, keepdims=True)) + lambda_init)
    k_slice = lambda off, n: k_ref[0, pl.ds(off, n), :]
    vt_slice = lambda off, n: vt_ref[0, 0, :, pl.ds(off, n)]
    n_keys = k_ref.shape[1]
    for i in range(qst_ref.shape[0]):
        rows = slice(i * tq, (i + 1) * tq)
        qst, l, l8, acc = qst_ref.at[i], l_ref.at[i], l8_ref.at[i], acc_ref.at[i]
        q = q_ref[0, rows, :].astype(F32)
        first = lax.broadcasted_iota(jnp.int32, q.shape, 1) < DIFF_DIM
        q1t = jnp.where(first, q, 0.0).T
        q2t = jnp.where(first, 0.0, q).T
        qst[:, 0:tq] = q1t.astype(BF16)
        qst[:, tq:2 * tq] = q2t.astype(BF16)
        shift = jnp.concatenate([jnp.sqrt(jnp.sum(q1t * q1t, axis=0, keepdims=True)) * max_key_norm[0],
                                 jnp.sqrt(jnp.sum(q2t * q2t, axis=0, keepdims=True)) * max_key_norm[1]],
                                axis=1)
        _fast_sweep(k_slice, vt_slice, qst, shift, l8, acc, (p0, p1), n_keys // tk, tk)
        _finish_sweep(k_slice, vt_slice, qst, m_ref, l, l8, acc, n_keys, tk)
        soft = acc[...] / l[...]
        out_t = soft[:, 0:tq] - lam * soft[:, tq:2 * tq]
        y = _rms(out_t.T, ng_ref[...]) * (1.0 - lambda_init)
        o_ref[0, rows, :] = y.astype(BF16)


def _diff_attn(q, k, vt, kn, lq1, lk1, lq2, lk2, norm_gain, lambda_init, tq, tk):
    b, s, d = q.shape
    h = vt.shape[1]
    dv = vt.shape[2]
    small = _full(lq1.shape)
    rows = DIFF_SUB_TILES * tq
    return pl.pallas_call(
        functools.partial(_diff_attn_kernel, tq=tq, tk=tk, lambda_init=lambda_init),
        grid=(b, h, s // rows),
        in_specs=[
            pl.BlockSpec((1, rows, dv), lambda i, j, t: (i, t, j)),
            pl.BlockSpec((1, s, dv), lambda i, j, t: (i, 0, j)),
            pl.BlockSpec((1, 1, dv, s), lambda i, j, t: (i, j, 0, 0)),
            pl.BlockSpec((1,) + kn.shape[1:], lambda i, j, t: (i, 0, 0, 0)),
            small, small, small, small, _full(norm_gain.shape),
        ],
        out_specs=pl.BlockSpec((1, rows, dv), lambda i, j, t: (i, t, j)),
        out_shape=jax.ShapeDtypeStruct((b, s, d), BF16),
        scratch_shapes=_attn_scratch(tk, 2 * tq, dv, dv, DIFF_SUB_TILES),
        compiler_params=_params("parallel", "parallel", "parallel"),
        name="diff_attn",
    )(q, k, vt, kn, lq1, lk1, lq2, lk2, norm_gain)


def _mix_ffn_kernel(*refs, n_mix, final, th):
    x_ref = refs[0]
    mix_refs = refs[1:1 + n_mix]
    wo_ref, g_ref, wg_ref, wu_ref, wd_ref, gf_ref, out_ref = refs[1 + n_mix:]
    mix = jnp.concatenate([m_ref[...] for m_ref in mix_refs], axis=1)
    x1 = x_ref[...] + jnp.dot(mix, wo_ref[...], preferred_element_type=F32)
    xn = _rms(x1, g_ref[...]).astype(BF16)
    y = x1
    for c in range(wg_ref.shape[1] // th):
        cs = slice(c * th, (c + 1) * th)
        gate = jnp.dot(xn, wg_ref[:, cs], preferred_element_type=F32)
        up = jnp.dot(xn, wu_ref[:, cs], preferred_element_type=F32)
        hid = (gate * jax.nn.sigmoid(gate) * up).astype(BF16)
        y = y + jnp.dot(hid, wd_ref[cs, :], preferred_element_type=F32)
    if final:
        y = _rms(y, gf_ref[...])
    out_ref[...] = y


def _mix_ffn(x, mixes, w_out, gain, wg, wu, wd, final_gain, final, tm, th):
    t, d = x.shape
    assert wg.shape[1] % th == 0 and sum(m.shape[1] for m in mixes) == w_out.shape[0]
    n_mix = len(mixes)
    row = lambda i: (i, 0)
    in_specs = [pl.BlockSpec((tm, d), row)]
    in_specs += [pl.BlockSpec((tm, m.shape[1]), row) for m in mixes]
    in_specs += [_resident(w_out.shape), _full(gain.shape), _resident(wg.shape), _resident(wu.shape),
                 _resident(wd.shape), _full(final_gain.shape)]
    return pl.pallas_call(
        functools.partial(_mix_ffn_kernel, n_mix=n_mix, final=final, th=th),
        grid=(t // tm,),
        in_specs=in_specs,
        out_specs=pl.BlockSpec((tm, d), row),
        out_shape=jax.ShapeDtypeStruct((t, d), F32),
        compiler_params=_params("parallel"),
        name="mix_ffn",
    )(x, *mixes, w_out, gain, wg, wu, wd, final_gain)


def _rope_tables(seq_len):
    dim = MLA_ROPE
    inv = 1.0 / (ROPE_THETA ** (jnp.arange(0, dim, 2, dtype=F32) / dim))
    ang = jnp.arange(seq_len, dtype=F32)[:, None] * inv[None, :]
    cos, sin = jnp.cos(ang), jnp.sin(ang)
    cos_t = jnp.concatenate([cos, cos, cos, cos], axis=1)
    sin_t = jnp.concatenate([-sin, sin, -sin, sin], axis=1)
    return cos_t, sin_t


def _swap_halves(w):
    half = w.shape[-1] // 2
    return jnp.concatenate([w[..., half:], w[..., :half]], axis=-1)


def _pad_heads(w):
    kdim, h, r = w.shape
    return jnp.concatenate([w, jnp.zeros_like(w)], axis=-1).reshape(kdim, h * 2 * r)


def _row(v):
    return v.reshape(1, -1).astype(F32)


def _tiles(b, s):
    assert s % 512 == 0
    tm_proj = 512
    tm_ffn = 1024 if (b * s) % 1024 == 0 else 512
    th_ffn = 256
    return (tm_proj, tm_ffn, th_ffn, min(1024, s // MLA_SUB_TILES), min(512, s // DIFF_SUB_TILES),
            min(4096, s))


def kernel(x, norm_attn, norm_ffn, ffn_w_gate, ffn_w_up, ffn_w_down, ab_w_in, hgrn_lower_bound, hgrn_out_norm,
           mla_q_norm, mla_w_uq, mla_kv_norm, mla_w_ukv, ab_w_out, c_w_in, diff_lambda_q1, diff_lambda_k1,
           diff_lambda_q2, diff_lambda_k2, diff_out_norm, c_w_out, final_norm):
    b, s, d = x.shape
    depth = norm_attn.shape[0]
    assert DIFF_DIM == MLA_ROPE and d == DIFF_HEADS * 2 * DIFF_DIM and d == 2 * HGRN_W
    tm_proj, tm_ffn, th_ffn, tq_mla, tq_diff, tk = _tiles(b, s)

    cos_t, sin_t = _rope_tables(s)
    lower_bounds = jnp.cumsum(jax.nn.softmax(hgrn_lower_bound.astype(F32), axis=1), axis=1)

    for layer in range(depth):
        j = layer // 2
        gain = _row(norm_attn[layer])
        if layer % 2 == 0:
            w_in = ab_w_in[j]
            c_h = 5 * HGRN_W
            c_q = c_h + MLA_Q_LORA
            c_kv = c_q + MLA_KV_LORA
            w_kr = w_in[:, c_kv:]
            w_krs = _swap_halves(w_kr)
            wh = w_in[:, :c_h].astype(BF16)
            wm = jnp.concatenate([w_in[:, c_h:c_kv], w_kr, w_kr, w_krs, w_krs], axis=1).astype(BF16)
            wuq = mla_w_uq[j].reshape(MLA_Q_LORA, MLA_HEADS, MLA_NOPE + MLA_ROPE)
            wqn = wuq[..., :MLA_NOPE].reshape(MLA_Q_LORA, MLA_HEADS * MLA_NOPE).astype(BF16)
            wqr = _pad_heads(wuq[..., MLA_NOPE:]).astype(BF16)
            wqs = _pad_heads(_swap_halves(wuq[..., MLA_NOPE:])).astype(BF16)
            wukv = mla_w_ukv[j].reshape(MLA_KV_LORA, MLA_HEADS, MLA_NOPE + MLA_V)
            wkv = jnp.concatenate([wukv[..., :MLA_NOPE].reshape(MLA_KV_LORA, -1),
                                   wukv[..., MLA_NOPE:].reshape(MLA_KV_LORA, -1)], axis=1).astype(BF16)
            zh, q, k, vt, kn = _proj0(x, gain, wh, wm, _row(mla_q_norm[j]), wqn, wqr, wqs,
                                      _row(mla_kv_norm[j]), wkv, cos_t, sin_t, tm_proj)
            o_a = _hgrn(zh, _row(lower_bounds[0, j]), _row(lower_bounds[1, j]), _row(hgrn_out_norm[j]))
            o_b = _mla_attn(q, k, vt, kn, tq_mla, tk)
            w_out = ab_w_out[j].astype(BF16)
            mixes = [o_a.reshape(b * s, HGRN_W), o_b.reshape(b * s, MLA_HEADS * MLA_V)]
        else:
            lambda_init = 0.8 - 0.6 * math.exp(-0.3 * layer)
            q, k, vt, kn = _proj1(x, gain, c_w_in[j].astype(BF16), cos_t, sin_t, tm_proj)
            o_c = _diff_attn(q, k, vt, kn, _row(diff_lambda_q1[j]), _row(diff_lambda_k1[j]),
                             _row(diff_lambda_q2[j]), _row(diff_lambda_k2[j]), _row(diff_out_norm[j]),
                             lambda_init, tq_diff, tk)
            mixes = [o_c.reshape(b * s, d)]
            w_out = c_w_out[j].astype(BF16)
        x = _mix_ffn(x.reshape(b * s, d), mixes, w_out, _row(norm_ffn[layer]),
                     ffn_w_gate[layer].astype(BF16), ffn_w_up[layer].astype(BF16),
                     ffn_w_down[layer].astype(BF16), _row(final_norm),
                     layer == depth - 1, tm_ffn, th_ffn).reshape(b, s, d)
    return x
```

```python
import functools
import math

import jax
import jax.numpy as jnp
import numpy as np
from jax import lax
from jax.experimental import pallas as pl
from jax.experimental.pallas import tpu as pltpu

F32 = jnp.float32
BF16 = jnp.bfloat16

NORM_EPS = 1e-6
ROPE_THETA = 10000.0
LANES = 128
NORM_SLAB = (8, LANES)
V7X_VMEM_BYTES = 64 * 1024 * 1024
VMEM_LIMIT = V7X_VMEM_BYTES * 7 // 8

HGRN_HEADS = 4
HGRN_DIM = 128
HGRN_W = HGRN_HEADS * HGRN_DIM
HGRN_CHUNK = 64
HGRN_BLOCK = 256

MLA_HEADS = 4
MLA_NOPE = 128
MLA_ROPE = 64
MLA_V = 128
MLA_Q_LORA = 384
MLA_KV_LORA = 256
MLA_QK_PAD = 256
LOG2_E = math.log2(math.e)
MLA_SCALE = (MLA_NOPE + MLA_ROPE) ** -0.5 * LOG2_E

DIFF_HEADS = 8
DIFF_DIM = 64
DIFF_SCALE = DIFF_DIM ** -0.5 * LOG2_E


def _rms(x, gain):
    ms = jnp.mean(x * x, axis=-1, keepdims=True)
    return x * lax.rsqrt(ms + NORM_EPS) * gain


def _params(*sem):
    return pltpu.CompilerParams(dimension_semantics=sem, vmem_limit_bytes=VMEM_LIMIT)


def _full(shape):
    n = len(shape)
    return pl.BlockSpec(shape, lambda *_: (0,) * n)


def _resident(shape):
    n = len(shape)
    return pl.BlockSpec(shape, lambda *_: (0,) * n, pipeline_mode=pl.Buffered(1))


def _row_sq_norm(k, lane_mask=None):
    k2 = k.astype(F32)
    k2 = k2 * k2
    if lane_mask is not None:
        k2 = jnp.where(lane_mask, k2, 0.0)
    return jnp.sum(k2, axis=1, keepdims=True)


def _tile_max(row_values, shape):
    return jnp.broadcast_to(jnp.max(row_values, axis=0, keepdims=True), shape)


def _proj0_kernel(x_ref, g_ref, wh_ref, wm_ref, qg_ref, wqn_ref, wqr_ref, wqs_ref, kvg_ref, wkv_ref,
                  cos_ref, sin_ref, zh_ref, q_ref, k_ref, vt_ref, kn_ref):
    xn = _rms(x_ref[0], g_ref[...]).astype(BF16)
    zh_ref[0] = jnp.dot(xn, wh_ref[...], preferred_element_type=F32)
    zm = jnp.dot(xn, wm_ref[...], preferred_element_type=F32)
    cos = cos_ref[...]
    sin = sin_ref[...]
    c0 = MLA_Q_LORA
    c1 = c0 + MLA_KV_LORA
    cqn = _rms(zm[:, :c0], qg_ref[...]).astype(BF16)
    ckvn = _rms(zm[:, c0:c1], kvg_ref[...]).astype(BF16)
    k_rope = zm[:, c1:c1 + LANES] * cos + zm[:, c1 + LANES:c1 + 2 * LANES] * sin
    k_rope = jnp.where(lax.broadcasted_iota(jnp.int32, k_rope.shape, 1) < MLA_ROPE, k_rope, 0.0).astype(BF16)
    qn = jnp.dot(cqn, wqn_ref[...], preferred_element_type=F32)
    qr = jnp.dot(cqn, wqr_ref[...], preferred_element_type=F32)
    qs = jnp.dot(cqn, wqs_ref[...], preferred_element_type=F32)
    kv = jnp.dot(ckvn, wkv_ref[...], preferred_element_type=F32)
    hw = MLA_HEADS * MLA_NOPE
    rope_sq = _row_sq_norm(k_rope)
    for h in range(MLA_HEADS):
        hs = slice(h * LANES, (h + 1) * LANES)
        q_ref[0, h, :, 0:LANES] = (qn[:, hs] * MLA_SCALE).astype(BF16)
        q_ref[0, h, :, LANES:2 * LANES] = ((qr[:, hs] * cos + qs[:, hs] * sin) * MLA_SCALE).astype(BF16)
        k_nope = kv[:, hs].astype(BF16)
        k_ref[0, h, :, 0:LANES] = k_nope
        k_ref[0, h, :, LANES:2 * LANES] = k_rope
        kn_ref[0, h, 0] = _tile_max(_row_sq_norm(k_nope) + rope_sq, kn_ref.shape[3:])
        vt_ref[0, h] = kv[:, hw + h * LANES:hw + (h + 1) * LANES].T.astype(BF16)


def _proj0(x, gain, wh, wm, qg, wqn, wqr, wqs, kvg, wkv, cos_t, sin_t, tm):
    b, s, d = x.shape
    grid = (b, s // tm)
    return pl.pallas_call(
        _proj0_kernel,
        grid=grid,
        in_specs=[
            pl.BlockSpec((1, tm, d), lambda i, j: (i, j, 0)),
            _full(gain.shape), _resident(wh.shape), _resident(wm.shape), _full(qg.shape), _resident(wqn.shape),
            _resident(wqr.shape), _resident(wqs.shape), _full(kvg.shape), _resident(wkv.shape),
            pl.BlockSpec((tm, LANES), lambda i, j: (j, 0)),
            pl.BlockSpec((tm, LANES), lambda i, j: (j, 0)),
        ],
        out_specs=[
            pl.BlockSpec((1, tm, wh.shape[1]), lambda i, j: (i, j, 0)),
            pl.BlockSpec((1, MLA_HEADS, tm, MLA_QK_PAD), lambda i, j: (i, 0, j, 0)),
            pl.BlockSpec((1, MLA_HEADS, tm, MLA_QK_PAD), lambda i, j: (i, 0, j, 0)),
            pl.BlockSpec((1, MLA_HEADS, MLA_V, tm), lambda i, j: (i, 0, 0, j)),
            pl.BlockSpec((1, MLA_HEADS, 1) + NORM_SLAB, lambda i, j: (i, 0, j, 0, 0)),
        ],
        out_shape=[
            jax.ShapeDtypeStruct((b, s, wh.shape[1]), F32),
            jax.ShapeDtypeStruct((b, MLA_HEADS, s, MLA_QK_PAD), BF16),
            jax.ShapeDtypeStruct((b, MLA_HEADS, s, MLA_QK_PAD), BF16),
            jax.ShapeDtypeStruct((b, MLA_HEADS, MLA_V, s), BF16),
            jax.ShapeDtypeStruct((b, MLA_HEADS, s // tm) + NORM_SLAB, F32),
        ],
        compiler_params=_params("parallel", "parallel"),
        name="proj0",
    )(x, gain, wh, wm, qg, wqn, wqr, wqs, kvg, wkv, cos_t, sin_t)


def _proj1_kernel(x_ref, g_ref, w_ref, cos_ref, sin_ref, sel_ref, q_ref, k_ref, vt_ref, kn_ref):
    d = x_ref.shape[2]
    xn = _rms(x_ref[0], g_ref[...]).astype(BF16)
    z = jnp.dot(xn, w_ref[...], preferred_element_type=F32)
    cos = cos_ref[...]
    sin = sin_ref[...]
    lane = lax.broadcasted_iota(jnp.int32, cos.shape, 1)
    first_half = (lane % DIFF_DIM) < (DIFF_DIM // 2)
    half = DIFF_DIM // 2

    def rope(t):
        partner = jnp.where(first_half, pltpu.roll(t, LANES - half, axis=1), pltpu.roll(t, half, axis=1))
        return t * cos + partner * sin

    k_sq = []
    for h in range(DIFF_HEADS):
        hs = slice(h * LANES, (h + 1) * LANES)
        q_ref[0, :, hs] = (rope(z[:, hs]) * DIFF_SCALE).astype(BF16)
        k = rope(z[:, d + h * LANES:d + (h + 1) * LANES]).astype(BF16)
        k_ref[0, :, hs] = k
        k_sq.append(k * k)
        vt_ref[0, h] = z[:, 2 * d + h * LANES:2 * d + (h + 1) * LANES].T.astype(BF16)
    row_sq = jnp.dot(jnp.concatenate(k_sq, axis=1), sel_ref[...], preferred_element_type=F32)
    kn_ref[0, 0] = _tile_max(row_sq, NORM_SLAB)


def _proj1(x, gain, w, cos_t, sin_t, tm):
    b, s, d = x.shape
    sel = jnp.asarray(np.arange(d)[:, None] // DIFF_DIM == np.arange(LANES)[None, :], BF16)
    return pl.pallas_call(
        _proj1_kernel,
        grid=(b, s // tm),
        in_specs=[
            pl.BlockSpec((1, tm, d), lambda i, j: (i, j, 0)),
            _full(gain.shape), _resident(w.shape),
            pl.BlockSpec((tm, LANES), lambda i, j: (j, 0)),
            pl.BlockSpec((tm, LANES), lambda i, j: (j, 0)),
            _resident(sel.shape),
        ],
        out_specs=[
            pl.BlockSpec((1, tm, d), lambda i, j: (i, j, 0)),
            pl.BlockSpec((1, tm, d), lambda i, j: (i, j, 0)),
            pl.BlockSpec((1, DIFF_HEADS, 2 * DIFF_DIM, tm), lambda i, j: (i, 0, 0, j)),
            pl.BlockSpec((1, 1) + NORM_SLAB, lambda i, j: (i, j, 0, 0)),
        ],
        out_shape=[
            jax.ShapeDtypeStruct((b, s, d), BF16),
            jax.ShapeDtypeStruct((b, s, d), BF16),
            jax.ShapeDtypeStruct((b, DIFF_HEADS, 2 * DIFF_DIM, s), BF16),
            jax.ShapeDtypeStruct((b, s // tm) + NORM_SLAB, F32),
        ],
        compiler_params=_params("parallel", "parallel"),
        name="proj1",
    )(x, gain, w, cos_t, sin_t, sel)


def _hgrn_pair_masks(rev):
    n = HGRN_CHUNK
    t = np.arange(n)[:, None]
    s = np.arange(n)[None, :]
    masks = [t == s]
    m = 1
    while m < n:
        t_blk, s_blk = t // m, s // m
        if rev:
            masks.append((s_blk - t_blk == 1) & (t_blk % 2 == 0))
        else:
            masks.append((t_blk - s_blk == 1) & (t_blk % 2 == 1))
        m *= 2
    return jnp.asarray(np.stack(masks).astype(np.float32))


def _hgrn_chunk(q, k, v, g, pair_ref, st_ref, rev):
    n = HGRN_CHUNK
    row = lax.broadcasted_iota(jnp.int32, g.shape, 0)
    p_sum = g
    total = g
    levels = [(q.astype(BF16), k.astype(BF16))]
    m = 1
    while m < n:
        levels.append(((q * jnp.exp2(p_sum)).astype(BF16), (k * jnp.exp2(total - p_sum)).astype(BF16)))
        in_right = (row % (2 * m)) >= m
        other = jnp.where(in_right, pltpu.roll(total, m, axis=0), pltpu.roll(total, n - m, axis=0))
        p_sum = p_sum + (jnp.where(in_right, 0.0, other) if rev else jnp.where(in_right, other, 0.0))
        total = total + other
        m *= 2
    q_in = (q * jnp.exp2(p_sum)).astype(BF16)
    k_out = (k * jnp.exp2(total - p_sum)).astype(BF16)
    chunk_decay = jnp.exp2(total[0:1])
    vb = v.astype(BF16)
    nt = (((1,), (1,)), ((), ()))
    tn = (((0,), (0,)), ((), ()))
    outs = []
    for h in range(HGRN_HEADS):
        hs = slice(h * HGRN_DIM, (h + 1) * HGRN_DIM)
        a = None
        for lvl, (qt, kt) in enumerate(levels):
            part = lax.dot_general(qt[:, hs], kt[:, hs], nt, preferred_element_type=F32) * pair_ref[lvl]
            a = part if a is None else a + part
        st = st_ref[h]
        o = jnp.dot(a.astype(BF16), vb[:, hs], preferred_element_type=F32)
        o = o + lax.dot_general(q_in[:, hs], st.astype(BF16), nt, preferred_element_type=F32)
        st_ref[h] = st * chunk_decay[:, hs] + lax.dot_general(vb[:, hs], k_out[:, hs], tn,
                                                               preferred_element_type=F32)
        outs.append(o)
    return outs


def _hgrn_gates(qh, fp, lb):
    q = qh * jax.nn.sigmoid(qh) * (HGRN_DIM ** -0.5)
    g = jnp.log2(lb + (1.0 - lb) * jax.nn.sigmoid(fp))
    k = (1.0 - lb) * jax.nn.sigmoid(-fp)
    return q, k, g


def _hgrn_fwd_kernel(q_ref, f_ref, v_ref, lb_ref, pair_ref, o_ref, st_ref):
    @pl.when(pl.program_id(1) == 0)
    def _():
        st_ref[...] = jnp.zeros_like(st_ref)

    lb = lb_ref[...]
    for c in range(HGRN_BLOCK // HGRN_CHUNK):
        rs = slice(c * HGRN_CHUNK, (c + 1) * HGRN_CHUNK)
        for b in range(q_ref.shape[0]):
            q, k, g = _hgrn_gates(q_ref[b, rs, :], f_ref[b, rs, :], lb)
            outs = _hgrn_chunk(q, k, v_ref[b, rs, :], g, pair_ref, st_ref.at[b], False)
            for h, o in enumerate(outs):
                o_ref[b, rs, h * HGRN_DIM:(h + 1) * HGRN_DIM] = o


def _hgrn_bwd_kernel(q_ref, f_ref, v_ref, gate_ref, of_ref, lb_ref, ng_ref, pair_ref, o_ref, st_ref):
    @pl.when(pl.program_id(1) == 0)
    def _():
        st_ref[...] = jnp.zeros_like(st_ref)

    lb = lb_ref[...]
    for c in reversed(range(HGRN_BLOCK // HGRN_CHUNK)):
        rs = slice(c * HGRN_CHUNK, (c + 1) * HGRN_CHUNK)
        for b in range(q_ref.shape[0]):
            q, k, g = _hgrn_gates(q_ref[b, rs, :], f_ref[b, rs, :], lb)
            outs = _hgrn_chunk(q, k, v_ref[b, rs, :], g, pair_ref, st_ref.at[b], True)
            gate = gate_ref[b, rs, :]
            gate = gate * jax.nn.sigmoid(gate)
            for h, o in enumerate(outs):
                hs = slice(h * HGRN_DIM, (h + 1) * HGRN_DIM)
                y = _rms(o + of_ref[b, rs, hs], ng_ref[:, hs])
                o_ref[b, rs, hs] = (y * gate[:, hs]).astype(BF16)


def _hgrn(zh, lb_fwd, lb_bwd, norm_gain):
    b, s, _ = zh.shape
    nb = s // HGRN_BLOCK
    group = 2 if b % 2 == 0 else 1
    state = pltpu.VMEM((group, HGRN_HEADS, HGRN_DIM, HGRN_DIM), F32)

    def col(c, rev):
        if rev:
            return pl.BlockSpec((group, HGRN_BLOCK, HGRN_W), lambda i, j: (i, nb - 1 - j, c))
        return pl.BlockSpec((group, HGRN_BLOCK, HGRN_W), lambda i, j: (i, j, c))

    pair_fwd, pair_bwd = _hgrn_pair_masks(False), _hgrn_pair_masks(True)
    o_fwd = pl.pallas_call(
        _hgrn_fwd_kernel,
        grid=(b // group, nb),
        in_specs=[col(0, False), col(1, False), col(3, False), _full(lb_fwd.shape), _full(pair_fwd.shape)],
        out_specs=col(0, False),
        out_shape=jax.ShapeDtypeStruct((b, s, HGRN_W), F32),
        scratch_shapes=[state],
        compiler_params=_params("parallel", "arbitrary"),
        name="hgrn_fwd",
    )(zh, zh, zh, lb_fwd, pair_fwd)
    return pl.pallas_call(
        _hgrn_bwd_kernel,
        grid=(b // group, nb),
        in_specs=[col(0, True), col(2, True), col(3, True), col(4, True), col(0, True),
                  _full(lb_bwd.shape), _full(norm_gain.shape), _full(pair_bwd.shape)],
        out_specs=col(0, True),
        out_shape=jax.ShapeDtypeStruct((b, s, HGRN_W), BF16),
        scratch_shapes=[state],
        compiler_params=_params("parallel", "arbitrary"),
        name="hgrn_bwd",
    )(zh, zh, zh, zh, o_fwd, lb_bwd, norm_gain, pair_bwd)


MIN_FAST_ROW_SUM = 2.0 ** -60
ONLINE_KEY_CHUNK = 256


def _fast_sweep(k_slice, vt_slice, qst_ref, shift, l8_ref, acc_ref, p_bufs, n_kv, tk):
    w = qst_ref.shape[1]
    l8_ref[...] = jnp.zeros_like(l8_ref)
    acc_ref[...] = jnp.zeros_like(acc_ref)

    def probs(j, slot):
        s = jnp.dot(k_slice(j * tk, tk), qst_ref[...], preferred_element_type=F32)
        p = jnp.exp2(s - shift)
        l8_ref[...] += jnp.sum(p.reshape(tk // 8, 8, w), axis=0)
        p_bufs[slot][...] = p.astype(BF16)

    def weighted(j, slot):
        acc_ref[...] += jnp.dot(vt_slice(j * tk, tk), p_bufs[slot][...], preferred_element_type=F32)

    probs(0, 0)
    for j in range(1, n_kv):
        weighted(j - 1, (j - 1) % 2)
        probs(j, j % 2)
    weighted(n_kv - 1, (n_kv - 1) % 2)


def _online_sweep(k_slice, vt_slice, qst_ref, m_ref, l_ref, acc_ref, n_kv, tk):
    m_ref[...] = jnp.full(m_ref.shape, -jnp.inf, F32)
    l_ref[...] = jnp.zeros_like(l_ref)
    acc_ref[...] = jnp.zeros_like(acc_ref)

    def body(j, carry):
        off = pl.multiple_of(j * tk, tk)
        s = jnp.dot(k_slice(off, tk), qst_ref[...], preferred_element_type=F32)
        m_old = m_ref[...]
        m_new = jnp.maximum(m_old, jnp.max(s, axis=0, keepdims=True))
        alpha = jnp.exp2(m_old - m_new)
        p = jnp.exp2(s - m_new)
        l_ref[...] = alpha * l_ref[...] + jnp.sum(p, axis=0, keepdims=True)
        acc_ref[...] = alpha * acc_ref[...] + jnp.dot(vt_slice(off, tk), p.astype(BF16),
                                                      preferred_element_type=F32)
        m_ref[...] = m_new
        return carry

    lax.fori_loop(0, n_kv, body, 0)


def _finish_sweep(k_slice, vt_slice, qst_ref, m_ref, l_ref, l8_ref, acc_ref, n_keys, tk):
    row_sum = jnp.sum(l8_ref[...], axis=0, keepdims=True)
    l_ref[...] = row_sum
    n_bad = jnp.sum(jnp.where(row_sum >= MIN_FAST_ROW_SUM, 0.0, 1.0))

    @pl.when(n_bad > 0.0)
    def _():
        tk_online = min(tk, ONLINE_KEY_CHUNK)
        _online_sweep(k_slice, vt_slice, qst_ref, m_ref, l_ref, acc_ref, n_keys // tk_online, tk_online)


MLA_SUB_TILES = 2
DIFF_SUB_TILES = 4


def _attn_scratch(tk, w, d, dv, n):
    return [pltpu.VMEM((n, d, w), BF16), pltpu.VMEM((1, w), F32), pltpu.VMEM((n, 1, w), F32),
            pltpu.VMEM((n, 8, w), F32), pltpu.VMEM((n, dv, w), F32), pltpu.VMEM((tk, w), BF16),
            pltpu.VMEM((tk, w), BF16)]


def _mla_attn_kernel(q_ref, k_ref, vt_ref, kn_ref, o_ref, qst_ref, m_ref, l_ref, l8_ref, acc_ref, p0, p1,
                     *, tq, tk):
    max_key_norm = jnp.sqrt(jnp.max(kn_ref[0, 0], axis=0))
    k_slice = lambda off, n: k_ref[0, 0, pl.ds(off, n), :]
    vt_slice = lambda off, n: vt_ref[0, 0, :, pl.ds(off, n)]
    n_keys = k_ref.shape[2]
    for i in range(qst_ref.shape[0]):
        rows = slice(i * tq, (i + 1) * tq)
        qst, l, l8, acc = qst_ref.at[i], l_ref.at[i], l8_ref.at[i], acc_ref.at[i]
        qt = q_ref[0, 0, rows, :].astype(F32).T
        qst[...] = qt.astype(BF16)
        shift = jnp.sqrt(jnp.sum(qt * qt, axis=0, keepdims=True)) * max_key_norm[0:1, 0:1]
        _fast_sweep(k_slice, vt_slice, qst, shift, l8, acc, (p0, p1), n_keys // tk, tk)
        _finish_sweep(k_slice, vt_slice, qst, m_ref, l, l8, acc, n_keys, tk)
        out_t = acc[...] / l[...]
        o_ref[0, rows, :] = out_t.T.astype(BF16)


def _mla_attn(q, k, vt, kn, tq, tk):
    b, h, s, d = q.shape
    dv = vt.shape[2]
    rows = MLA_SUB_TILES * tq
    return pl.pallas_call(
        functools.partial(_mla_attn_kernel, tq=tq, tk=tk),
        grid=(b, h, s // rows),
        in_specs=[
            pl.BlockSpec((1, 1, rows, d), lambda i, j, t: (i, j, t, 0)),
            pl.BlockSpec((1, 1, s, d), lambda i, j, t: (i, j, 0, 0)),
            pl.BlockSpec((1, 1, dv, s), lambda i, j, t: (i, j, 0, 0)),
            pl.BlockSpec((1, 1) + kn.shape[2:], lambda i, j, t: (i, j, 0, 0, 0)),
        ],
        out_specs=pl.BlockSpec((1, rows, dv), lambda i, j, t: (i, t, j)),
        out_shape=jax.ShapeDtypeStruct((b, s, h * dv), BF16),
        scratch_shapes=_attn_scratch(tk, tq, d, dv, MLA_SUB_TILES),
        compiler_params=_params("parallel", "parallel", "parallel"),
        name="mla_attn",
    )(q, k, vt, kn)


def _diff_attn_kernel(q_ref, k_ref, vt_ref, kn_ref, lq1_ref, lk1_ref, lq2_ref, lk2_ref, ng_ref, o_ref,
                      qst_ref, m_ref, l_ref, l8_ref, acc_ref, p0, p1, *, tq, tk, lambda_init):
    key_sq = jnp.max(kn_ref[0], axis=0)
    chunk = lax.broadcasted_iota(jnp.int32, key_sq.shape, 1) - 2 * pl.program_id(1)
    max_key_norm = [jnp.sqrt(jnp.max(jnp.where(chunk == c, key_sq, 0.0), axis=1, keepdims=True))[0:1]
                    for c in range(2)]

    lam = (jnp.exp(jnp.sum(lq1_ref[...] * lk1_ref[...], axis=1, keepdims=True))
           - jnp.exp(jnp.sum(lq2_ref[...] * lk2_ref[...], axis=1, keepdims=True)) + lambda_init)
    k_slice = lambda off, n: k_ref[0, pl.ds(off, n), :]
    vt_slice = lambda off, n: vt_ref[0, 0, :, pl.ds(off, n)]
    n_keys = k_ref.shape[1]
    for i in range(qst_ref.shape[0]):
        rows = slice(i * tq, (i + 1) * tq)
        qst, l, l8, acc = qst_ref.at[i], l_ref.at[i], l8_ref.at[i], acc_ref.at[i]
        q = q_ref[0, rows, :].astype(F32)
        first = lax.broadcasted_iota(jnp.int32, q.shape, 1) < DIFF_DIM
        q1t = jnp.where(first, q, 0.0).T
        q2t = jnp.where(first, 0.0, q).T
        qst[:, 0:tq] = q1t.astype(BF16)
        qst[:, tq:2 * tq] = q2t.astype(BF16)
        shift = jnp.concatenate([jnp.sqrt(jnp.sum(q1t * q1t, axis=0, keepdims=True)) * max_key_norm[0],
                                 jnp.sqrt(jnp.sum(q2t * q2t, axis=0, keepdims=True)) * max_key_norm[1]],
                                axis=1)
        _fast_sweep(k_slice, vt_slice, qst, shift, l8, acc, (p0, p1), n_keys // tk, tk)
        _finish_sweep(k_slice, vt_slice, qst, m_ref, l, l8, acc, n_keys, tk)
        soft = acc[...] / l[...]
        out_t = soft[:, 0:tq] - lam * soft[:, tq:2 * tq]
        y = _rms(out_t.T, ng_ref[...]) * (1.0 - lambda_init)
        o_ref[0, rows, :] = y.astype(BF16)


def _diff_attn(q, k, vt, kn, lq1, lk1, lq2, lk2, norm_gain, lambda_init, tq, tk):
    b, s, d = q.shape
    h = vt.shape[1]
    dv = vt.shape[2]
    small = _full(lq1.shape)
    rows = DIFF_SUB_TILES * tq
    return pl.pallas_call(
        functools.partial(_diff_attn_kernel, tq=tq, tk=tk, lambda_init=lambda_init),
        grid=(b, h, s // rows),
        in_specs=[
            pl.BlockSpec((1, rows, dv), lambda i, j, t: (i, t, j)),
            pl.BlockSpec((1, s, dv), lambda i, j, t: (i, 0, j)),
            pl.BlockSpec((1, 1, dv, s), lambda i, j, t: (i, j, 0, 0)),
            pl.BlockSpec((1,) + kn.shape[1:], lambda i, j, t: (i, 0, 0, 0)),
            small, small, small, small, _full(norm_gain.shape),
        ],
        out_specs=pl.BlockSpec((1, rows, dv), lambda i, j, t: (i, t, j)),
        out_shape=jax.ShapeDtypeStruct((b, s, d), BF16),
        scratch_shapes=_attn_scratch(tk, 2 * tq, dv, dv, DIFF_SUB_TILES),
        compiler_params=_params("parallel", "parallel", "parallel"),
        name="diff_attn",
    )(q, k, vt, kn, lq1, lk1, lq2, lk2, norm_gain)


def _mix_ffn_kernel(*refs, n_mix, final, th):
    x_ref = refs[0]
    mix_refs = refs[1:1 + n_mix]
    wo_ref, g_ref, wg_ref, wu_ref, wd_ref, gf_ref, out_ref = refs[1 + n_mix:]
    mix = jnp.concatenate([m_ref[...] for m_ref in mix_refs], axis=1)
    x1 = x_ref[...] + jnp.dot(mix, wo_ref[...], preferred_element_type=F32)
    xn = _rms(x1, g_ref[...]).astype(BF16)
    y = x1
    for c in range(wg_ref.shape[1] // th):
        cs = slice(c * th, (c + 1) * th)
        gate = jnp.dot(xn, wg_ref[:, cs], preferred_element_type=F32)
        up = jnp.dot(xn, wu_ref[:, cs], preferred_element_type=F32)
        hid = (gate * jax.nn.sigmoid(gate) * up).astype(BF16)
        y = y + jnp.dot(hid, wd_ref[cs, :], preferred_element_type=F32)
    if final:
        y = _rms(y, gf_ref[...])
    out_ref[...] = y


def _mix_ffn(x, mixes, w_out, gain, wg, wu, wd, final_gain, final, tm, th):
    t, d = x.shape
    assert wg.shape[1] % th == 0 and sum(m.shape[1] for m in mixes) == w_out.shape[0]
    n_mix = len(mixes)
    row = lambda i: (i, 0)
    in_specs = [pl.BlockSpec((tm, d), row)]
    in_specs += [pl.BlockSpec((tm, m.shape[1]), row) for m in mixes]
    in_specs += [_resident(w_out.shape), _full(gain.shape), _resident(wg.shape), _resident(wu.shape),
                 _resident(wd.shape), _full(final_gain.shape)]
    return pl.pallas_call(
        functools.partial(_mix_ffn_kernel, n_mix=n_mix, final=final, th=th),
        grid=(t // tm,),
        in_specs=in_specs,
        out_specs=pl.BlockSpec((tm, d), row),
        out_shape=jax.ShapeDtypeStruct((t, d), F32),
        compiler_params=_params("parallel"),
        name="mix_ffn",
    )(x, *mixes, w_out, gain, wg, wu, wd, final_gain)


def _rope_tables(seq_len):
    dim = MLA_ROPE
    inv = 1.0 / (ROPE_THETA ** (jnp.arange(0, dim, 2, dtype=F32) / dim))
    ang = jnp.arange(seq_len, dtype=F32)[:, None] * inv[None, :]
    cos, sin = jnp.cos(ang), jnp.sin(ang)
    cos_t = jnp.concatenate([cos, cos, cos, cos], axis=1)
    sin_t = jnp.concatenate([-sin, sin, -sin, sin], axis=1)
    return cos_t, sin_t


def _swap_halves(w):
    half = w.shape[-1] // 2
    return jnp.concatenate([w[..., half:], w[..., :half]], axis=-1)


def _pad_heads(w):
    kdim, h, r = w.shape
    return jnp.concatenate([w, jnp.zeros_like(w)], axis=-1).reshape(kdim, h * 2 * r)


def _row(v):
    return v.reshape(1, -1).astype(F32)


def _tiles(b, s):
    assert s % 512 == 0
    tm_proj = 512
    tm_ffn = 1024 if (b * s) % 1024 == 0 else 512
    th_ffn = 256
    return (tm_proj, tm_ffn, th_ffn, min(1024, s // MLA_SUB_TILES), min(512, s // DIFF_SUB_TILES),
            min(4096, s))


def kernel(x, norm_attn, norm_ffn, ffn_w_gate, ffn_w_up, ffn_w_down, ab_w_in, hgrn_lower_bound, hgrn_out_norm,
           mla_q_norm, mla_w_uq, mla_kv_norm, mla_w_ukv, ab_w_out, c_w_in, diff_lambda_q1, diff_lambda_k1,
           diff_lambda_q2, diff_lambda_k2, diff_out_norm, c_w_out, final_norm):
    b, s, d = x.shape
    depth = norm_attn.shape[0]
    assert DIFF_DIM == MLA_ROPE and d == DIFF_HEADS * 2 * DIFF_DIM and d == 2 * HGRN_W
    tm_proj, tm_ffn, th_ffn, tq_mla, tq_diff, tk = _tiles(b, s)

    cos_t, sin_t = _rope_tables(s)
    slot_w = jax.nn.softmax(hgrn_lower_bound.astype(F32), axis=1)
    bounds = [slot_w[:, 0]]
    for slot in range(1, slot_w.shape[1]):
        bounds.append(bounds[-1] + slot_w[:, slot])
    lower_bounds = jnp.stack(bounds, axis=1)

    for layer in range(depth):
        j = layer // 2
        gain = _row(norm_attn[layer])
        if layer % 2 == 0:
            w_in = ab_w_in[j]
            c_h = 5 * HGRN_W
            c_q = c_h + MLA_Q_LORA
            c_kv = c_q + MLA_KV_LORA
            w_kr = w_in[:, c_kv:]
            w_krs = _swap_halves(w_kr)
            wh = w_in[:, :c_h].astype(BF16)
            wm = jnp.concatenate([w_in[:, c_h:c_kv], w_kr, w_kr, w_krs, w_krs], axis=1).astype(BF16)
            wuq = mla_w_uq[j].reshape(MLA_Q_LORA, MLA_HEADS, MLA_NOPE + MLA_ROPE)
            wqn = wuq[..., :MLA_NOPE].reshape(MLA_Q_LORA, MLA_HEADS * MLA_NOPE).astype(BF16)
            wqr = _pad_heads(wuq[..., MLA_NOPE:]).astype(BF16)
            wqs = _pad_heads(_swap_halves(wuq[..., MLA_NOPE:])).astype(BF16)
            wukv = mla_w_ukv[j].reshape(MLA_KV_LORA, MLA_HEADS, MLA_NOPE + MLA_V)
            wkv = jnp.concatenate([wukv[..., :MLA_NOPE].reshape(MLA_KV_LORA, -1),
                                   wukv[..., MLA_NOPE:].reshape(MLA_KV_LORA, -1)], axis=1).astype(BF16)
            zh, q, k, vt, kn = _proj0(x, gain, wh, wm, _row(mla_q_norm[j]), wqn, wqr, wqs,
                                      _row(mla_kv_norm[j]), wkv, cos_t, sin_t, tm_proj)
            o_a = _hgrn(zh, _row(lower_bounds[0, j]), _row(lower_bounds[1, j]), _row(hgrn_out_norm[j]))
            o_b = _mla_attn(q, k, vt, kn, tq_mla, tk)
            w_out = ab_w_out[j].astype(BF16)
            mixes = [o_a.reshape(b * s, HGRN_W), o_b.reshape(b * s, MLA_HEADS * MLA_V)]
        else:
            lambda_init = 0.8 - 0.6 * math.exp(-0.3 * layer)
            q, k, vt, kn = _proj1(x, gain, c_w_in[j].astype(BF16), cos_t, sin_t, tm_proj)
            o_c = _diff_attn(q, k, vt, kn, _row(diff_lambda_q1[j]), _row(diff_lambda_k1[j]),
                             _row(diff_lambda_q2[j]), _row(diff_lambda_k2[j]), _row(diff_out_norm[j]),
                             lambda_init, tq_diff, tk)
            mixes = [o_c.reshape(b * s, d)]
            w_out = c_w_out[j].astype(BF16)
        x = _mix_ffn(x.reshape(b * s, d), mixes, w_out, _row(norm_ffn[layer]),
                     ffn_w_gate[layer].astype(BF16), ffn_w_up[layer].astype(BF16),
                     ffn_w_down[layer].astype(BF16), _row(final_norm),
                     layer == depth - 1, tm_ffn, th_ffn).reshape(b, s, d)
    return x
```
